```python
import math
import jax, jax.numpy as jnp
from jax import lax
import numpy as np

D_MODEL = 1024
BATCH = 16
SEQ = 2048
DEPTH = 2

CTX_LEN = 256
GRID_W = 64
W_FOURIER = D_MODEL // 2
N_FOURIER_GROUPS = 4
W_SSM = D_MODEL // 2
SSM_GROUP = 16
N_SSM_GROUPS = W_SSM // SSM_GROUP
SSM_STATE = 64
DT_MIN = 1e-3
DT_MAX = 1e-1
D_FF = ((8 * D_MODEL // 3 + 255) // 256) * 256
N_EXPERTS = 8
TOP_K = 2
D_FF_EXPERT = 7 * D_MODEL // 2
N_DENSE = (DEPTH + 1) // 2
N_MOE = DEPTH // 2
OFF_SSM = W_FOURIER
OFF_GA = W_FOURIER + W_SSM
OFF_GB = OFF_GA + D_MODEL
IN_COLS = OFF_GB + D_MODEL
EPS = 1e-6
POS_BASE = 10000.0

kernel_name = 'fourier_s5_gated_moe_diffusion_block'


def rmsnorm(x, g):
    xf = x.astype(jnp.float32)
    y = xf * lax.rsqrt(jnp.mean(xf * xf, axis=-1, keepdims=True) + EPS)
    return (y * g.astype(jnp.float32)).astype(x.dtype)


def sincos_2d(rows, dim, dtype):
    t = jnp.arange(rows * GRID_W)
    r = (t // GRID_W).astype(jnp.float32)
    col = (t % GRID_W).astype(jnp.float32)
    quarter = dim // 4
    omega = 1.0 / (POS_BASE ** (jnp.arange(quarter, dtype=jnp.float32) / quarter))
    ar = r[:, None] * omega
    ac = col[:, None] * omega
    return jnp.concatenate([jnp.sin(ar), jnp.cos(ar), jnp.sin(ac), jnp.cos(ac)], axis=-1).astype(dtype)


def adaln(cond, w_mod, b_mod):
    return jnp.split(jax.nn.silu(cond) @ w_mod + b_mod, 6, axis=-1)


def fourier_mix(u):
    b, n, _ = u.shape
    ug = u.reshape(b, n, N_FOURIER_GROUPS, W_FOURIER // N_FOURIER_GROUPS).astype(jnp.float32)
    y = jnp.fft.fft2(ug, axes=(1, 3), norm='ortho').real
    return y.reshape(b, n, W_FOURIER).astype(u.dtype)


def zoh_discretise(a_re, a_im, log_dt, b_re, b_im):
    lam = lax.complex(a_re.astype(jnp.float32), a_im.astype(jnp.float32))
    dt = jnp.exp(log_dt.astype(jnp.float32))[:, None]
    lam_dt = lam * dt
    lam_bar = jnp.exp(lam_dt)
    b_mat = lax.complex(b_re.astype(jnp.float32), b_im.astype(jnp.float32))
    b_bar = ((lam_bar - 1.0) / lam)[..., None] * b_mat
    return lam_dt, lam_bar, b_bar


def _scan_op(e1, e2):
    a1, b1 = e1
    a2, b2 = e2
    return a1 * a2, a2 * b1 + b2


def diag_scan(bu, lam_bar, lam_dt, h0):
    n = bu.shape[1]
    a = jnp.broadcast_to(lam_bar, bu.shape)
    _, h = lax.associative_scan(_scan_op, (a, bu), axis=1)
    if h0 is not None:
        steps = jnp.arange(1, n + 1, dtype=jnp.float32)[:, None, None]
        h = h + jnp.exp(lam_dt[None] * steps)[None] * h0[:, None]
    return h


def s5_branch(u_ctx, u_lat, a_re, a_im, log_dt, b_re, b_im, c_re, c_im, d_skip, need_ctx):
    dtype = u_lat.dtype

    def groups(u):
        return u.reshape(u.shape[0], u.shape[1], N_SSM_GROUPS, SSM_GROUP).astype(jnp.float32)

    uc = groups(u_ctx)
    ul = groups(u_lat)
    d = d_skip.astype(jnp.float32).reshape(N_SSM_GROUPS, SSM_GROUP)
    y_lat = d * ul
    y_ctx = d * uc
    for direction in range(2):
        rev = direction == 1
        flip = (lambda t: jnp.flip(t, axis=1)) if rev else (lambda t: t)
        lam_dt, lam_bar, b_bar = zoh_discretise(a_re[direction], a_im[direction], log_dt[direction],
                                                b_re[direction], b_im[direction])
        c_mat = lax.complex(c_re[direction].astype(jnp.float32), c_im[direction].astype(jnp.float32))
        bu_c = jnp.einsum('blgh,gph->blgp', flip(uc).astype(jnp.complex64), b_bar)
        h_c = diag_scan(bu_c, lam_bar, lam_dt, None)
        bu_l = jnp.einsum('blgh,gph->blgp', flip(ul).astype(jnp.complex64), b_bar)
        h_l = diag_scan(bu_l, lam_bar, lam_dt, h_c[:, -1])
        y_lat = y_lat + flip(jnp.einsum('blgp,ghp->blgh', h_l, c_mat).real)
        if need_ctx:
            y_ctx = y_ctx + flip(jnp.einsum('blgp,ghp->blgh', h_c, c_mat).real)

    def finish(y):
        return jax.nn.gelu(y).reshape(y.shape[0], y.shape[1], W_SSM).astype(dtype)

    return (finish(y_ctx) if need_ctx else None), finish(y_lat)


def merge_branches(p, y_s, w_four, w_glu_a, w_glu_b, w_out):
    g_a = jax.nn.sigmoid(p[..., OFF_GA:OFF_GB])
    g_b = jax.nn.sigmoid(p[..., OFF_GB:])
    y_a = fourier_mix(p[..., :OFF_SSM]) @ w_four
    y_b = (y_s @ w_glu_a) * jax.nn.sigmoid(y_s @ w_glu_b)
    return (g_a * y_a + g_b * y_b) @ w_out


def token_mixer(a_ctx, a_lat, w_in, w_four, a_re, a_im, log_dt, b_re, b_im, c_re, c_im,
                d_skip, w_glu_a, w_glu_b, w_out, need_ctx):
    p_lat = a_lat @ w_in
    if need_ctx:
        p_ctx = a_ctx @ w_in
        s_ctx = p_ctx[..., OFF_SSM:OFF_GA]
    else:
        p_ctx = None
        s_ctx = a_ctx @ w_in[:, OFF_SSM:OFF_GA]
    y_ctx, y_lat = s5_branch(s_ctx, p_lat[..., OFF_SSM:OFF_GA], a_re, a_im, log_dt,
                             b_re, b_im, c_re, c_im, d_skip, need_ctx)
    out_lat = merge_branches(p_lat, y_lat, w_four, w_glu_a, w_glu_b, w_out)
    out_ctx = merge_branches(p_ctx, y_ctx, w_four, w_glu_a, w_glu_b, w_out) if need_ctx else None
    return out_ctx, out_lat


def swiglu(t, w_gate, w_up, w_down):
    return (jax.nn.silu(t @ w_gate) * (t @ w_up)) @ w_down


def moe_swiglu(t, w_router, b_router, w_gate, w_up, w_down):
    b, n, d = t.shape
    tok = t.reshape(b * n, d)
    logits = (tok @ w_router).astype(jnp.float32) + b_router.astype(jnp.float32)
    top_v, top_i = lax.top_k(logits, TOP_K)
    top_w = jax.nn.softmax(top_v, axis=-1)
    combine = jnp.sum(jax.nn.one_hot(top_i, N_EXPERTS, dtype=jnp.float32) * top_w[..., None], axis=1)
    combine = combine.astype(tok.dtype)
    out = jnp.zeros_like(tok)
    for e in range(N_EXPERTS):
        out = out + combine[:, e:e + 1] * swiglu(tok, w_gate[e], w_up[e], w_down[e])
    return out.reshape(b, n, d)


def channel_mixer(l, t, ffn_w_gate, ffn_w_up, ffn_w_down,
                  moe_w_router, moe_b_router, moe_w_gate, moe_w_up, moe_w_down):
    i = l // 2
    if l % 2 == 0:
        return swiglu(t, ffn_w_gate[i], ffn_w_up[i], ffn_w_down[i])
    return moe_swiglu(t, moe_w_router[i], moe_b_router[i], moe_w_gate[i], moe_w_up[i], moe_w_down[i])


def setup_inputs(seed: int = 0) -> dict:
    key = jax.random.key(seed)
    ks = jax.random.split(key, 32)
    f32 = jnp.float32

    def nrm(k, shape, scale):
        return scale * jax.random.normal(k, shape, f32)

    G, P, H = N_SSM_GROUPS, SSM_STATE, SSM_GROUP
    a_im_init = math.pi * jnp.arange(P, dtype=f32)
    return {
        'x': nrm(ks[0], (BATCH, SEQ, D_MODEL), 1.0),
        'c': nrm(ks[1], (BATCH, D_MODEL), 1.0),
        'ctx': nrm(ks[2], (BATCH, CTX_LEN, D_MODEL), 1.0),
        'c_ctx': nrm(ks[3], (D_MODEL,), 1.0),
        'w_mod': nrm(ks[4], (DEPTH, D_MODEL, 6 * D_MODEL), 0.5 * D_MODEL ** -0.5),
        'b_mod': nrm(ks[5], (DEPTH, 6 * D_MODEL), 0.02),
        'norm1_g': 1.0 + nrm(ks[6], (DEPTH, D_MODEL), 0.02),
        'norm2_g': 1.0 + nrm(ks[7], (DEPTH, D_MODEL), 0.02),
        'w_in': nrm(ks[8], (DEPTH, D_MODEL, IN_COLS), D_MODEL ** -0.5),
        'w_four': nrm(ks[9], (DEPTH, W_FOURIER, D_MODEL), W_FOURIER ** -0.5),
        'ssm_a_re': -0.5 + nrm(ks[10], (DEPTH, 2, G, P), 0.01),
        'ssm_a_im': a_im_init + nrm(ks[11], (DEPTH, 2, G, P), 0.01),
        'ssm_log_dt': jax.random.uniform(ks[12], (DEPTH, 2, G), f32, math.log(DT_MIN), math.log(DT_MAX)),
        'ssm_b_re': nrm(ks[13], (DEPTH, 2, G, P, H), (2 * H) ** -0.5),
        'ssm_b_im': nrm(ks[14], (DEPTH, 2, G, P, H), (2 * H) ** -0.5),
        'ssm_c_re': nrm(ks[15], (DEPTH, 2, G, H, P), (2 * P) ** -0.5),
        'ssm_c_im': nrm(ks[16], (DEPTH, 2, G, H, P), (2 * P) ** -0.5),
        'ssm_d': nrm(ks[17], (DEPTH, W_SSM), 1.0),
        'w_glu_a': nrm(ks[18], (DEPTH, W_SSM, D_MODEL), W_SSM ** -0.5),
        'w_glu_b': nrm(ks[19], (DEPTH, W_SSM, D_MODEL), W_SSM ** -0.5),
        'w_out': nrm(ks[20], (DEPTH, D_MODEL, D_MODEL), D_MODEL ** -0.5),
        'ffn_w_gate': nrm(ks[21], (N_DENSE, D_MODEL, D_FF), D_MODEL ** -0.5),
        'ffn_w_up': nrm(ks[22], (N_DENSE, D_MODEL, D_FF), D_MODEL ** -0.5),
        'ffn_w_down': nrm(ks[23], (N_DENSE, D_FF, D_MODEL), D_FF ** -0.5),
        'moe_w_router': nrm(ks[24], (N_MOE, D_MODEL, N_EXPERTS), D_MODEL ** -0.5),
        'moe_b_router': nrm(ks[25], (N_MOE, N_EXPERTS), 0.01),
        'moe_w_gate': nrm(ks[26], (N_MOE, N_EXPERTS, D_MODEL, D_FF_EXPERT), D_MODEL ** -0.5),
        'moe_w_up': nrm(ks[27], (N_MOE, N_EXPERTS, D_MODEL, D_FF_EXPERT), D_MODEL ** -0.5),
        'moe_w_down': nrm(ks[28], (N_MOE, N_EXPERTS, D_FF_EXPERT, D_MODEL), D_FF_EXPERT ** -0.5),
        'final_g': 1.0 + nrm(ks[29], (D_MODEL,), 0.02),
    }


def reference(x, c, ctx, c_ctx, w_mod, b_mod, norm1_g, norm2_g, w_in, w_four,
              ssm_a_re, ssm_a_im, ssm_log_dt, ssm_b_re, ssm_b_im, ssm_c_re, ssm_c_im, ssm_d,
              w_glu_a, w_glu_b, w_out, ffn_w_gate, ffn_w_up, ffn_w_down,
              moe_w_router, moe_b_router, moe_w_gate, moe_w_up, moe_w_down, final_g):
    rows = x.shape[1] // GRID_W
    h = x + sincos_2d(rows, D_MODEL, x.dtype)[None]
    z = ctx
    for l in range(DEPTH):
        need_ctx = l < DEPTH - 1
        sh1, sc1, g1, sh2, sc2, g2 = [m[:, None, :] for m in adaln(c, w_mod[l], b_mod[l])]
        csh1, csc1, cg1, csh2, csc2, cg2 = adaln(c_ctx, w_mod[l], b_mod[l])
        a_lat = rmsnorm(h, norm1_g[l]) * (1.0 + sc1) + sh1
        a_ctx = rmsnorm(z, norm1_g[l]) * (1.0 + csc1) + csh1
        o_ctx, o_lat = token_mixer(a_ctx, a_lat, w_in[l], w_four[l],
                                   ssm_a_re[l], ssm_a_im[l], ssm_log_dt[l],
                                   ssm_b_re[l], ssm_b_im[l], ssm_c_re[l], ssm_c_im[l], ssm_d[l],
                                   w_glu_a[l], w_glu_b[l], w_out[l], need_ctx)
        h = h + g1 * o_lat
        f_lat = rmsnorm(h, norm2_g[l]) * (1.0 + sc2) + sh2
        h = h + g2 * channel_mixer(l, f_lat, ffn_w_gate, ffn_w_up, ffn_w_down,
                                   moe_w_router, moe_b_router, moe_w_gate, moe_w_up, moe_w_down)
        if need_ctx:
            z = z + cg1 * o_ctx
            f_ctx = rmsnorm(z, norm2_g[l]) * (1.0 + csc2) + csh2
            z = z + cg2 * channel_mixer(l, f_ctx, ffn_w_gate, ffn_w_up, ffn_w_down,
                                        moe_w_router, moe_b_router, moe_w_gate, moe_w_up, moe_w_down)
    return rmsnorm(h, final_g)
```

```python
import functools
import math

import numpy as np
import jax
import jax.numpy as jnp
from jax import lax
from jax.experimental import pallas as pl
from jax.experimental.pallas import tpu as pltpu

F32 = jnp.float32
BF16 = jnp.bfloat16

GRID_W = 64
N_FOURIER_GROUPS = 4
FOURIER_GROUP = 128
SSM_GROUP = 16
SSM_STATE = 64
N_EXPERTS = 8
EPS = 1e-6
POS_BASE = 10000.0

CHUNK = 16
LANES = 128
ROUTER_PAD = LANES
NEG_BIG = -1e30
VMEM_LIMIT = 56 * 1024 * 1024


def _cparams(sem):
    return pltpu.CompilerParams(dimension_semantics=sem, vmem_limit_bytes=VMEM_LIMIT)


def _dot(a, b):
    return jnp.dot(a, b, preferred_element_type=F32)


def _split_bf16(a):
    hi = a.astype(BF16)
    lo = (a - hi.astype(F32)).astype(BF16)
    return hi, lo


def _dot3(a, b):
    ah, al = _split_bf16(a)
    bh, bl = _split_bf16(b)
    return _dot(ah, bh) + (_dot(ah, bl) + _dot(al, bh))


def _rms_mod(h, g, sc, sh):
    y = h * lax.rsqrt(jnp.mean(h * h, axis=-1, keepdims=True) + EPS)
    return (y * g) * (1.0 + sc) + sh


def _mods_kernel(c_ref, w_ref, b_ref, o_ref):
    s = jax.nn.silu(c_ref[...])
    o_ref[...] = _dot3(s, w_ref[...]) + b_ref[...]


def _mods_call(cond, w_mod, b_mod):
    r, d = cond.shape
    n = w_mod.shape[1]
    tn = 1024
    return pl.pallas_call(
        _mods_kernel,
        grid=(n // tn,),
        in_specs=[pl.BlockSpec((r, d), lambda j: (0, 0)),
                  pl.BlockSpec((d, tn), lambda j: (0, j)),
                  pl.BlockSpec((1, tn), lambda j: (0, j))],
        out_specs=pl.BlockSpec((r, tn), lambda j: (0, j)),
        out_shape=jax.ShapeDtypeStruct((r, n), F32),
        compiler_params=_cparams(("arbitrary",)),
        name="mods",
    )(cond, w_mod, b_mod.reshape(1, n))


def _inproj_kernel(*refs, acts, has_pos):
    it = iter(refs)
    x_ref = next(it)
    pos_ref = next(it) if has_pos else None
    sc_ref, sh_ref, g_ref, w_ref = next(it), next(it), next(it), next(it)
    h_out = next(it) if has_pos else None
    outs = list(it)
    h = x_ref[...]
    if has_pos:
        h = h + pos_ref[...]
        h_out[...] = h
    a = _rms_mod(h, g_ref[...], sc_ref[...], sh_ref[...]).astype(BF16)
    col = 0
    for o, act in zip(outs, acts):
        width = o.shape[-1]
        p = _dot(a, w_ref[:, col:col + width])
        if act:
            p = jax.nn.sigmoid(p)
        o[...] = p.astype(o.dtype)
        col += width


def _inproj_call(x, pos, sc, sh, g, w, widths, acts, tm):
    b, l, d = x.shape
    has_pos = pos is not None
    tok = pl.BlockSpec((None, tm, d), lambda i, m: (i, m, 0))
    vec = pl.BlockSpec((None, 1, d), lambda i, m: (i, 0, 0))
    in_specs = [tok]
    args = [x]
    if has_pos:
        in_specs.append(pl.BlockSpec((tm, d), lambda i, m: (m, 0)))
        args.append(pos)
    in_specs += [vec, vec, pl.BlockSpec((1, d), lambda i, m: (0, 0)),
                 pl.BlockSpec(w.shape, lambda i, m: (0, 0))]
    args += [sc, sh, g.reshape(1, d), w]
    out_specs, out_shape = [], []
    if has_pos:
        out_specs.append(tok)
        out_shape.append(jax.ShapeDtypeStruct((b, l, d), F32))
    for width in widths:
        out_specs.append(pl.BlockSpec((None, tm, width), lambda i, m: (i, m, 0)))
        out_shape.append(jax.ShapeDtypeStruct((b, l, width), BF16))
    return pl.pallas_call(
        functools.partial(_inproj_kernel, acts=tuple(acts), has_pos=has_pos),
        grid=(b, l // tm),
        in_specs=in_specs, out_specs=out_specs, out_shape=out_shape,
        compiler_params=_cparams(("parallel", "parallel")),
        name="inproj",
    )(*args)


def _dft_tables(l):
    k = np.arange(l, dtype=np.int64)
    ang = 2.0 * np.pi * ((k[:, None] * k[None, :]) % l).astype(np.float64) / l
    dl = np.concatenate([np.cos(ang), -np.sin(ang)], axis=1).astype(np.float32)
    c = np.arange(FOURIER_GROUP, dtype=np.int64)
    angc = 2.0 * np.pi * ((c[:, None] * c[None, :]) % FOURIER_GROUP).astype(np.float64) / FOURIER_GROUP
    return dl, np.cos(angc).astype(np.float32), np.sin(angc).astype(np.float32)


def _fourier_kernel(u_ref, cc_ref, cs_ref, dl_ref, o_ref, v_ref, *, l, scale):
    @pl.when(pl.program_id(1) == 0)
    def _():
        for j in range(N_FOURIER_GROUPS):
            cols = slice(j * FOURIER_GROUP, (j + 1) * FOURIER_GROUP)
            uj = u_ref[:, cols]
            v_ref[0:l, cols] = _dot(uj, cc_ref[...]).astype(BF16)
            v_ref[l:2 * l, cols] = _dot(uj, cs_ref[...]).astype(BF16)

    o_ref[...] = (_dot(dl_ref[...], v_ref[...]) * scale).astype(o_ref.dtype)


def _fourier_call(u, tm):
    b, l, w = u.shape
    dl, cc, cs = _dft_tables(l)
    dl = jnp.asarray(dl).astype(BF16)
    cc = jnp.asarray(cc).astype(BF16)
    cs = jnp.asarray(cs).astype(BF16)
    scale = 1.0 / math.sqrt(l * FOURIER_GROUP)
    return pl.pallas_call(
        functools.partial(_fourier_kernel, l=l, scale=scale),
        grid=(b, l // tm),
        in_specs=[pl.BlockSpec((None, l, w), lambda i, m: (i, 0, 0)),
                  pl.BlockSpec((FOURIER_GROUP, FOURIER_GROUP), lambda i, m: (0, 0)),
                  pl.BlockSpec((FOURIER_GROUP, FOURIER_GROUP), lambda i, m: (0, 0)),
                  pl.BlockSpec((tm, 2 * l), lambda i, m: (m, 0))],
        out_specs=pl.BlockSpec((None, tm, w), lambda i, m: (i, m, 0)),
        out_shape=jax.ShapeDtypeStruct((b, l, w), BF16),
        scratch_shapes=[pltpu.VMEM((2 * l, w), BF16)],
        compiler_params=_cparams(("parallel", "arbitrary")),
        name="fourier",
    )(u, cc, cs, dl)


def _s5_tables(a_re, a_im, log_dt, b_re, b_im, c_re, c_im):
    hp = lax.Precision.HIGHEST
    q = CHUNK
    g = a_re.shape[1]
    dt = jnp.exp(log_dt)[..., None]
    lr, li = a_re * dt, a_im * dt
    em1_r = jnp.expm1(lr) * jnp.cos(li) - 2.0 * jnp.sin(0.5 * li) ** 2
    em1_i = jnp.exp(lr) * jnp.sin(li)
    den = a_re * a_re + a_im * a_im
    fr = (em1_r * a_re + em1_i * a_im) / den
    fi = (em1_i * a_re - em1_r * a_im) / den
    bbr = fr[..., None] * b_re - fi[..., None] * b_im
    bbi = fr[..., None] * b_im + fi[..., None] * b_re
    k = jnp.arange(q + 1, dtype=F32)
    mag = jnp.exp(lr[..., None] * k)
    pr = mag * jnp.cos(li[..., None] * k)
    pi = mag * jnp.sin(li[..., None] * k)

    prk = jnp.moveaxis(pr, -1, 2)[..., None]
    pik = jnp.moveaxis(pi, -1, 2)[..., None]
    wr = prk * bbr[:, :, None] - pik * bbi[:, :, None]
    wi = prk * bbi[:, :, None] + pik * bbr[:, :, None]
    kern = (jnp.einsum('dgop,dgtph->dgtoh', c_re, wr[:, :, :q], precision=hp)
            - jnp.einsum('dgop,dgtph->dgtoh', c_im, wi[:, :, :q], precision=hp))
    qi = jnp.arange(q)
    tau_f = qi[None, :] - qi[:, None]
    m_f = jnp.where((tau_f >= 0)[None, :, :, None, None], kern[0][:, jnp.clip(tau_f, 0, q - 1)], 0.0)
    m_b = jnp.where((tau_f <= 0)[None, :, :, None, None], kern[1][:, jnp.clip(-tau_f, 0, q - 1)], 0.0)
    mt = (m_f + m_b).transpose(0, 1, 4, 2, 3).reshape(g, q * SSM_GROUP, q * SSM_GROUP)

    def inject(w):
        wf = w[0][:, q - 1 - qi]
        wb = w[1][:, qi]
        both = jnp.stack([wf, wb])
        return both.transpose(0, 1, 2, 4, 3).reshape(2, g, q * SSM_GROUP, SSM_STATE)

    def pair_rows(x):
        x = x.reshape(2, g // 2, 2, q * SSM_GROUP, SSM_STATE)
        z = jnp.zeros_like(x[:, :, 0])
        top = jnp.concatenate([x[:, :, 0], z], axis=-1)
        bot = jnp.concatenate([z, x[:, :, 1]], axis=-1)
        return jnp.concatenate([top, bot], axis=-2)

    wre, wim = pair_rows(inject(wr)), pair_rows(inject(wi))

    def carry_out(sign):
        outs = []
        for d, idx in ((0, qi + 1), (1, q - qi)):
            ppr = pr[d][:, :, idx]
            ppi = pi[d][:, :, idx]
            cr = c_re[d].transpose(0, 2, 1)[:, :, None, :]
            ci = c_im[d].transpose(0, 2, 1)[:, :, None, :]
            if sign > 0:
                val = cr * ppr[..., None] - ci * ppi[..., None]
            else:
                val = -(cr * ppi[..., None] + ci * ppr[..., None])
            outs.append(val.reshape(g, SSM_STATE, q * SSM_GROUP))
        return jnp.stack(outs)

    def pair_cols(x):
        x = x.reshape(2, g // 2, 2, SSM_STATE, q * SSM_GROUP)
        z = jnp.zeros_like(x[:, :, 0])
        top = jnp.concatenate([x[:, :, 0], z], axis=-1)
        bot = jnp.concatenate([z, x[:, :, 1]], axis=-1)
        return jnp.concatenate([top, bot], axis=-2)

    cre, cim = pair_cols(carry_out(+1)), pair_cols(carry_out(-1))
    are = pr[..., q].reshape(2, g // 2, 1, 2 * SSM_STATE)
    aim = pi[..., q].reshape(2, g // 2, 1, 2 * SSM_STATE)
    return (mt.astype(BF16), wre.astype(BF16), wim.astype(BF16), are, aim,
            cre.astype(BF16), cim.astype(BF16))


def _s5_kernel(ut_ref, mt_ref, wre_ref, wim_ref, are_ref, aim_ref, cre_ref, cim_ref, o_ref,
               sre_ref, sim_ref, *, batch, n_ctx_chunks, n_chunks):
    half = CHUNK * SSM_GROUP
    u0 = ut_ref[0]
    u1 = ut_ref[1]
    for d in range(2):
        sre_ref[d] = _dot(u0, wre_ref[d, :half]) + _dot(u1, wre_ref[d, half:])
        sim_ref[d] = _dot(u0, wim_ref[d, :half]) + _dot(u1, wim_ref[d, half:])

    afr, afi = are_ref[0], aim_ref[0]
    abr, abi = are_ref[1], aim_ref[1]

    def swap(d, row, sr, si):
        rows = pl.ds(pl.multiple_of(row, batch), batch)
        lr = sre_ref[d, rows, :]
        li = sim_ref[d, rows, :]
        sre_ref[d, rows, :] = sr
        sim_ref[d, rows, :] = si
        return lr, li

    def step(i, carry):
        fr, fi, br, bi = carry
        lr, li = swap(0, i * batch, fr, fi)
        nfr = afr * fr - afi * fi + lr
        nfi = afr * fi + afi * fr + li
        jb = jnp.where(i < n_ctx_chunks, n_ctx_chunks - 1 - i, n_chunks + n_ctx_chunks - 1 - i)
        lr, li = swap(1, jb * batch, br, bi)
        nbr = abr * br - abi * bi + lr
        nbi = abr * bi + abi * br + li
        return nfr, nfi, nbr, nbi

    z = jnp.zeros((batch, 2 * SSM_STATE), F32)
    lax.fori_loop(0, n_chunks, step, (z, z, z, z))

    yc = _dot(sre_ref[0].astype(BF16), cre_ref[0]) + _dot(sim_ref[0].astype(BF16), cim_ref[0])
    yc = yc + (_dot(sre_ref[1].astype(BF16), cre_ref[1]) + _dot(sim_ref[1].astype(BF16), cim_ref[1]))
    o_ref[:, :half] = (yc[:, :half] + _dot(u0, mt_ref[0])).astype(o_ref.dtype)
    o_ref[:, half:] = (yc[:, half:] + _dot(u1, mt_ref[1])).astype(o_ref.dtype)


def _s5_call(u_ctx, u_lat, tables):
    mt, wre, wim, are, aim, cre, cim = tables
    b, lc, w = u_ctx.shape
    l = u_lat.shape[1]
    g = w // SSM_GROUP
    n_chunks = (lc + l) // CHUNK
    rows = n_chunks * b
    half = CHUNK * SSM_GROUP
    u_all = jnp.concatenate([u_ctx, u_lat], axis=1)
    ut = u_all.reshape(b, n_chunks, CHUNK, g, SSM_GROUP).transpose(3, 1, 0, 2, 4).reshape(g, rows, half)
    pair3 = lambda shape: pl.BlockSpec((None,) + shape, lambda i: (i, 0, 0))
    dpair = lambda shape: pl.BlockSpec((2, None) + shape, lambda i: (0, i, 0, 0))
    y = pl.pallas_call(
        functools.partial(_s5_kernel, batch=b, n_ctx_chunks=lc // CHUNK, n_chunks=n_chunks),
        grid=(g // 2,),
        in_specs=[pl.BlockSpec((2, rows, half), lambda i: (i, 0, 0)),
                  pl.BlockSpec((2, half, half), lambda i: (i, 0, 0)),
                  dpair((2 * half, 2 * SSM_STATE)), dpair((2 * half, 2 * SSM_STATE)),
                  dpair((1, 2 * SSM_STATE)), dpair((1, 2 * SSM_STATE)),
                  dpair((2 * SSM_STATE, 2 * half)), dpair((2 * SSM_STATE, 2 * half))],
        out_specs=pair3((rows, 2 * half)),
        out_shape=jax.ShapeDtypeStruct((g // 2, rows, 2 * half), BF16),
        scratch_shapes=[pltpu.VMEM((2, rows, 2 * SSM_STATE), F32),
                        pltpu.VMEM((2, rows, 2 * SSM_STATE), F32)],
        compiler_params=_cparams(("parallel",)),
        name="s5",
    )(ut, mt, wre, wim, are, aim, cre, cim)
    y_all = (y.reshape(g // 2, n_chunks, b, 2, CHUNK, SSM_GROUP)
             .transpose(2, 1, 4, 0, 3, 5).reshape(b, lc + l, w))
    return y_all[:, :lc], y_all[:, lc:]


def _merge_kernel(*refs, with_router):
    (h_ref, yf_ref, yc_ref, u_ref, ga_ref, gb_ref, dsk_ref, g1_ref, sc2_ref, sh2_ref, n2g_ref,
     wf_ref, wa_ref, wb_ref, wo_ref) = refs[:15]
    rest = refs[15:]
    if with_router:
        wr_ref, br_ref, h1_ref, f_ref, lg_ref = rest
    else:
        h1_ref, f_ref = rest
    ys = jax.nn.gelu(dsk_ref[...] * u_ref[...].astype(F32) + yc_ref[...].astype(F32)).astype(BF16)
    ya = _dot(yf_ref[...], wf_ref[...])
    yb = _dot(ys, wa_ref[...]) * jax.nn.sigmoid(_dot(ys, wb_ref[...]))
    m = (ga_ref[...].astype(F32) * ya + gb_ref[...].astype(F32) * yb).astype(BF16)
    h1 = h_ref[...] + g1_ref[...] * _dot(m, wo_ref[...])
    h1_ref[...] = h1
    f = _rms_mod(h1, n2g_ref[...], sc2_ref[...], sh2_ref[...])
    f_ref[...] = f.astype(f_ref.dtype)
    if with_router:
        lg_ref[...] = _dot3(f, wr_ref[...]) + br_ref[...]


def _merge_call(h, yf, yc, u, ga, gb, dsk, g1, sc2, sh2, n2g, wf, wa, wb, wo, router, tm):
    b, l, d = h.shape
    w = yf.shape[-1]
    tok = lambda width: pl.BlockSpec((None, tm, width), lambda i, m: (i, m, 0))
    vec = pl.BlockSpec((None, 1, d), lambda i, m: (i, 0, 0))
    full = lambda a: pl.BlockSpec(a.shape, lambda i, m: (0,) * a.ndim)
    dsk = dsk.reshape(1, w)
    n2g = n2g.reshape(1, d)
    args = [h, yf, yc, u, ga, gb, dsk, g1, sc2, sh2, n2g, wf, wa, wb, wo]
    in_specs = [tok(d), tok(w), tok(w), tok(w), tok(d), tok(d), full(dsk), vec, vec, vec, full(n2g),
                full(wf), full(wa), full(wb), full(wo)]
    out_specs = [tok(d), tok(d)]
    out_shape = [jax.ShapeDtypeStruct((b, l, d), F32), jax.ShapeDtypeStruct((b, l, d), BF16)]
    if router is not None:
        wr, br = router
        args += [wr, br]
        in_specs += [full(wr), full(br)]
        out_specs.append(tok(ROUTER_PAD))
        out_shape.append(jax.ShapeDtypeStruct((b, l, ROUTER_PAD), F32))
    return pl.pallas_call(
        functools.partial(_merge_kernel, with_router=router is not None),
        grid=(b, l // tm),
        in_specs=in_specs, out_specs=out_specs, out_shape=out_shape,
        compiler_params=_cparams(("parallel", "parallel")),
        name="merge",
    )(*args)


def _route_kernel(lg_ref, comb_ref):
    lg = lg_ref[...]
    lane = lax.broadcasted_iota(jnp.int32, lg.shape, 1)
    m1 = jnp.max(lg, axis=-1, keepdims=True)
    i1 = jnp.min(jnp.where(lg == m1, lane, ROUTER_PAD), axis=-1, keepdims=True)
    lg2 = jnp.where(lane == i1, -jnp.inf, lg)
    m2 = jnp.max(lg2, axis=-1, keepdims=True)
    i2 = jnp.min(jnp.where(lg2 == m2, lane, ROUTER_PAD), axis=-1, keepdims=True)
    e = jnp.exp(m2 - m1)
    w1 = 1.0 / (1.0 + e)
    w2 = e * w1
    comb_ref[...] = jnp.where(lane == i1, w1, 0.0) + jnp.where(lane == i2, w2, 0.0)


def _route_call(logits, tm):
    t, n = logits.shape
    return pl.pallas_call(
        _route_kernel,
        grid=(t // tm,),
        in_specs=[pl.BlockSpec((tm, n), lambda i: (i, 0))],
        out_specs=pl.BlockSpec((tm, n), lambda i: (i, 0)),
        out_shape=jax.ShapeDtypeStruct((t, n), F32),
        compiler_params=_cparams(("parallel",)),
        name="route",
    )(logits)


def _ffn_kernel(*refs, n_experts, final_norm):
    if n_experts > 1:
        x_ref, comb_ref, wg_ref, wu_ref, wd_ref, h_ref, g2_ref, fg_ref, o_ref, acc_ref = refs
    else:
        x_ref, wg_ref, wu_ref, wd_ref, h_ref, g2_ref, fg_ref, o_ref, acc_ref = refs
    e = pl.program_id(1)
    j = pl.program_id(2)

    @pl.when((e == 0) & (j == 0))
    def _():
        acc_ref[...] = jnp.zeros_like(acc_ref)

    x = x_ref[...]
    hid = jax.nn.silu(_dot(x, wg_ref[...].astype(BF16))) * _dot(x, wu_ref[...].astype(BF16))
    if n_experts > 1:
        lane = lax.broadcasted_iota(jnp.int32, comb_ref.shape, 1)
        hid = hid * jnp.sum(jnp.where(lane == e, comb_ref[...], 0.0), axis=-1, keepdims=True)
    acc_ref[...] += _dot(hid.astype(BF16), wd_ref[...].astype(BF16))

    @pl.when((e == pl.num_programs(1) - 1) & (j == pl.num_programs(2) - 1))
    def _():
        out = h_ref[...] + g2_ref[...] * acc_ref[...]
        if final_norm:
            out = out * lax.rsqrt(jnp.mean(out * out, axis=-1, keepdims=True) + EPS) * fg_ref[...]
        o_ref[...] = out


def _ffn_call(x, comb, wg, wu, wd, h, g2, final_g, tm, tf):
    t, d = x.shape
    n_e, _, f = wg.shape
    tiles_per_g2 = t // g2.shape[0] // tm
    tok = lambda width: pl.BlockSpec((tm, width), lambda m, e, j: (m, 0))
    args = [x]
    in_specs = [tok(d)]
    if n_e > 1:
        args.append(comb)
        in_specs.append(tok(ROUTER_PAD))
    fg = (final_g if final_g is not None else jnp.ones((d,), F32)).reshape(1, d)
    args += [wg, wu, wd, h, g2, fg]
    in_specs += [pl.BlockSpec((None, d, tf), lambda m, e, j: (e, 0, j)),
                 pl.BlockSpec((None, d, tf), lambda m, e, j: (e, 0, j)),
                 pl.BlockSpec((None, tf, d), lambda m, e, j: (e, j, 0)),
                 tok(d),
                 pl.BlockSpec((None, 1, d), lambda m, e, j: (m // tiles_per_g2, 0, 0)),
                 pl.BlockSpec((1, d), lambda m, e, j: (0, 0))]
    return pl.pallas_call(
        functools.partial(_ffn_kernel, n_experts=n_e, final_norm=final_g is not None),
        grid=(t // tm, n_e, f // tf),
        in_specs=in_specs,
        out_specs=tok(d),
        out_shape=jax.ShapeDtypeStruct((t, d), F32),
        scratch_shapes=[pltpu.VMEM((tm, d), F32)],
        compiler_params=_cparams(("parallel", "arbitrary", "arbitrary")),
        name="ffn",
    )(*args)


def _pos_table(n, dim):
    t = np.arange(n)
    r = (t // GRID_W).astype(np.float32)
    col = (t % GRID_W).astype(np.float32)
    quarter = dim // 4
    omega = (1.0 / (POS_BASE ** (np.arange(quarter, dtype=np.float32) / quarter))).astype(np.float32)
    ar = r[:, None] * omega
    ac = col[:, None] * omega
    return np.concatenate([np.sin(ar), np.cos(ar), np.sin(ac), np.cos(ac)], axis=-1).astype(np.float32)


def _pick_tile(n, pref):
    return pref if n % pref == 0 else n


def kernel(x, c, ctx, c_ctx, w_mod, b_mod, norm1_g, norm2_g, w_in, w_four, ssm_a_re, ssm_a_im, ssm_log_dt, ssm_b_re, ssm_b_im, ssm_c_re, ssm_c_im, ssm_d, w_glu_a, w_glu_b, w_out, ffn_w_gate, ffn_w_up, ffn_w_down, moe_w_router, moe_b_router, moe_w_gate, moe_w_up, moe_w_down, final_g):
    b, l, d = x.shape
    lc = ctx.shape[1]
    depth = w_mod.shape[0]
    wf_cols = w_four.shape[1]
    ws_cols = ssm_d.shape[1]
    off_ga = wf_cols + ws_cols
    tm_lat = _pick_tile(l, 512)
    tm_ctx = _pick_tile(lc, 256)

    n_cond = b + 8
    cond = jnp.concatenate([c, jnp.broadcast_to(c_ctx[None], (n_cond - b, d))], axis=0)
    pos = jnp.asarray(_pos_table(l, d))

    h = x
    z = ctx
    for layer in range(depth):
        need_ctx = layer < depth - 1
        mods = _mods_call(cond, w_mod[layer], b_mod[layer])
        lat_mods = [m.reshape(b, 1, d) for m in jnp.split(mods[:b], 6, axis=-1)]
        ctx_mods = [jnp.broadcast_to(m.reshape(1, 1, d), (b, 1, d)) for m in jnp.split(mods[b:b + 1], 6, axis=-1)]
        sh1, sc1, g1, sh2, sc2, g2 = lat_mods
        csh1, csc1, cg1, csh2, csc2, cg2 = ctx_mods
        w_in_l = w_in[layer].astype(BF16)
        widths = (wf_cols, ws_cols, d, d)
        acts = (False, False, True, True)

        outs = _inproj_call(h, pos if layer == 0 else None, sc1, sh1, norm1_g[layer], w_in_l,
                            widths, acts, tm_lat)
        if layer == 0:
            h, outs = outs[0], outs[1:]
        pf, ps, ga, gb = outs
        if need_ctx:
            cpf, cps, cga, cgb = _inproj_call(z, None, csc1, csh1, norm1_g[layer], w_in_l,
                                              widths, acts, tm_ctx)
        else:
            (cps,) = _inproj_call(z, None, csc1, csh1, norm1_g[layer], w_in_l[:, wf_cols:off_ga],
                                  (ws_cols,), (False,), tm_ctx)

        tables = _s5_tables(ssm_a_re[layer], ssm_a_im[layer], ssm_log_dt[layer], ssm_b_re[layer],
                            ssm_b_im[layer], ssm_c_re[layer], ssm_c_im[layer])
        yc_ctx, yc_lat = _s5_call(cps, ps, tables)
        yf = _fourier_call(pf, tm_lat)

        wf = w_four[layer].astype(BF16)
        wa = w_glu_a[layer].astype(BF16)
        wb = w_glu_b[layer].astype(BF16)
        wo = w_out[layer].astype(BF16)
        moe = layer % 2 == 1
        idx = layer // 2
        router = None
        if moe:
            wr = jnp.zeros((d, ROUTER_PAD), F32).at[:, :N_EXPERTS].set(moe_w_router[idx])
            br = jnp.full((1, ROUTER_PAD), NEG_BIG, F32).at[0, :N_EXPERTS].set(moe_b_router[idx])
            router = (wr, br)
        res = _merge_call(h, yf, yc_lat, ps, ga, gb, ssm_d[layer], g1, sc2, sh2, norm2_g[layer],
                          wf, wa, wb, wo, router, tm_lat)
        last = layer == depth - 1
        fin = final_g if last else None
        tm_ffn = _pick_tile(l, 1024)
        if moe:
            h1, f, logits = res
            comb = _route_call(logits.reshape(b * l, ROUTER_PAD), tm_ffn)
            ffn_w = (moe_w_gate[idx], moe_w_up[idx], moe_w_down[idx])
        else:
            h1, f = res
            comb = None
            ffn_w = (ffn_w_gate[idx][None], ffn_w_up[idx][None], ffn_w_down[idx][None])
        tf = _pick_tile(ffn_w[0].shape[-1], 256)
        h = _ffn_call(f.reshape(b * l, d), comb, *ffn_w, h1.reshape(b * l, d), g2, fin,
                      tm_ffn, tf).reshape(b, l, d)

        if need_ctx:
            if moe:
                raise NotImplementedError("context tokens through an expert layer")
            cyf = _fourier_call(cpf, tm_ctx)
            z1, cf = _merge_call(z, cyf, yc_ctx, cps, cga, cgb, ssm_d[layer], cg1, csc2, csh2,
                                 norm2_g[layer], wf, wa, wb, wo, None, tm_ctx)
            z = _ffn_call(cf.reshape(b * lc, d), None, *ffn_w, z1.reshape(b * lc, d), cg2[:1], None,
                          _pick_tile(b * lc, 1024), tf).reshape(b, lc, d)
    return h
```

```python
import functools
import math

import numpy as np
import jax
import jax.numpy as jnp
from jax import lax
from jax.experimental import pallas as pl
from jax.experimental.pallas import tpu as pltpu

F32 = jnp.float32
BF16 = jnp.bfloat16

GRID_W = 64
N_FOURIER_GROUPS = 4
FOURIER_GROUP = 128
SSM_GROUP = 16
SSM_STATE = 64
N_EXPERTS = 8
EPS = 1e-6
POS_BASE = 10000.0

CHUNK = 16
LANES = 128
ROUTER_PAD = LANES
NEG_BIG = -1e30
VMEM_LIMIT = 56 * 1024 * 1024


def _cparams(sem):
    return pltpu.CompilerParams(dimension_semantics=sem, vmem_limit_bytes=VMEM_LIMIT)


def _dot(a, b):
    return jnp.dot(a, b, preferred_element_type=F32)


def _split_bf16(a):
    hi = a.astype(BF16)
    lo = (a - hi.astype(F32)).astype(BF16)
    return hi, lo


def _dot3(a, b):
    ah, al = _split_bf16(a)
    bh, bl = _split_bf16(b)
    return _dot(ah, bh) + (_dot(ah, bl) + _dot(al, bh))


def _rms_mod(h, g, sc, sh):
    y = h * lax.rsqrt(jnp.mean(h * h, axis=-1, keepdims=True) + EPS)
    return (y * g) * (1.0 + sc) + sh


def _mods_kernel(c_ref, w_ref, b_ref, o_ref):
    s = jax.nn.silu(c_ref[...])
    o_ref[...] = _dot3(s, w_ref[...]) + b_ref[...]


def _mods_call(cond, w_mod, b_mod):
    r, d = cond.shape
    n = w_mod.shape[1]
    tn = 1024
    return pl.pallas_call(
        _mods_kernel,
        grid=(n // tn,),
        in_specs=[pl.BlockSpec((r, d), lambda j: (0, 0)),
                  pl.BlockSpec((d, tn), lambda j: (0, j)),
                  pl.BlockSpec((1, tn), lambda j: (0, j))],
        out_specs=pl.BlockSpec((r, tn), lambda j: (0, j)),
        out_shape=jax.ShapeDtypeStruct((r, n), F32),
        compiler_params=_cparams(("arbitrary",)),
        name="mods",
    )(cond, w_mod, b_mod.reshape(1, n))


def _inproj_kernel(*refs, acts, has_pos):
    it = iter(refs)
    x_ref = next(it)
    pos_ref = next(it) if has_pos else None
    sc_ref, sh_ref, g_ref, w_ref = next(it), next(it), next(it), next(it)
    h_out = next(it) if has_pos else None
    outs = list(it)
    h = x_ref[...]
    if has_pos:
        h = h + pos_ref[...]
        h_out[...] = h
    a = _rms_mod(h, g_ref[...], sc_ref[...], sh_ref[...]).astype(BF16)
    col = 0
    for o, act in zip(outs, acts):
        width = o.shape[-1]
        p = _dot(a, w_ref[:, col:col + width])
        if act:
            p = jax.nn.sigmoid(p)
        o[...] = p.astype(o.dtype)
        col += width


def _inproj_call(x, pos, sc, sh, g, w, widths, acts, tm):
    b, l, d = x.shape
    has_pos = pos is not None
    tok = pl.BlockSpec((None, tm, d), lambda i, m: (i, m, 0))
    vec = pl.BlockSpec((None, 1, d), lambda i, m: (i, 0, 0))
    in_specs = [tok]
    args = [x]
    if has_pos:
        in_specs.append(pl.BlockSpec((tm, d), lambda i, m: (m, 0)))
        args.append(pos)
    in_specs += [vec, vec, pl.BlockSpec((1, d), lambda i, m: (0, 0)),
                 pl.BlockSpec(w.shape, lambda i, m: (0, 0))]
    args += [sc, sh, g.reshape(1, d), w]
    out_specs, out_shape = [], []
    if has_pos:
        out_specs.append(tok)
        out_shape.append(jax.ShapeDtypeStruct((b, l, d), F32))
    for width in widths:
        out_specs.append(pl.BlockSpec((None, tm, width), lambda i, m: (i, m, 0)))
        out_shape.append(jax.ShapeDtypeStruct((b, l, width), BF16))
    return pl.pallas_call(
        functools.partial(_inproj_kernel, acts=tuple(acts), has_pos=has_pos),
        grid=(b, l // tm),
        in_specs=in_specs, out_specs=out_specs, out_shape=out_shape,
        compiler_params=_cparams(("parallel", "parallel")),
        name="inproj",
    )(*args)


def _dft_tables(l):
    k = np.arange(l, dtype=np.int64)
    ang = 2.0 * np.pi * ((k[:, None] * k[None, :]) % l).astype(np.float64) / l
    dl = np.concatenate([np.cos(ang), -np.sin(ang)], axis=1).astype(np.float32)
    c = np.arange(FOURIER_GROUP, dtype=np.int64)
    angc = 2.0 * np.pi * ((c[:, None] * c[None, :]) % FOURIER_GROUP).astype(np.float64) / FOURIER_GROUP
    return dl, np.cos(angc).astype(np.float32), np.sin(angc).astype(np.float32)


def _fourier_kernel(u_ref, cc_ref, cs_ref, dl_ref, o_ref, v_ref, *, l, scale):
    @pl.when(pl.program_id(1) == 0)
    def _():
        for j in range(N_FOURIER_GROUPS):
            cols = slice(j * FOURIER_GROUP, (j + 1) * FOURIER_GROUP)
            uj = u_ref[:, cols]
            v_ref[0:l, cols] = _dot(uj, cc_ref[...]).astype(BF16)
            v_ref[l:2 * l, cols] = _dot(uj, cs_ref[...]).astype(BF16)

    o_ref[...] = (_dot(dl_ref[...], v_ref[...]) * scale).astype(o_ref.dtype)


def _fourier_call(u, tm):
    b, l, w = u.shape
    dl, cc, cs = _dft_tables(l)
    dl = jnp.asarray(dl).astype(BF16)
    cc = jnp.asarray(cc).astype(BF16)
    cs = jnp.asarray(cs).astype(BF16)
    scale = 1.0 / math.sqrt(l * FOURIER_GROUP)
    return pl.pallas_call(
        functools.partial(_fourier_kernel, l=l, scale=scale),
        grid=(b, l // tm),
        in_specs=[pl.BlockSpec((None, l, w), lambda i, m: (i, 0, 0)),
                  pl.BlockSpec((FOURIER_GROUP, FOURIER_GROUP), lambda i, m: (0, 0)),
                  pl.BlockSpec((FOURIER_GROUP, FOURIER_GROUP), lambda i, m: (0, 0)),
                  pl.BlockSpec((tm, 2 * l), lambda i, m: (m, 0))],
        out_specs=pl.BlockSpec((None, tm, w), lambda i, m: (i, m, 0)),
        out_shape=jax.ShapeDtypeStruct((b, l, w), BF16),
        scratch_shapes=[pltpu.VMEM((2 * l, w), BF16)],
        compiler_params=_cparams(("parallel", "arbitrary")),
        name="fourier",
    )(u, cc, cs, dl)


def _s5_tables(a_re, a_im, log_dt, b_re, b_im, c_re, c_im):
    hp = lax.Precision.HIGHEST
    q = CHUNK
    g = a_re.shape[1]
    dt = jnp.exp(log_dt)[..., None]
    lr, li = a_re * dt, a_im * dt
    em1_r = jnp.expm1(lr) * jnp.cos(li) - 2.0 * jnp.sin(0.5 * li) ** 2
    em1_i = jnp.exp(lr) * jnp.sin(li)
    den = a_re * a_re + a_im * a_im
    fr = (em1_r * a_re + em1_i * a_im) / den
    fi = (em1_i * a_re - em1_r * a_im) / den
    bbr = fr[..., None] * b_re - fi[..., None] * b_im
    bbi = fr[..., None] * b_im + fi[..., None] * b_re
    k = jnp.arange(q + 1, dtype=F32)
    mag = jnp.exp(lr[..., None] * k)
    pr = mag * jnp.cos(li[..., None] * k)
    pi = mag * jnp.sin(li[..., None] * k)

    prk = jnp.moveaxis(pr, -1, 2)[..., None]
    pik = jnp.moveaxis(pi, -1, 2)[..., None]
    wr = prk * bbr[:, :, None] - pik * bbi[:, :, None]
    wi = prk * bbi[:, :, None] + pik * bbr[:, :, None]
    kern = (jnp.einsum('dgop,dgtph->dgtoh', c_re, wr[:, :, :q], precision=hp)
            - jnp.einsum('dgop,dgtph->dgtoh', c_im, wi[:, :, :q], precision=hp))
    qi = jnp.arange(q)
    tau_f = qi[None, :] - qi[:, None]
    m_f = jnp.where((tau_f >= 0)[None, :, :, None, None], kern[0][:, jnp.clip(tau_f, 0, q - 1)], 0.0)
    m_b = jnp.where((tau_f <= 0)[None, :, :, None, None], kern[1][:, jnp.clip(-tau_f, 0, q - 1)], 0.0)
    mt = (m_f + m_b).transpose(0, 1, 4, 2, 3).reshape(g, q * SSM_GROUP, q * SSM_GROUP)

    def inject(w):
        wf = w[0][:, q - 1 - qi]
        wb = w[1][:, qi]
        both = jnp.stack([wf, wb])
        return both.transpose(0, 1, 2, 4, 3).reshape(2, g, q * SSM_GROUP, SSM_STATE)

    def pair_rows(x):
        x = x.reshape(2, g // 2, 2, q * SSM_GROUP, SSM_STATE)
        z = jnp.zeros_like(x[:, :, 0])
        top = jnp.concatenate([x[:, :, 0], z], axis=-1)
        bot = jnp.concatenate([z, x[:, :, 1]], axis=-1)
        return jnp.concatenate([top, bot], axis=-2)

    wre, wim = pair_rows(inject(wr)), pair_rows(inject(wi))

    def carry_out(sign):
        outs = []
        for d, idx in ((0, qi + 1), (1, q - qi)):
            ppr = pr[d][:, :, idx]
            ppi = pi[d][:, :, idx]
            cr = c_re[d].transpose(0, 2, 1)[:, :, None, :]
            ci = c_im[d].transpose(0, 2, 1)[:, :, None, :]
            if sign > 0:
                val = cr * ppr[..., None] - ci * ppi[..., None]
            else:
                val = -(cr * ppi[..., None] + ci * ppr[..., None])
            outs.append(val.reshape(g, SSM_STATE, q * SSM_GROUP))
        return jnp.stack(outs)

    def pair_cols(x):
        x = x.reshape(2, g // 2, 2, SSM_STATE, q * SSM_GROUP)
        z = jnp.zeros_like(x[:, :, 0])
        top = jnp.concatenate([x[:, :, 0], z], axis=-1)
        bot = jnp.concatenate([z, x[:, :, 1]], axis=-1)
        return jnp.concatenate([top, bot], axis=-2)

    cre, cim = pair_cols(carry_out(+1)), pair_cols(carry_out(-1))
    are = pr[..., q].reshape(2, g // 2, 1, 2 * SSM_STATE)
    aim = pi[..., q].reshape(2, g // 2, 1, 2 * SSM_STATE)
    return (mt.astype(BF16), wre.astype(BF16), wim.astype(BF16), are, aim,
            cre.astype(BF16), cim.astype(BF16))


def _s5_kernel(ut_ref, mt_ref, wre_ref, wim_ref, are_ref, aim_ref, cre_ref, cim_ref, o_ref,
               sre_ref, sim_ref, *, batch, n_ctx_chunks, n_chunks):
    half = CHUNK * SSM_GROUP
    u0 = ut_ref[0]
    u1 = ut_ref[1]
    for d in range(2):
        sre_ref[d] = _dot(u0, wre_ref[d, :half]) + _dot(u1, wre_ref[d, half:])
        sim_ref[d] = _dot(u0, wim_ref[d, :half]) + _dot(u1, wim_ref[d, half:])

    afr, afi = are_ref[0], aim_ref[0]
    abr, abi = are_ref[1], aim_ref[1]

    def swap(d, row, sr, si):
        rows = pl.ds(pl.multiple_of(row, batch), batch)
        lr = sre_ref[d, rows, :]
        li = sim_ref[d, rows, :]
        sre_ref[d, rows, :] = sr
        sim_ref[d, rows, :] = si
        return lr, li

    def step(i, carry):
        fr, fi, br, bi = carry
        lr, li = swap(0, i * batch, fr, fi)
        nfr = afr * fr - afi * fi + lr
        nfi = afr * fi + afi * fr + li
        jb = jnp.where(i < n_ctx_chunks, n_ctx_chunks - 1 - i, n_chunks + n_ctx_chunks - 1 - i)
        lr, li = swap(1, jb * batch, br, bi)
        nbr = abr * br - abi * bi + lr
        nbi = abr * bi + abi * br + li
        return nfr, nfi, nbr, nbi

    z = jnp.zeros((batch, 2 * SSM_STATE), F32)
    lax.fori_loop(0, n_chunks, step, (z, z, z, z))

    yc = _dot(sre_ref[0].astype(BF16), cre_ref[0]) + _dot(sim_ref[0].astype(BF16), cim_ref[0])
    yc = yc + (_dot(sre_ref[1].astype(BF16), cre_ref[1]) + _dot(sim_ref[1].astype(BF16), cim_ref[1]))
    o_ref[:, :half] = (yc[:, :half] + _dot(u0, mt_ref[0])).astype(o_ref.dtype)
    o_ref[:, half:] = (yc[:, half:] + _dot(u1, mt_ref[1])).astype(o_ref.dtype)


def _s5_call(u_ctx, u_lat, tables):
    mt, wre, wim, are, aim, cre, cim = tables
    b, lc, w = u_ctx.shape
    l = u_lat.shape[1]
    g = w // SSM_GROUP
    n_chunks = (lc + l) // CHUNK
    rows = n_chunks * b
    half = CHUNK * SSM_GROUP
    u_all = jnp.concatenate([u_ctx, u_lat], axis=1)
    ut = u_all.reshape(b, n_chunks, CHUNK, g, SSM_GROUP).transpose(3, 1, 0, 2, 4).reshape(g, rows, half)
    pair3 = lambda shape: pl.BlockSpec((None,) + shape, lambda i: (i, 0, 0))
    dpair = lambda shape: pl.BlockSpec((2, None) + shape, lambda i: (0, i, 0, 0))
    y = pl.pallas_call(
        functools.partial(_s5_kernel, batch=b, n_ctx_chunks=lc // CHUNK, n_chunks=n_chunks),
        grid=(g // 2,),
        in_specs=[pl.BlockSpec((2, rows, half), lambda i: (i, 0, 0)),
                  pl.BlockSpec((2, half, half), lambda i: (i, 0, 0)),
                  dpair((2 * half, 2 * SSM_STATE)), dpair((2 * half, 2 * SSM_STATE)),
                  dpair((1, 2 * SSM_STATE)), dpair((1, 2 * SSM_STATE)),
                  dpair((2 * SSM_STATE, 2 * half)), dpair((2 * SSM_STATE, 2 * half))],
        out_specs=pair3((rows, 2 * half)),
        out_shape=jax.ShapeDtypeStruct((g // 2, rows, 2 * half), BF16),
        scratch_shapes=[pltpu.VMEM((2, rows, 2 * SSM_STATE), F32),
                        pltpu.VMEM((2, rows, 2 * SSM_STATE), F32)],
        compiler_params=_cparams(("parallel",)),
        name="s5",
    )(ut, mt, wre, wim, are, aim, cre, cim)
    y_all = (y.reshape(g // 2, n_chunks, b, 2, CHUNK, SSM_GROUP)
             .transpose(2, 1, 4, 0, 3, 5).reshape(b, lc + l, w))
    return y_all[:, :lc], y_all[:, lc:]


def _merge_kernel(*refs, with_router):
    (h_ref, yf_ref, yc_ref, u_ref, ga_ref, gb_ref, dsk_ref, g1_ref, sc2_ref, sh2_ref, n2g_ref,
     wf_ref, wa_ref, wb_ref, wo_ref) = refs[:15]
    rest = refs[15:]
    if with_router:
        wr_ref, br_ref, h1_ref, f_ref, lg_ref = rest
    else:
        h1_ref, f_ref = rest
    ys = jax.nn.gelu(dsk_ref[...] * u_ref[...].astype(F32) + yc_ref[...].astype(F32)).astype(BF16)
    ya = _dot(yf_ref[...], wf_ref[...])
    yb = _dot(ys, wa_ref[...]) * jax.nn.sigmoid(_dot(ys, wb_ref[...]))
    m = (ga_ref[...].astype(F32) * ya + gb_ref[...].astype(F32) * yb).astype(BF16)
    h1 = h_ref[...] + g1_ref[...] * _dot(m, wo_ref[...])
    h1_ref[...] = h1
    f = _rms_mod(h1, n2g_ref[...], sc2_ref[...], sh2_ref[...])
    f_ref[...] = f.astype(f_ref.dtype)
    if with_router:
        lg_ref[...] = _dot3(f, wr_ref[...]) + br_ref[...]


def _merge_call(h, yf, yc, u, ga, gb, dsk, g1, sc2, sh2, n2g, wf, wa, wb, wo, router, tm):
    b, l, d = h.shape
    w = yf.shape[-1]
    tok = lambda width: pl.BlockSpec((None, tm, width), lambda i, m: (i, m, 0))
    vec = pl.BlockSpec((None, 1, d), lambda i, m: (i, 0, 0))
    full = lambda a: pl.BlockSpec(a.shape, lambda i, m: (0,) * a.ndim)
    dsk = dsk.reshape(1, w)
    n2g = n2g.reshape(1, d)
    args = [h, yf, yc, u, ga, gb, dsk, g1, sc2, sh2, n2g, wf, wa, wb, wo]
    in_specs = [tok(d), tok(w), tok(w), tok(w), tok(d), tok(d), full(dsk), vec, vec, vec, full(n2g),
                full(wf), full(wa), full(wb), full(wo)]
    out_specs = [tok(d), tok(d)]
    f_dtype = F32 if router is not None else BF16
    out_shape = [jax.ShapeDtypeStruct((b, l, d), F32), jax.ShapeDtypeStruct((b, l, d), f_dtype)]
    if router is not None:
        wr, br = router
        args += [wr, br]
        in_specs += [full(wr), full(br)]
        out_specs.append(tok(ROUTER_PAD))
        out_shape.append(jax.ShapeDtypeStruct((b, l, ROUTER_PAD), F32))
    return pl.pallas_call(
        functools.partial(_merge_kernel, with_router=router is not None),
        grid=(b, l // tm),
        in_specs=in_specs, out_specs=out_specs, out_shape=out_shape,
        compiler_params=_cparams(("parallel", "parallel")),
        name="merge",
    )(*args)


R_E1, R_E2, R_RANK1, R_RANK2, R_W1, R_W2 = range(6)


def _route_kernel(lg_ref, rec_ref, cnt_ref, carry_ref):
    @pl.when(pl.program_id(0) == 0)
    def _():
        carry_ref[...] = jnp.zeros_like(carry_ref)

    lg = lg_ref[...]
    tm = lg.shape[0]
    lane = lax.broadcasted_iota(jnp.int32, lg.shape, 1)
    m1 = jnp.max(lg, axis=-1, keepdims=True)
    i1 = jnp.min(jnp.where(lg == m1, lane, ROUTER_PAD), axis=-1, keepdims=True)
    lg2 = jnp.where(lane == i1, -jnp.inf, lg)
    m2 = jnp.max(lg2, axis=-1, keepdims=True)
    i2 = jnp.min(jnp.where(lg2 == m2, lane, ROUTER_PAD), axis=-1, keepdims=True)
    e = jnp.exp(m2 - m1)
    w1 = 1.0 / (1.0 + e)
    w2 = e * w1

    oh1 = lane == i1
    oh2 = lane == i2
    row = lax.broadcasted_iota(jnp.int32, (tm, tm), 0)
    col = lax.broadcasted_iota(jnp.int32, (tm, tm), 1)
    below = (row > col).astype(BF16)
    p1 = _dot(below, oh1.astype(BF16))
    p2 = _dot(below, oh2.astype(BF16))
    c1 = jnp.sum(oh1.astype(F32), axis=0, keepdims=True)
    c2 = jnp.sum(oh2.astype(F32), axis=0, keepdims=True)
    base = carry_ref[...]
    r1 = jnp.sum(jnp.where(oh1, p1 + base, 0.0), axis=-1, keepdims=True)
    r2 = jnp.sum(jnp.where(oh2, p2 + (base + c1), 0.0), axis=-1, keepdims=True)
    total = base + c1 + c2
    carry_ref[...] = total
    cnt_ref[...] = total

    rec = jnp.zeros_like(lg)
    for slot, val in ((R_E1, i1.astype(F32)), (R_E2, i2.astype(F32)), (R_RANK1, r1), (R_RANK2, r2),
                      (R_W1, w1), (R_W2, w2)):
        rec = jnp.where(lane == slot, val, rec)
    rec_ref[...] = rec


def _route_call(logits, tm):
    t, n = logits.shape
    return pl.pallas_call(
        _route_kernel,
        grid=(t // tm,),
        in_specs=[pl.BlockSpec((tm, n), lambda i: (i, 0))],
        out_specs=[pl.BlockSpec((tm, n), lambda i: (i, 0)), pl.BlockSpec((1, n), lambda i: (0, 0))],
        out_shape=[jax.ShapeDtypeStruct((t, n), F32), jax.ShapeDtypeStruct((1, n), F32)],
        scratch_shapes=[pltpu.VMEM((1, n), F32)],
        compiler_params=_cparams(("arbitrary",)),
        name="route",
    )(logits)


def _dispatch_kernel(pos_ref, f_ref, xs_in_ref, xs_ref, sem):
    del xs_in_ref
    tm = f_ref.shape[0]
    base = pl.program_id(0) * (2 * tm)

    def row_copy(r, k):
        return pltpu.make_async_copy(f_ref.at[pl.ds(r, 1)], xs_ref.at[pl.ds(pos_ref[base + 2 * r + k], 1)], sem)

    def issue(r, carry):
        row_copy(r, 0).start()
        row_copy(r, 1).start()
        return carry

    lax.fori_loop(0, tm, issue, 0, unroll=8)
    for _ in range(2):
        pltpu.make_async_copy(f_ref, xs_ref.at[pl.ds(0, tm)], sem).wait()


def _dispatch_call(pos, f, n_rows, tm):
    t, d = f.shape
    return pl.pallas_call(
        _dispatch_kernel,
        grid_spec=pltpu.PrefetchScalarGridSpec(
            num_scalar_prefetch=1,
            grid=(t // tm,),
            in_specs=[pl.BlockSpec((tm, d), lambda i, pos: (i, 0)),
                      pl.BlockSpec(memory_space=pl.ANY)],
            out_specs=pl.BlockSpec(memory_space=pl.ANY),
            scratch_shapes=[pltpu.SemaphoreType.DMA]),
        out_shape=jax.ShapeDtypeStruct((n_rows, d), f.dtype),
        input_output_aliases={2: 0},
        compiler_params=pltpu.CompilerParams(dimension_semantics=("arbitrary",), vmem_limit_bytes=VMEM_LIMIT,
                                             disable_bounds_checks=True),
        name="dispatch",
    )(pos, f, jnp.zeros((n_rows, d), f.dtype))


def _gffn_kernel(te_ref, nu_ref, x_ref, wg_ref, wu_ref, wd_ref, o_ref, xb_ref):
    del te_ref
    j = pl.program_id(1)

    @pl.when(pl.program_id(0) < nu_ref[0])
    def _():
        @pl.when(j == 0)
        def _():
            xb_ref[...] = x_ref[...].astype(BF16)
            o_ref[...] = jnp.zeros_like(o_ref)

        x = xb_ref[...]
        hid = jax.nn.silu(_dot(x, wg_ref[...].astype(BF16))) * _dot(x, wu_ref[...].astype(BF16))
        o_ref[...] += _dot(hid.astype(BF16), wd_ref[...].astype(BF16))


def _gffn_call(tile_expert, n_used, xs, wg, wu, wd, tm, tf):
    n_rows, d = xs.shape
    f = wg.shape[-1]
    nj = f // tf

    def row_map(i, j, te, nu):
        return jnp.minimum(i, nu[0] - 1), 0

    def jj(i, j, nu):
        return jnp.where(i < nu[0], j, nj - 1)

    return pl.pallas_call(
        _gffn_kernel,
        grid_spec=pltpu.PrefetchScalarGridSpec(
            num_scalar_prefetch=2,
            grid=(n_rows // tm, nj),
            in_specs=[pl.BlockSpec((tm, d), row_map),
                      pl.BlockSpec((None, d, tf), lambda i, j, te, nu: (te[i], 0, jj(i, j, nu))),
                      pl.BlockSpec((None, d, tf), lambda i, j, te, nu: (te[i], 0, jj(i, j, nu))),
                      pl.BlockSpec((None, tf, d), lambda i, j, te, nu: (te[i], jj(i, j, nu), 0))],
            out_specs=pl.BlockSpec((tm, d), row_map),
            scratch_shapes=[pltpu.VMEM((tm, d), BF16)]),
        out_shape=jax.ShapeDtypeStruct((n_rows, d), F32),
        input_output_aliases={2: 0},
        compiler_params=_cparams(("arbitrary", "arbitrary")),
        name="gffn",
    )(tile_expert, n_used, xs, wg, wu, wd)


def _combine_kernel(pos_ref, ys_ref, h_ref, rec_ref, g2_ref, fg_ref, o_ref, ybuf_ref, sem, *, final_norm):
    tm = h_ref.shape[0]
    base = pl.program_id(0) * (2 * tm)

    def row_copy(r, k):
        return pltpu.make_async_copy(ys_ref.at[pl.ds(pos_ref[base + 2 * r + k], 1)],
                                     ybuf_ref.at[k, pl.ds(r, 1)], sem)

    def issue(r, carry):
        row_copy(r, 0).start()
        row_copy(r, 1).start()
        return carry

    lax.fori_loop(0, tm, issue, 0, unroll=8)
    for k in range(2):
        pltpu.make_async_copy(ys_ref.at[pl.ds(0, tm)], ybuf_ref.at[k], sem).wait()
    rec = rec_ref[...]
    y = rec[:, R_W1:R_W1 + 1] * ybuf_ref[0] + rec[:, R_W2:R_W2 + 1] * ybuf_ref[1]
    out = h_ref[...] + g2_ref[...] * y
    if final_norm:
        out = out * lax.rsqrt(jnp.mean(out * out, axis=-1, keepdims=True) + EPS) * fg_ref[...]
    o_ref[...] = out


def _combine_call(pos, ys, h, rec, g2, final_g, tm):
    t, d = h.shape
    tiles_per_g2 = t // g2.shape[0] // tm
    fg = (final_g if final_g is not None else jnp.ones((d,), F32)).reshape(1, d)
    return pl.pallas_call(
        functools.partial(_combine_kernel, final_norm=final_g is not None),
        grid_spec=pltpu.PrefetchScalarGridSpec(
            num_scalar_prefetch=1,
            grid=(t // tm,),
            in_specs=[pl.BlockSpec(memory_space=pl.ANY),
                      pl.BlockSpec((tm, d), lambda i, pos: (i, 0)),
                      pl.BlockSpec((tm, ROUTER_PAD), lambda i, pos: (i, 0)),
                      pl.BlockSpec((None, 1, d), lambda i, pos: (i // tiles_per_g2, 0, 0)),
                      pl.BlockSpec((1, d), lambda i, pos: (0, 0))],
            out_specs=pl.BlockSpec((tm, d), lambda i, pos: (i, 0)),
            scratch_shapes=[pltpu.VMEM((2, tm, d), F32), pltpu.SemaphoreType.DMA]),
        out_shape=jax.ShapeDtypeStruct((t, d), F32),
        compiler_params=pltpu.CompilerParams(dimension_semantics=("arbitrary",), vmem_limit_bytes=VMEM_LIMIT,
                                             disable_bounds_checks=True),
        name="combine",
    )(pos, ys, h, rec, g2, fg)


def _moe_plan(rec, counts, tm, n_tiles):
    cnt = counts[0, :N_EXPERTS].astype(jnp.int32)
    nt = (cnt + (tm - 1)) // tm
    cum = jnp.cumsum(nt)
    start = (cum - nt) * tm
    e = rec[:, R_E1:R_E2 + 1].astype(jnp.int32)
    rank = rec[:, R_RANK1:R_RANK2 + 1].astype(jnp.int32)
    ex = lax.broadcasted_iota(jnp.int32, e.shape + (N_EXPERTS,), 2)
    pos = rank + jnp.sum(jnp.where(e[..., None] == ex, start, 0), axis=-1)
    tile = jnp.arange(n_tiles, dtype=jnp.int32)
    te = jnp.sum((tile[:, None] >= cum[None, :]).astype(jnp.int32), axis=1)
    n_used = cum[-1:]
    last_e = jnp.sum((n_used - 1 >= cum).astype(jnp.int32))
    return pos.reshape(-1), jnp.minimum(te, last_e), n_used


def _ffn_kernel(x_ref, wg_ref, wu_ref, wd_ref, h_ref, g2_ref, fg_ref, o_ref, acc_ref, *, final_norm):
    j = pl.program_id(1)

    @pl.when(j == 0)
    def _():
        acc_ref[...] = jnp.zeros_like(acc_ref)

    x = x_ref[...]
    hid = jax.nn.silu(_dot(x, wg_ref[...].astype(BF16))) * _dot(x, wu_ref[...].astype(BF16))
    acc_ref[...] += _dot(hid.astype(BF16), wd_ref[...].astype(BF16))

    @pl.when(j == pl.num_programs(1) - 1)
    def _():
        out = h_ref[...] + g2_ref[...] * acc_ref[...]
        if final_norm:
            out = out * lax.rsqrt(jnp.mean(out * out, axis=-1, keepdims=True) + EPS) * fg_ref[...]
        o_ref[...] = out


def _ffn_call(x, wg, wu, wd, h, g2, final_g, tm, tf):
    t, d = x.shape
    f = wg.shape[-1]
    tiles_per_g2 = t // g2.shape[0] // tm
    tok = lambda width: pl.BlockSpec((tm, width), lambda m, j: (m, 0))
    fg = (final_g if final_g is not None else jnp.ones((d,), F32)).reshape(1, d)
    return pl.pallas_call(
        functools.partial(_ffn_kernel, final_norm=final_g is not None),
        grid=(t // tm, f // tf),
        in_specs=[tok(d),
                  pl.BlockSpec((d, tf), lambda m, j: (0, j)),
                  pl.BlockSpec((d, tf), lambda m, j: (0, j)),
                  pl.BlockSpec((tf, d), lambda m, j: (j, 0)),
                  tok(d),
                  pl.BlockSpec((None, 1, d), lambda m, j: (m // tiles_per_g2, 0, 0)),
                  pl.BlockSpec((1, d), lambda m, j: (0, 0))],
        out_specs=tok(d),
        out_shape=jax.ShapeDtypeStruct((t, d), F32),
        scratch_shapes=[pltpu.VMEM((tm, d), F32)],
        compiler_params=_cparams(("parallel", "arbitrary")),
        name="ffn",
    )(x, wg, wu, wd, h, g2, fg)


def _pos_table(n, dim):
    t = np.arange(n)
    r = (t // GRID_W).astype(np.float32)
    col = (t % GRID_W).astype(np.float32)
    quarter = dim // 4
    omega = (1.0 / (POS_BASE ** (np.arange(quarter, dtype=np.float32) / quarter))).astype(np.float32)
    ar = r[:, None] * omega
    ac = col[:, None] * omega
    return np.concatenate([np.sin(ar), np.cos(ar), np.sin(ac), np.cos(ac)], axis=-1).astype(np.float32)


def _pick_tile(n, pref):
    return pref if n % pref == 0 else n


def kernel(x, c, ctx, c_ctx, w_mod, b_mod, norm1_g, norm2_g, w_in, w_four, ssm_a_re, ssm_a_im, ssm_log_dt, ssm_b_re, ssm_b_im, ssm_c_re, ssm_c_im, ssm_d, w_glu_a, w_glu_b, w_out, ffn_w_gate, ffn_w_up, ffn_w_down, moe_w_router, moe_b_router, moe_w_gate, moe_w_up, moe_w_down, final_g):
    b, l, d = x.shape
    lc = ctx.shape[1]
    depth = w_mod.shape[0]
    wf_cols = w_four.shape[1]
    ws_cols = ssm_d.shape[1]
    off_ga = wf_cols + ws_cols
    tm_lat = _pick_tile(l, 512)
    tm_ctx = _pick_tile(lc, 256)

    n_cond = b + 8
    cond = jnp.concatenate([c, jnp.broadcast_to(c_ctx[None], (n_cond - b, d))], axis=0)
    pos = jnp.asarray(_pos_table(l, d))

    h = x
    z = ctx
    for layer in range(depth):
        need_ctx = layer < depth - 1
        mods = _mods_call(cond, w_mod[layer], b_mod[layer])
        lat_mods = [m.reshape(b, 1, d) for m in jnp.split(mods[:b], 6, axis=-1)]
        ctx_mods = [jnp.broadcast_to(m.reshape(1, 1, d), (b, 1, d)) for m in jnp.split(mods[b:b + 1], 6, axis=-1)]
        sh1, sc1, g1, sh2, sc2, g2 = lat_mods
        csh1, csc1, cg1, csh2, csc2, cg2 = ctx_mods
        w_in_l = w_in[layer].astype(BF16)
        widths = (wf_cols, ws_cols, d, d)
        acts = (False, False, True, True)

        outs = _inproj_call(h, pos if layer == 0 else None, sc1, sh1, norm1_g[layer], w_in_l,
                            widths, acts, tm_lat)
        if layer == 0:
            h, outs = outs[0], outs[1:]
        pf, ps, ga, gb = outs
        if need_ctx:
            cpf, cps, cga, cgb = _inproj_call(z, None, csc1, csh1, norm1_g[layer], w_in_l,
                                              widths, acts, tm_ctx)
        else:
            (cps,) = _inproj_call(z, None, csc1, csh1, norm1_g[layer], w_in_l[:, wf_cols:off_ga],
                                  (ws_cols,), (False,), tm_ctx)

        tables = _s5_tables(ssm_a_re[layer], ssm_a_im[layer], ssm_log_dt[layer], ssm_b_re[layer],
                            ssm_b_im[layer], ssm_c_re[layer], ssm_c_im[layer])
        yc_ctx, yc_lat = _s5_call(cps, ps, tables)
        yf = _fourier_call(pf, tm_lat)

        wf = w_four[layer].astype(BF16)
        wa = w_glu_a[layer].astype(BF16)
        wb = w_glu_b[layer].astype(BF16)
        wo = w_out[layer].astype(BF16)
        moe = layer % 2 == 1
        idx = layer // 2
        router = None
        if moe:
            wr = jnp.zeros((d, ROUTER_PAD), F32).at[:, :N_EXPERTS].set(moe_w_router[idx])
            br = jnp.full((1, ROUTER_PAD), NEG_BIG, F32).at[0, :N_EXPERTS].set(moe_b_router[idx])
            router = (wr, br)
        res = _merge_call(h, yf, yc_lat, ps, ga, gb, ssm_d[layer], g1, sc2, sh2, norm2_g[layer],
                          wf, wa, wb, wo, router, tm_lat)
        last = layer == depth - 1
        fin = final_g if last else None
        t = b * l
        if moe:
            h1, f, logits = res
            tm_moe = min(1024, max(128, t // 8))
            n_tiles = 2 * t // tm_moe + N_EXPERTS
            rec, counts = _route_call(logits.reshape(t, ROUTER_PAD), _pick_tile(t, 512))
            pos, tile_expert, n_used = _moe_plan(rec, counts, tm_moe, n_tiles)
            xs = _dispatch_call(pos, f.reshape(t, d), n_tiles * tm_moe, _pick_tile(t, 512))
            ys = _gffn_call(tile_expert, n_used, xs, moe_w_gate[idx], moe_w_up[idx], moe_w_down[idx],
                            tm_moe, _pick_tile(moe_w_gate.shape[-1], 256))
            h = _combine_call(pos, ys, h1.reshape(t, d), rec, g2, fin, _pick_tile(l, 512)).reshape(b, l, d)
        else:
            h1, f = res
            ffn_w = (ffn_w_gate[idx], ffn_w_up[idx], ffn_w_down[idx])
            tf = _pick_tile(ffn_w[0].shape[-1], 256)
            h = _ffn_call(f.reshape(t, d), *ffn_w, h1.reshape(t, d), g2, fin,
                          _pick_tile(l, 1024), tf).reshape(b, l, d)

        if need_ctx:
            if moe:
                raise NotImplementedError("context tokens through an expert layer")
            cyf = _fourier_call(cpf, tm_ctx)
            z1, cf = _merge_call(z, cyf, yc_ctx, cps, cga, cgb, ssm_d[layer], cg1, csc2, csh2,
                                 norm2_g[layer], wf, wa, wb, wo, None, tm_ctx)
            z = _ffn_call(cf.reshape(b * lc, d), *ffn_w, z1.reshape(b * lc, d), cg2[:1], None,
                          _pick_tile(b * lc, 1024), tf).reshape(b, lc, d)
    return h
```

```python
import functools
import math

import numpy as np
import jax
import jax.numpy as jnp
from jax import lax
from jax.experimental import pallas as pl
from jax.experimental.pallas import tpu as pltpu

F32 = jnp.float32
BF16 = jnp.bfloat16

GRID_W = 64
N_FOURIER_GROUPS = 4
FOURIER_GROUP = 128
SSM_GROUP = 16
SSM_STATE = 64
N_EXPERTS = 8
EPS = 1e-6
POS_BASE = 10000.0

CHUNK = 16
LANES = 128
GPT = LANES // SSM_GROUP
N_COMP = 4
ROUTER_PAD = LANES
NEG_BIG = -1e30
VMEM_LIMIT = 56 * 1024 * 1024


def _cparams(sem):
    return pltpu.CompilerParams(dimension_semantics=sem, vmem_limit_bytes=VMEM_LIMIT)


def _dot(a, b):
    return jnp.dot(a, b, preferred_element_type=F32)


def _split_bf16(a):
    hi = a.astype(BF16)
    lo = (a - hi.astype(F32)).astype(BF16)
    return hi, lo


def _dot3(a, b):
    ah, al = _split_bf16(a)
    bh, bl = _split_bf16(b)
    return _dot(ah, bh) + (_dot(ah, bl) + _dot(al, bh))


def _rms_mod(h, g, sc, sh):
    y = h * lax.rsqrt(jnp.mean(h * h, axis=-1, keepdims=True) + EPS)
    return (y * g) * (1.0 + sc) + sh


def _mods_kernel(c_ref, w_ref, b_ref, o_ref):
    s = jax.nn.silu(c_ref[...])
    o_ref[...] = _dot3(s, w_ref[...]) + b_ref[...]


def _mods_call(cond, w_mod, b_mod):
    r, d = cond.shape
    n = w_mod.shape[1]
    tn = 1024
    return pl.pallas_call(
        _mods_kernel,
        grid=(n // tn,),
        in_specs=[pl.BlockSpec((r, d), lambda j: (0, 0)),
                  pl.BlockSpec((d, tn), lambda j: (0, j)),
                  pl.BlockSpec((1, tn), lambda j: (0, j))],
        out_specs=pl.BlockSpec((r, tn), lambda j: (0, j)),
        out_shape=jax.ShapeDtypeStruct((r, n), F32),
        compiler_params=_cparams(("arbitrary",)),
        name="mods",
    )(cond, w_mod, b_mod.reshape(1, n))


def _inproj_kernel(*refs, acts, has_pos):
    it = iter(refs)
    x_ref = next(it)
    pos_ref = next(it) if has_pos else None
    sc_ref, sh_ref, g_ref, w_ref = next(it), next(it), next(it), next(it)
    h_out = next(it) if has_pos else None
    outs = list(it)
    h = x_ref[...]
    if has_pos:
        h = h + pos_ref[...]
        h_out[...] = h
    a = _rms_mod(h, g_ref[...], sc_ref[...], sh_ref[...]).astype(BF16)
    col = 0
    for o, act in zip(outs, acts):
        width = o.shape[-1]
        p = _dot(a, w_ref[:, col:col + width])
        if act:
            p = jax.nn.sigmoid(p)
        o[...] = p.astype(o.dtype)
        col += width


def _inproj_call(x, pos, sc, sh, g, w, widths, acts, tm):
    b, l, d = x.shape
    has_pos = pos is not None
    tok = pl.BlockSpec((None, tm, d), lambda i, m: (i, m, 0))
    vec = pl.BlockSpec((None, 1, d), lambda i, m: (i, 0, 0))
    in_specs = [tok]
    args = [x]
    if has_pos:
        in_specs.append(pl.BlockSpec((tm, d), lambda i, m: (m, 0)))
        args.append(pos)
    in_specs += [vec, vec, pl.BlockSpec((1, d), lambda i, m: (0, 0)),
                 pl.BlockSpec(w.shape, lambda i, m: (0, 0))]
    args += [sc, sh, g.reshape(1, d), w]
    out_specs, out_shape = [], []
    if has_pos:
        out_specs.append(tok)
        out_shape.append(jax.ShapeDtypeStruct((b, l, d), F32))
    for width in widths:
        out_specs.append(pl.BlockSpec((None, tm, width), lambda i, m: (i, m, 0)))
        out_shape.append(jax.ShapeDtypeStruct((b, l, width), BF16))
    return pl.pallas_call(
        functools.partial(_inproj_kernel, acts=tuple(acts), has_pos=has_pos),
        grid=(b, l // tm),
        in_specs=in_specs, out_specs=out_specs, out_shape=out_shape,
        compiler_params=_cparams(("parallel", "parallel")),
        name="inproj",
    )(*args)


def _dft_tables(l):
    k = np.arange(l, dtype=np.int64)
    ang = 2.0 * np.pi * ((k[:, None] * k[None, :]) % l).astype(np.float64) / l
    dl = np.concatenate([np.cos(ang), -np.sin(ang)], axis=1).astype(np.float32)
    c = np.arange(FOURIER_GROUP, dtype=np.int64)
    angc = 2.0 * np.pi * ((c[:, None] * c[None, :]) % FOURIER_GROUP).astype(np.float64) / FOURIER_GROUP
    return dl, np.cos(angc).astype(np.float32), np.sin(angc).astype(np.float32)


def _fourier_kernel(u_ref, cc_ref, cs_ref, dl_ref, o_ref, v_ref, *, l, scale):
    @pl.when(pl.program_id(1) == 0)
    def _():
        for j in range(N_FOURIER_GROUPS):
            cols = slice(j * FOURIER_GROUP, (j + 1) * FOURIER_GROUP)
            uj = u_ref[:, cols]
            v_ref[0:l, cols] = _dot(uj, cc_ref[...]).astype(BF16)
            v_ref[l:2 * l, cols] = _dot(uj, cs_ref[...]).astype(BF16)

    o_ref[...] = (_dot(dl_ref[...], v_ref[...]) * scale).astype(o_ref.dtype)


def _fourier_call(u, tm):
    b, l, w = u.shape
    dl, cc, cs = _dft_tables(l)
    dl = jnp.asarray(dl).astype(BF16)
    cc = jnp.asarray(cc).astype(BF16)
    cs = jnp.asarray(cs).astype(BF16)
    scale = 1.0 / math.sqrt(l * FOURIER_GROUP)
    return pl.pallas_call(
        functools.partial(_fourier_kernel, l=l, scale=scale),
        grid=(b, l // tm),
        in_specs=[pl.BlockSpec((None, l, w), lambda i, m: (i, 0, 0)),
                  pl.BlockSpec((FOURIER_GROUP, FOURIER_GROUP), lambda i, m: (0, 0)),
                  pl.BlockSpec((FOURIER_GROUP, FOURIER_GROUP), lambda i, m: (0, 0)),
                  pl.BlockSpec((tm, 2 * l), lambda i, m: (m, 0))],
        out_specs=pl.BlockSpec((None, tm, w), lambda i, m: (i, m, 0)),
        out_shape=jax.ShapeDtypeStruct((b, l, w), BF16),
        scratch_shapes=[pltpu.VMEM((2 * l, w), BF16)],
        compiler_params=_cparams(("parallel", "arbitrary")),
        name="fourier",
    )(u, cc, cs, dl)


def _s5_tables(a_re, a_im, log_dt, b_re, b_im, c_re, c_im):
    hp = lax.Precision.HIGHEST
    q = CHUNK
    g = a_re.shape[1]
    dt = jnp.exp(log_dt)[..., None]
    lr, li = a_re * dt, a_im * dt
    em1_r = jnp.expm1(lr) * jnp.cos(li) - 2.0 * jnp.sin(0.5 * li) ** 2
    em1_i = jnp.exp(lr) * jnp.sin(li)
    den = a_re * a_re + a_im * a_im
    fr = (em1_r * a_re + em1_i * a_im) / den
    fi = (em1_i * a_re - em1_r * a_im) / den
    bbr = fr[..., None] * b_re - fi[..., None] * b_im
    bbi = fr[..., None] * b_im + fi[..., None] * b_re
    k = jnp.arange(q + 1, dtype=F32)
    mag = jnp.exp(lr[..., None] * k)
    pr = mag * jnp.cos(li[..., None] * k)
    pi = mag * jnp.sin(li[..., None] * k)

    prk = jnp.moveaxis(pr, -1, 2)[..., None]
    pik = jnp.moveaxis(pi, -1, 2)[..., None]
    wr = prk * bbr[:, :, None] - pik * bbi[:, :, None]
    wi = prk * bbi[:, :, None] + pik * bbr[:, :, None]
    kern = (jnp.einsum('dgop,dgtph->dgtoh', c_re, wr[:, :, :q], precision=hp)
            - jnp.einsum('dgop,dgtph->dgtoh', c_im, wi[:, :, :q], precision=hp))
    qi = jnp.arange(q)
    tau_f = qi[None, :] - qi[:, None]
    m_f = jnp.where((tau_f >= 0)[None, :, :, None, None], kern[0][:, jnp.clip(tau_f, 0, q - 1)], 0.0)
    m_b = jnp.where((tau_f <= 0)[None, :, :, None, None], kern[1][:, jnp.clip(-tau_f, 0, q - 1)], 0.0)
    mt = (m_f + m_b).transpose(0, 1, 4, 2, 3).reshape(g, q * SSM_GROUP, q * SSM_GROUP)

    def inject(w):
        wf = w[0][:, q - 1 - qi]
        wb = w[1][:, qi]
        both = jnp.stack([wf, wb])
        return both.transpose(0, 1, 2, 4, 3).reshape(2, g, q * SSM_GROUP, SSM_STATE)

    wre, wim = inject(wr), inject(wi)

    def carry_out(sign):
        outs = []
        for d, idx in ((0, qi + 1), (1, q - qi)):
            ppr = pr[d][:, :, idx]
            ppi = pi[d][:, :, idx]
            cr = c_re[d].transpose(0, 2, 1)[:, :, None, :]
            ci = c_im[d].transpose(0, 2, 1)[:, :, None, :]
            if sign > 0:
                val = cr * ppr[..., None] - ci * ppi[..., None]
            else:
                val = -(cr * ppi[..., None] + ci * ppr[..., None])
            outs.append(val.reshape(g, SSM_STATE, q * SSM_GROUP))
        return jnp.stack(outs)

    cre, cim = carry_out(+1), carry_out(-1)

    nt = g // GPT
    eye = jnp.eye(GPT, dtype=BF16)
    spread = lambda x: x[..., :, None, :] * eye[:, None, None, :, None]
    m6 = mt.astype(BF16).reshape(nt, GPT, q, SSM_GROUP, q, SSM_GROUP).transpose(0, 2, 1, 3, 4, 5)
    big_m = spread(m6).reshape(nt, q * LANES, q * LANES)
    wc = jnp.stack([wre[0], wim[0], wre[1], wim[1]]).astype(BF16)
    wc = wc.reshape(N_COMP, nt, GPT, q, SSM_GROUP, SSM_STATE).transpose(1, 3, 2, 4, 0, 5)
    big_w = spread(wc).reshape(nt, q * LANES, N_COMP * GPT * SSM_STATE)
    cc = jnp.stack([cre[0], cim[0], cre[1], cim[1]]).astype(BF16)
    cc = cc.reshape(N_COMP, nt, GPT, SSM_STATE, q, SSM_GROUP).transpose(1, 0, 2, 3, 4, 5)
    big_c = spread(cc).reshape(nt, N_COMP * GPT * SSM_STATE, q * LANES)
    decay = jnp.stack([pr[0, ..., q], pi[0, ..., q], pr[1, ..., q], pi[1, ..., q]])
    decay = decay.reshape(N_COMP, nt, 1, GPT * SSM_STATE).transpose(1, 0, 2, 3)
    return big_m, big_w, big_c, decay


def _chunk_rows(u_ref):
    nc, _, nb, lanes = u_ref.shape
    return jnp.concatenate([u_ref[:, qq].reshape(nc * nb, lanes) for qq in range(CHUNK)], axis=-1)


def _s5_inject_kernel(u_ref, w_ref, s_ref):
    s = _dot(_chunk_rows(u_ref), w_ref[...])
    width = s_ref.shape[-1]
    for comp in range(N_COMP):
        s_ref[comp] = s[:, comp * width:(comp + 1) * width]


def _s5_scan_kernel(s_ref, a_ref, o_ref, *, batch, n_lat_chunks, n_ctx_chunks):
    backward = pl.program_id(1) == 1
    ar, ai = a_ref[0], a_ref[1]

    def step(i, carry):
        sr, si = carry
        k = i - n_ctx_chunks
        fwd = jnp.where(i < n_ctx_chunks, n_lat_chunks + i, k)
        bwd = jnp.where(i < n_ctx_chunks, n_lat_chunks + n_ctx_chunks - 1 - i, n_lat_chunks - 1 - k)
        rows = pl.ds(pl.multiple_of(jnp.where(backward, bwd, fwd) * batch, batch), batch)
        o_ref[0, rows, :] = sr.astype(o_ref.dtype)
        o_ref[1, rows, :] = si.astype(o_ref.dtype)
        return (ar * sr - ai * si + s_ref[0, rows, :], ar * si + ai * sr + s_ref[1, rows, :])

    z = jnp.zeros((batch, s_ref.shape[-1]), F32)
    lax.fori_loop(0, n_lat_chunks + n_ctx_chunks, step, (z, z))


def _s5_out_kernel(u_ref, sp_ref, m_ref, c_ref, d_ref, o_ref):
    nc, _, nb, lanes = u_ref.shape
    width = sp_ref.shape[-1]
    y = _dot(_chunk_rows(u_ref), m_ref[...])
    for comp in range(N_COMP):
        y = y + _dot(sp_ref[comp], c_ref[comp * width:(comp + 1) * width, :])
    for qq in range(CHUNK):
        yq = y[:, qq * lanes:(qq + 1) * lanes].reshape(nc, nb, lanes)
        o_ref[:, qq] = jax.nn.gelu(d_ref[...] * u_ref[:, qq].astype(F32) + yq).astype(o_ref.dtype)


def _s5_call(u_ctx, u_lat, d_skip, tables):
    big_m, big_w, big_c, decay = tables
    b, lc, w = u_ctx.shape
    l = u_lat.shape[1]
    nt = w // LANES
    n_lat, n_ctx = l // CHUNK, lc // CHUNK
    n_chunks = n_lat + n_ctx
    rows = n_chunks * b
    sw = GPT * SSM_STATE
    u_t = jnp.concatenate([u_lat, u_ctx], axis=1).transpose(1, 0, 2).reshape(n_chunks, CHUNK, b, w)
    cb = _pick_tile(n_chunks, 18)
    rb = cb * b
    u_spec = pl.BlockSpec((cb, CHUNK, b, LANES), lambda j, r: (r, 0, 0, j))
    op_spec = lambda shape: pl.BlockSpec((None,) + shape, lambda j, r: (j, 0, 0))
    s_loc = pl.pallas_call(
        _s5_inject_kernel,
        grid=(nt, n_chunks // cb),
        in_specs=[u_spec, op_spec(big_w.shape[1:])],
        out_specs=pl.BlockSpec((None, N_COMP, rb, sw), lambda j, r: (j, 0, r, 0)),
        out_shape=jax.ShapeDtypeStruct((nt, N_COMP, rows, sw), F32),
        compiler_params=_cparams(("parallel", "parallel")),
        name="s5_inject",
    )(u_t, big_w)
    s_prev = pl.pallas_call(
        functools.partial(_s5_scan_kernel, batch=b, n_lat_chunks=n_lat, n_ctx_chunks=n_ctx),
        grid=(nt, 2),
        in_specs=[pl.BlockSpec((None, 2, rows, sw), lambda j, d: (j, d, 0, 0)),
                  pl.BlockSpec((None, 2, 1, sw), lambda j, d: (j, d, 0, 0))],
        out_specs=pl.BlockSpec((None, 2, rows, sw), lambda j, d: (j, d, 0, 0)),
        out_shape=jax.ShapeDtypeStruct((nt, N_COMP, rows, sw), BF16),
        compiler_params=_cparams(("parallel", "parallel")),
        name="s5_scan",
    )(s_loc, decay)
    y = pl.pallas_call(
        _s5_out_kernel,
        grid=(nt, n_chunks // cb),
        in_specs=[u_spec,
                  pl.BlockSpec((None, N_COMP, rb, sw), lambda j, r: (j, 0, r, 0)),
                  op_spec(big_m.shape[1:]), op_spec(big_c.shape[1:]),
                  pl.BlockSpec((1, LANES), lambda j, r: (0, j))],
        out_specs=u_spec,
        out_shape=jax.ShapeDtypeStruct((n_chunks, CHUNK, b, w), BF16),
        compiler_params=_cparams(("parallel", "parallel")),
        name="s5_out",
    )(u_t, s_prev, big_m, big_c, d_skip.reshape(1, w))
    return y.reshape(l + lc, b, w).transpose(1, 0, 2)


def _merge_kernel(*refs, with_router):
    (h_ref, yf_ref, ys_ref, ga_ref, gb_ref, g1_ref, sc2_ref, sh2_ref, n2g_ref,
     wf_ref, wa_ref, wb_ref, wo_ref) = refs[:13]
    rest = refs[13:]
    if with_router:
        wr_ref, br_ref, h1_ref, f_ref, lg_ref = rest
    else:
        h1_ref, f_ref = rest
    ys = ys_ref[...]
    ya = _dot(yf_ref[...], wf_ref[...])
    yb = _dot(ys, wa_ref[...]) * jax.nn.sigmoid(_dot(ys, wb_ref[...]))
    m = (ga_ref[...].astype(F32) * ya + gb_ref[...].astype(F32) * yb).astype(BF16)
    h1 = h_ref[...] + g1_ref[...] * _dot(m, wo_ref[...])
    h1_ref[...] = h1
    f = _rms_mod(h1, n2g_ref[...], sc2_ref[...], sh2_ref[...])
    f_ref[...] = f.astype(f_ref.dtype)
    if with_router:
        lg_ref[...] = _dot3(f, wr_ref[...]) + br_ref[...]


def _merge_call(h, yf, ys_all, ys_row0, ga, gb, g1, sc2, sh2, n2g, wf, wa, wb, wo, router, tm):
    b, l, d = h.shape
    w = yf.shape[-1]
    tok = lambda width: pl.BlockSpec((None, tm, width), lambda i, m: (i, m, 0))
    vec = pl.BlockSpec((None, 1, d), lambda i, m: (i, 0, 0))
    full = lambda a: pl.BlockSpec(a.shape, lambda i, m: (0,) * a.ndim)
    n2g = n2g.reshape(1, d)
    blk0 = ys_row0 // tm
    assert blk0 * tm == ys_row0
    args = [h, yf, ys_all, ga, gb, g1, sc2, sh2, n2g, wf, wa, wb, wo]
    in_specs = [tok(d), tok(w), pl.BlockSpec((None, tm, w), lambda i, m: (i, m + blk0, 0)), tok(d), tok(d),
                vec, vec, vec, full(n2g), full(wf), full(wa), full(wb), full(wo)]
    out_specs = [tok(d), tok(d)]
    f_dtype = F32 if router is not None else BF16
    out_shape = [jax.ShapeDtypeStruct((b, l, d), F32), jax.ShapeDtypeStruct((b, l, d), f_dtype)]
    if router is not None:
        wr, br = router
        args += [wr, br]
        in_specs += [full(wr), full(br)]
        out_specs.append(tok(ROUTER_PAD))
        out_shape.append(jax.ShapeDtypeStruct((b, l, ROUTER_PAD), F32))
    return pl.pallas_call(
        functools.partial(_merge_kernel, with_router=router is not None),
        grid=(b, l // tm),
        in_specs=in_specs, out_specs=out_specs, out_shape=out_shape,
        compiler_params=_cparams(("parallel", "parallel")),
        name="merge",
    )(*args)


R_E1, R_E2, R_RANK1, R_RANK2, R_W1, R_W2 = range(6)


def _route_kernel(lg_ref, rec_ref, cnt_ref, carry_ref):
    @pl.when(pl.program_id(0) == 0)
    def _():
        carry_ref[...] = jnp.zeros_like(carry_ref)

    lg = lg_ref[...]
    tm = lg.shape[0]
    lane = lax.broadcasted_iota(jnp.int32, lg.shape, 1)
    m1 = jnp.max(lg, axis=-1, keepdims=True)
    i1 = jnp.min(jnp.where(lg == m1, lane, ROUTER_PAD), axis=-1, keepdims=True)
    lg2 = jnp.where(lane == i1, -jnp.inf, lg)
    m2 = jnp.max(lg2, axis=-1, keepdims=True)
    i2 = jnp.min(jnp.where(lg2 == m2, lane, ROUTER_PAD), axis=-1, keepdims=True)
    e = jnp.exp(m2 - m1)
    w1 = 1.0 / (1.0 + e)
    w2 = e * w1

    oh1 = lane == i1
    oh2 = lane == i2
    row = lax.broadcasted_iota(jnp.int32, (tm, tm), 0)
    col = lax.broadcasted_iota(jnp.int32, (tm, tm), 1)
    below = (row > col).astype(BF16)
    p1 = _dot(below, oh1.astype(BF16))
    p2 = _dot(below, oh2.astype(BF16))
    c1 = jnp.sum(oh1.astype(F32), axis=0, keepdims=True)
    c2 = jnp.sum(oh2.astype(F32), axis=0, keepdims=True)
    base = carry_ref[...]
    r1 = jnp.sum(jnp.where(oh1, p1 + base, 0.0), axis=-1, keepdims=True)
    r2 = jnp.sum(jnp.where(oh2, p2 + (base + c1), 0.0), axis=-1, keepdims=True)
    total = base + c1 + c2
    carry_ref[...] = total
    cnt_ref[...] = total

    rec = jnp.zeros_like(lg)
    for slot, val in ((R_E1, i1.astype(F32)), (R_E2, i2.astype(F32)), (R_RANK1, r1), (R_RANK2, r2),
                      (R_W1, w1), (R_W2, w2)):
        rec = jnp.where(lane == slot, val, rec)
    rec_ref[...] = rec


def _route_call(logits, tm):
    t, n = logits.shape
    return pl.pallas_call(
        _route_kernel,
        grid=(t // tm,),
        in_specs=[pl.BlockSpec((tm, n), lambda i: (i, 0))],
        out_specs=[pl.BlockSpec((tm, n), lambda i: (i, 0)), pl.BlockSpec((1, n), lambda i: (0, 0))],
        out_shape=[jax.ShapeDtypeStruct((t, n), F32), jax.ShapeDtypeStruct((1, n), F32)],
        scratch_shapes=[pltpu.VMEM((1, n), F32)],
        compiler_params=_cparams(("arbitrary",)),
        name="route",
    )(logits)


def _dispatch_kernel(pos_ref, f_ref, xs_in_ref, xs_ref, sem):
    del xs_in_ref
    tm = f_ref.shape[0]
    base = pl.program_id(0) * (2 * tm)

    def row_copy(r, k):
        return pltpu.make_async_copy(f_ref.at[pl.ds(r, 1)], xs_ref.at[pl.ds(pos_ref[base + 2 * r + k], 1)], sem)

    def issue(r, carry):
        row_copy(r, 0).start()
        row_copy(r, 1).start()
        return carry

    lax.fori_loop(0, tm, issue, 0, unroll=8)
    for _ in range(2):
        pltpu.make_async_copy(f_ref, xs_ref.at[pl.ds(0, tm)], sem).wait()


def _dispatch_call(pos, f, n_rows, tm):
    t, d = f.shape
    return pl.pallas_call(
        _dispatch_kernel,
        grid_spec=pltpu.PrefetchScalarGridSpec(
            num_scalar_prefetch=1,
            grid=(t // tm,),
            in_specs=[pl.BlockSpec((tm, d), lambda i, pos: (i, 0)),
                      pl.BlockSpec(memory_space=pl.ANY)],
            out_specs=pl.BlockSpec(memory_space=pl.ANY),
            scratch_shapes=[pltpu.SemaphoreType.DMA]),
        out_shape=jax.ShapeDtypeStruct((n_rows, d), f.dtype),
        input_output_aliases={2: 0},
        compiler_params=pltpu.CompilerParams(dimension_semantics=("arbitrary",), vmem_limit_bytes=VMEM_LIMIT,
                                             disable_bounds_checks=True),
        name="dispatch",
    )(pos, f, jnp.zeros((n_rows, d), f.dtype))


def _gffn_kernel(te_ref, nu_ref, x_ref, wg_ref, wu_ref, wd_ref, o_ref, xb_ref):
    del te_ref
    j = pl.program_id(1)

    @pl.when(pl.program_id(0) < nu_ref[0])
    def _():
        @pl.when(j == 0)
        def _():
            xb_ref[...] = x_ref[...].astype(BF16)
            o_ref[...] = jnp.zeros_like(o_ref)

        x = xb_ref[...]
        hid = jax.nn.silu(_dot(x, wg_ref[...].astype(BF16))) * _dot(x, wu_ref[...].astype(BF16))
        o_ref[...] += _dot(hid.astype(BF16), wd_ref[...].astype(BF16))


def _gffn_call(tile_expert, n_used, xs, wg, wu, wd, tm, tf):
    n_rows, d = xs.shape
    f = wg.shape[-1]
    nj = f // tf

    def row_map(i, j, te, nu):
        return jnp.minimum(i, nu[0] - 1), 0

    def jj(i, j, nu):
        return jnp.where(i < nu[0], j, nj - 1)

    return pl.pallas_call(
        _gffn_kernel,
        grid_spec=pltpu.PrefetchScalarGridSpec(
            num_scalar_prefetch=2,
            grid=(n_rows // tm, nj),
            in_specs=[pl.BlockSpec((tm, d), row_map),
                      pl.BlockSpec((None, d, tf), lambda i, j, te, nu: (te[i], 0, jj(i, j, nu))),
                      pl.BlockSpec((None, d, tf), lambda i, j, te, nu: (te[i], 0, jj(i, j, nu))),
                      pl.BlockSpec((None, tf, d), lambda i, j, te, nu: (te[i], jj(i, j, nu), 0))],
            out_specs=pl.BlockSpec((tm, d), row_map),
            scratch_shapes=[pltpu.VMEM((tm, d), BF16)]),
        out_shape=jax.ShapeDtypeStruct((n_rows, d), F32),
        input_output_aliases={2: 0},
        compiler_params=_cparams(("arbitrary", "arbitrary")),
        name="gffn",
    )(tile_expert, n_used, xs, wg, wu, wd)


def _combine_kernel(pos_ref, ys_ref, h_ref, rec_ref, g2_ref, fg_ref, o_ref, ybuf_ref, sem, *, final_norm):
    tm = h_ref.shape[0]
    base = pl.program_id(0) * (2 * tm)

    def row_copy(r, k):
        return pltpu.make_async_copy(ys_ref.at[pl.ds(pos_ref[base + 2 * r + k], 1)],
                                     ybuf_ref.at[k, pl.ds(r, 1)], sem)

    def issue(r, carry):
        row_copy(r, 0).start()
        row_copy(r, 1).start()
        return carry

    lax.fori_loop(0, tm, issue, 0, unroll=8)
    for k in range(2):
        pltpu.make_async_copy(ys_ref.at[pl.ds(0, tm)], ybuf_ref.at[k], sem).wait()
    rec = rec_ref[...]
    y = rec[:, R_W1:R_W1 + 1] * ybuf_ref[0] + rec[:, R_W2:R_W2 + 1] * ybuf_ref[1]
    out = h_ref[...] + g2_ref[...] * y
    if final_norm:
        out = out * lax.rsqrt(jnp.mean(out * out, axis=-1, keepdims=True) + EPS) * fg_ref[...]
    o_ref[...] = out


def _combine_call(pos, ys, h, rec, g2, final_g, tm):
    t, d = h.shape
    tiles_per_g2 = t // g2.shape[0] // tm
    fg = (final_g if final_g is not None else jnp.ones((d,), F32)).reshape(1, d)
    return pl.pallas_call(
        functools.partial(_combine_kernel, final_norm=final_g is not None),
        grid_spec=pltpu.PrefetchScalarGridSpec(
            num_scalar_prefetch=1,
            grid=(t // tm,),
            in_specs=[pl.BlockSpec(memory_space=pl.ANY),
                      pl.BlockSpec((tm, d), lambda i, pos: (i, 0)),
                      pl.BlockSpec((tm, ROUTER_PAD), lambda i, pos: (i, 0)),
                      pl.BlockSpec((None, 1, d), lambda i, pos: (i // tiles_per_g2, 0, 0)),
                      pl.BlockSpec((1, d), lambda i, pos: (0, 0))],
            out_specs=pl.BlockSpec((tm, d), lambda i, pos: (i, 0)),
            scratch_shapes=[pltpu.VMEM((2, tm, d), F32), pltpu.SemaphoreType.DMA]),
        out_shape=jax.ShapeDtypeStruct((t, d), F32),
        compiler_params=pltpu.CompilerParams(dimension_semantics=("arbitrary",), vmem_limit_bytes=VMEM_LIMIT,
                                             disable_bounds_checks=True),
        name="combine",
    )(pos, ys, h, rec, g2, fg)


def _moe_plan(rec, counts, tm, n_tiles):
    cnt = counts[0, :N_EXPERTS].astype(jnp.int32)
    nt = (cnt + (tm - 1)) // tm
    cum = jnp.cumsum(nt)
    start = (cum - nt) * tm
    e = rec[:, R_E1:R_E2 + 1].astype(jnp.int32)
    rank = rec[:, R_RANK1:R_RANK2 + 1].astype(jnp.int32)
    ex = lax.broadcasted_iota(jnp.int32, e.shape + (N_EXPERTS,), 2)
    pos = rank + jnp.sum(jnp.where(e[..., None] == ex, start, 0), axis=-1)
    tile = jnp.arange(n_tiles, dtype=jnp.int32)
    te = jnp.sum((tile[:, None] >= cum[None, :]).astype(jnp.int32), axis=1)
    n_used = cum[-1:]
    last_e = jnp.sum((n_used - 1 >= cum).astype(jnp.int32))
    return pos.reshape(-1), jnp.minimum(te, last_e), n_used


def _ffn_kernel(x_ref, wg_ref, wu_ref, wd_ref, h_ref, g2_ref, fg_ref, o_ref, acc_ref, *, final_norm):
    j = pl.program_id(1)

    @pl.when(j == 0)
    def _():
        acc_ref[...] = jnp.zeros_like(acc_ref)

    x = x_ref[...]
    hid = jax.nn.silu(_dot(x, wg_ref[...].astype(BF16))) * _dot(x, wu_ref[...].astype(BF16))
    acc_ref[...] += _dot(hid.astype(BF16), wd_ref[...].astype(BF16))

    @pl.when(j == pl.num_programs(1) - 1)
    def _():
        out = h_ref[...] + g2_ref[...] * acc_ref[...]
        if final_norm:
            out = out * lax.rsqrt(jnp.mean(out * out, axis=-1, keepdims=True) + EPS) * fg_ref[...]
        o_ref[...] = out


def _ffn_call(x, wg, wu, wd, h, g2, final_g, tm, tf):
    t, d = x.shape
    f = wg.shape[-1]
    tiles_per_g2 = t // g2.shape[0] // tm
    tok = lambda width: pl.BlockSpec((tm, width), lambda m, j: (m, 0))
    fg = (final_g if final_g is not None else jnp.ones((d,), F32)).reshape(1, d)
    return pl.pallas_call(
        functools.partial(_ffn_kernel, final_norm=final_g is not None),
        grid=(t // tm, f // tf),
        in_specs=[tok(d),
                  pl.BlockSpec((d, tf), lambda m, j: (0, j)),
                  pl.BlockSpec((d, tf), lambda m, j: (0, j)),
                  pl.BlockSpec((tf, d), lambda m, j: (j, 0)),
                  tok(d),
                  pl.BlockSpec((None, 1, d), lambda m, j: (m // tiles_per_g2, 0, 0)),
                  pl.BlockSpec((1, d), lambda m, j: (0, 0))],
        out_specs=tok(d),
        out_shape=jax.ShapeDtypeStruct((t, d), F32),
        scratch_shapes=[pltpu.VMEM((tm, d), F32)],
        compiler_params=_cparams(("parallel", "arbitrary")),
        name="ffn",
    )(x, wg, wu, wd, h, g2, fg)


def _pos_table(n, dim):
    t = np.arange(n)
    r = (t // GRID_W).astype(np.float32)
    col = (t % GRID_W).astype(np.float32)
    quarter = dim // 4
    omega = (1.0 / (POS_BASE ** (np.arange(quarter, dtype=np.float32) / quarter))).astype(np.float32)
    ar = r[:, None] * omega
    ac = col[:, None] * omega
    return np.concatenate([np.sin(ar), np.cos(ar), np.sin(ac), np.cos(ac)], axis=-1).astype(np.float32)


def _pick_tile(n, pref):
    return pref if n % pref == 0 else n


def kernel(x, c, ctx, c_ctx, w_mod, b_mod, norm1_g, norm2_g, w_in, w_four, ssm_a_re, ssm_a_im, ssm_log_dt, ssm_b_re, ssm_b_im, ssm_c_re, ssm_c_im, ssm_d, w_glu_a, w_glu_b, w_out, ffn_w_gate, ffn_w_up, ffn_w_down, moe_w_router, moe_b_router, moe_w_gate, moe_w_up, moe_w_down, final_g):
    b, l, d = x.shape
    lc = ctx.shape[1]
    depth = w_mod.shape[0]
    wf_cols = w_four.shape[1]
    ws_cols = ssm_d.shape[1]
    off_ga = wf_cols + ws_cols
    tm_lat = _pick_tile(l, 512)
    tm_ctx = _pick_tile(lc, 256)

    n_cond = b + 8
    cond = jnp.concatenate([c, jnp.broadcast_to(c_ctx[None], (n_cond - b, d))], axis=0)
    pos = jnp.asarray(_pos_table(l, d))

    h = x
    z = ctx
    for layer in range(depth):
        need_ctx = layer < depth - 1
        mods = _mods_call(cond, w_mod[layer], b_mod[layer])
        lat_mods = [m.reshape(b, 1, d) for m in jnp.split(mods[:b], 6, axis=-1)]
        ctx_mods = [jnp.broadcast_to(m.reshape(1, 1, d), (b, 1, d)) for m in jnp.split(mods[b:b + 1], 6, axis=-1)]
        sh1, sc1, g1, sh2, sc2, g2 = lat_mods
        csh1, csc1, cg1, csh2, csc2, cg2 = ctx_mods
        w_in_l = w_in[layer].astype(BF16)
        widths = (wf_cols, ws_cols, d, d)
        acts = (False, False, True, True)

        outs = _inproj_call(h, pos if layer == 0 else None, sc1, sh1, norm1_g[layer], w_in_l,
                            widths, acts, tm_lat)
        if layer == 0:
            h, outs = outs[0], outs[1:]
        pf, ps, ga, gb = outs
        if need_ctx:
            cpf, cps, cga, cgb = _inproj_call(z, None, csc1, csh1, norm1_g[layer], w_in_l,
                                              widths, acts, tm_ctx)
        else:
            (cps,) = _inproj_call(z, None, csc1, csh1, norm1_g[layer], w_in_l[:, wf_cols:off_ga],
                                  (ws_cols,), (False,), tm_ctx)

        tables = _s5_tables(ssm_a_re[layer], ssm_a_im[layer], ssm_log_dt[layer], ssm_b_re[layer],
                            ssm_b_im[layer], ssm_c_re[layer], ssm_c_im[layer])
        ys_all = _s5_call(cps, ps, ssm_d[layer], tables)
        yf = _fourier_call(pf, tm_lat)

        wf = w_four[layer].astype(BF16)
        wa = w_glu_a[layer].astype(BF16)
        wb = w_glu_b[layer].astype(BF16)
        wo = w_out[layer].astype(BF16)
        moe = layer % 2 == 1
        idx = layer // 2
        router = None
        if moe:
            wr = jnp.zeros((d, ROUTER_PAD), F32).at[:, :N_EXPERTS].set(moe_w_router[idx])
            br = jnp.full((1, ROUTER_PAD), NEG_BIG, F32).at[0, :N_EXPERTS].set(moe_b_router[idx])
            router = (wr, br)
        res = _merge_call(h, yf, ys_all, 0, ga, gb, g1, sc2, sh2, norm2_g[layer],
                          wf, wa, wb, wo, router, tm_lat)
        last = layer == depth - 1
        fin = final_g if last else None
        t = b * l
        if moe:
            h1, f, logits = res
            tm_moe = min(1024, max(128, t // 8))
            n_tiles = 2 * t // tm_moe + N_EXPERTS
            rec, counts = _route_call(logits.reshape(t, ROUTER_PAD), _pick_tile(t, 512))
            pos, tile_expert, n_used = _moe_plan(rec, counts, tm_moe, n_tiles)
            xs = _dispatch_call(pos, f.reshape(t, d), n_tiles * tm_moe, _pick_tile(t, 512))
            ys = _gffn_call(tile_expert, n_used, xs, moe_w_gate[idx], moe_w_up[idx], moe_w_down[idx],
                            tm_moe, _pick_tile(moe_w_gate.shape[-1], 256))
            h = _combine_call(pos, ys, h1.reshape(t, d), rec, g2, fin, _pick_tile(l, 512)).reshape(b, l, d)
        else:
            h1, f = res
            ffn_w = (ffn_w_gate[idx], ffn_w_up[idx], ffn_w_down[idx])
            tf = _pick_tile(ffn_w[0].shape[-1], 256)
            h = _ffn_call(f.reshape(t, d), *ffn_w, h1.reshape(t, d), g2, fin,
                          _pick_tile(l, 1024), tf).reshape(b, l, d)

        if need_ctx:
            if moe:
                raise NotImplementedError("context tokens through an expert layer")
            cyf = _fourier_call(cpf, tm_ctx)
            z1, cf = _merge_call(z, cyf, ys_all, l, cga, cgb, cg1, csc2, csh2,
                                 norm2_g[layer], wf, wa, wb, wo, None, tm_ctx)
            z = _ffn_call(cf.reshape(b * lc, d), *ffn_w, z1.reshape(b * lc, d), cg2[:1], None,
                          _pick_tile(b * lc, 1024), tf).reshape(b, lc, d)
    return h
```

```python
import functools
import math

import numpy as np
import jax
import jax.numpy as jnp
from jax import lax
from jax.experimental import pallas as pl
from jax.experimental.pallas import tpu as pltpu

F32 = jnp.float32
BF16 = jnp.bfloat16

GRID_W = 64
N_FOURIER_GROUPS = 4
FOURIER_GROUP = 128
SSM_GROUP = 16
SSM_STATE = 64
N_EXPERTS = 8
EPS = 1e-6
POS_BASE = 10000.0

CHUNK = 16
LANES = 128
GPT = LANES // SSM_GROUP
N_COMP = 4
ROUTER_PAD = LANES
NEG_BIG = -1e30
VMEM_LIMIT = 56 * 1024 * 1024


def _cparams(sem):
    return pltpu.CompilerParams(dimension_semantics=sem, vmem_limit_bytes=VMEM_LIMIT)


def _dot(a, b):
    return jnp.dot(a, b, preferred_element_type=F32)


def _split_bf16(a):
    hi = a.astype(BF16)
    lo = (a - hi.astype(F32)).astype(BF16)
    return hi, lo


def _dot3(a, b):
    ah, al = _split_bf16(a)
    bh, bl = _split_bf16(b)
    return _dot(ah, bh) + (_dot(ah, bl) + _dot(al, bh))


def _rms_mod(h, g, sc, sh):
    y = h * lax.rsqrt(jnp.mean(h * h, axis=-1, keepdims=True) + EPS)
    return (y * g) * (1.0 + sc) + sh


def _mods_kernel(c_ref, w_ref, b_ref, o_ref):
    s = jax.nn.silu(c_ref[...])
    o_ref[...] = _dot3(s, w_ref[...]) + b_ref[...]


N_MODS = 6
MOD_SHIFT1, MOD_SCALE1, MOD_GATE1, MOD_SHIFT2, MOD_SCALE2, MOD_GATE2 = range(N_MODS)


def _mods_call(cond, w_mod, b_mod):
    r, d = cond.shape
    depth, _, n = w_mod.shape
    tn = 1024
    out = pl.pallas_call(
        _mods_kernel,
        grid=(depth, n // tn),
        in_specs=[pl.BlockSpec((r, d), lambda i, j: (0, 0)),
                  pl.BlockSpec((None, d, tn), lambda i, j: (i, 0, j)),
                  pl.BlockSpec((None, 1, tn), lambda i, j: (i, 0, j))],
        out_specs=pl.BlockSpec((None, r, tn), lambda i, j: (i, 0, j)),
        out_shape=jax.ShapeDtypeStruct((depth, r, n), F32),
        compiler_params=_cparams(("arbitrary", "arbitrary")),
        name="mods",
    )(cond, w_mod, b_mod.reshape(depth, 1, n))
    return out.reshape(depth * r * N_MODS, 1, d)


def _inproj_kernel(*refs, acts, has_pos):
    it = iter(refs)
    x_ref = next(it)
    pos_ref = next(it) if has_pos else None
    sc_ref, sh_ref, g_ref, w_ref = next(it), next(it), next(it), next(it)
    h_out = next(it) if has_pos else None
    outs = list(it)
    h = x_ref[...]
    if has_pos:
        h = h + pos_ref[...]
        h_out[...] = h
    a = _rms_mod(h, g_ref[...], sc_ref[...], sh_ref[...]).astype(BF16)
    col = 0
    for o, act in zip(outs, acts):
        width = o.shape[-1]
        p = _dot(a, w_ref[:, col:col + width])
        if act:
            p = jax.nn.sigmoid(p)
        o[...] = p.astype(o.dtype)
        col += width


def _inproj_call(x, pos, mods, sc_row, sh_row, g, w, layer, col0, widths, acts, tm):
    b, l, d = x.shape
    has_pos = pos is not None
    n_cols = sum(widths)
    col_blk = col0 // n_cols
    assert col_blk * n_cols == col0
    tok = pl.BlockSpec((None, tm, d), lambda i, m: (i, m, 0))
    vec = lambda row: pl.BlockSpec((None, 1, d), lambda i, m: (row(i), 0, 0))
    in_specs = [tok]
    args = [x]
    if has_pos:
        in_specs.append(pl.BlockSpec((tm, d), lambda i, m: (m, 0)))
        args.append(pos)
    in_specs += [vec(sc_row), vec(sh_row), pl.BlockSpec((None, 1, d), lambda i, m: (layer, 0, 0)),
                 pl.BlockSpec((None, d, n_cols), lambda i, m: (layer, 0, col_blk))]
    args += [mods, mods, g, w]
    out_specs, out_shape = [], []
    if has_pos:
        out_specs.append(tok)
        out_shape.append(jax.ShapeDtypeStruct((b, l, d), F32))
    for width in widths:
        out_specs.append(pl.BlockSpec((None, tm, width), lambda i, m: (i, m, 0)))
        out_shape.append(jax.ShapeDtypeStruct((b, l, width), BF16))
    return pl.pallas_call(
        functools.partial(_inproj_kernel, acts=tuple(acts), has_pos=has_pos),
        grid=(b, l // tm),
        in_specs=in_specs, out_specs=out_specs, out_shape=out_shape,
        compiler_params=_cparams(("parallel", "parallel")),
        name="inproj",
    )(*args)


def _dft_tables(l):
    k = np.arange(l, dtype=np.int64)
    ang = 2.0 * np.pi * ((k[:, None] * k[None, :]) % l).astype(np.float64) / l
    dl = np.concatenate([np.cos(ang), -np.sin(ang)], axis=1).astype(np.float32)
    c = np.arange(FOURIER_GROUP, dtype=np.int64)
    angc = 2.0 * np.pi * ((c[:, None] * c[None, :]) % FOURIER_GROUP).astype(np.float64) / FOURIER_GROUP
    return dl, np.cos(angc).astype(np.float32), np.sin(angc).astype(np.float32)


def _fourier_kernel(u_ref, cc_ref, cs_ref, dl_ref, o_ref, v_ref, *, l, scale):
    @pl.when(pl.program_id(1) == 0)
    def _():
        for j in range(N_FOURIER_GROUPS):
            cols = slice(j * FOURIER_GROUP, (j + 1) * FOURIER_GROUP)
            uj = u_ref[:, cols]
            v_ref[0:l, cols] = _dot(uj, cc_ref[...]).astype(BF16)
            v_ref[l:2 * l, cols] = _dot(uj, cs_ref[...]).astype(BF16)

    o_ref[...] = (_dot(dl_ref[...], v_ref[...]) * scale).astype(o_ref.dtype)


def _fourier_call(u, tm):
    b, l, w = u.shape
    dl, cc, cs = _dft_tables(l)
    dl = jnp.asarray(dl).astype(BF16)
    cc = jnp.asarray(cc).astype(BF16)
    cs = jnp.asarray(cs).astype(BF16)
    scale = 1.0 / math.sqrt(l * FOURIER_GROUP)
    return pl.pallas_call(
        functools.partial(_fourier_kernel, l=l, scale=scale),
        grid=(b, l // tm),
        in_specs=[pl.BlockSpec((None, l, w), lambda i, m: (i, 0, 0)),
                  pl.BlockSpec((FOURIER_GROUP, FOURIER_GROUP), lambda i, m: (0, 0)),
                  pl.BlockSpec((FOURIER_GROUP, FOURIER_GROUP), lambda i, m: (0, 0)),
                  pl.BlockSpec((tm, 2 * l), lambda i, m: (m, 0))],
        out_specs=pl.BlockSpec((None, tm, w), lambda i, m: (i, m, 0)),
        out_shape=jax.ShapeDtypeStruct((b, l, w), BF16),
        scratch_shapes=[pltpu.VMEM((2 * l, w), BF16)],
        compiler_params=_cparams(("parallel", "arbitrary")),
        name="fourier",
    )(u, cc, cs, dl)


def _expand_matrix(n_outer, inner):
    r = np.arange(n_outer * inner)[:, None]
    c = np.arange(n_outer * GPT * inner)[None, :]
    same = (r // inner == c // (GPT * inner)) & (r % inner == c % inner)
    return jnp.asarray(same.astype(np.float32)).astype(BF16)


def _spread_groups(x, expand, row_inner, col_inner):
    y = jnp.einsum('jrc,cn->jrn', x, expand, preferred_element_type=BF16)
    r = lax.broadcasted_iota(jnp.int32, y.shape, 1)
    c = lax.broadcasted_iota(jnp.int32, y.shape, 2)
    return jnp.where((r // row_inner) % GPT == (c // col_inner) % GPT, y, jnp.zeros_like(y))


def _s5_tables(a_re, a_im, log_dt, b_re, b_im, c_re, c_im):
    hp = lax.Precision.HIGHEST
    q = CHUNK
    g = a_re.shape[1]
    dt = jnp.exp(log_dt)[..., None]
    lr, li = a_re * dt, a_im * dt
    em1_r = jnp.expm1(lr) * jnp.cos(li) - 2.0 * jnp.sin(0.5 * li) ** 2
    em1_i = jnp.exp(lr) * jnp.sin(li)
    den = a_re * a_re + a_im * a_im
    fr = (em1_r * a_re + em1_i * a_im) / den
    fi = (em1_i * a_re - em1_r * a_im) / den
    bbr = fr[..., None] * b_re - fi[..., None] * b_im
    bbi = fr[..., None] * b_im + fi[..., None] * b_re
    k = jnp.arange(q + 1, dtype=F32)
    mag = jnp.exp(lr[..., None] * k)
    pr = mag * jnp.cos(li[..., None] * k)
    pi = mag * jnp.sin(li[..., None] * k)

    prk = jnp.moveaxis(pr, -1, 2)[..., None]
    pik = jnp.moveaxis(pi, -1, 2)[..., None]
    wr = prk * bbr[:, :, None] - pik * bbi[:, :, None]
    wi = prk * bbi[:, :, None] + pik * bbr[:, :, None]
    kern = (jnp.einsum('dgop,dgtph->dgtoh', c_re, wr[:, :, :q], precision=hp)
            - jnp.einsum('dgop,dgtph->dgtoh', c_im, wi[:, :, :q], precision=hp))
    qi = jnp.arange(q)
    tau_f = qi[None, :] - qi[:, None]
    m_f = jnp.where((tau_f >= 0)[None, :, :, None, None], kern[0][:, jnp.clip(tau_f, 0, q - 1)], 0.0)
    m_b = jnp.where((tau_f <= 0)[None, :, :, None, None], kern[1][:, jnp.clip(-tau_f, 0, q - 1)], 0.0)
    mt = (m_f + m_b).transpose(0, 1, 4, 2, 3).reshape(g, q * SSM_GROUP, q * SSM_GROUP)

    def inject(w):
        wf = w[0][:, q - 1 - qi]
        wb = w[1][:, qi]
        both = jnp.stack([wf, wb])
        return both.transpose(0, 1, 2, 4, 3).reshape(2, g, q * SSM_GROUP, SSM_STATE)

    wre, wim = inject(wr), inject(wi)

    def carry_out(sign):
        outs = []
        for d, idx in ((0, qi + 1), (1, q - qi)):
            ppr = pr[d][:, :, idx]
            ppi = pi[d][:, :, idx]
            cr = c_re[d].transpose(0, 2, 1)[:, :, None, :]
            ci = c_im[d].transpose(0, 2, 1)[:, :, None, :]
            if sign > 0:
                val = cr * ppr[..., None] - ci * ppi[..., None]
            else:
                val = -(cr * ppi[..., None] + ci * ppr[..., None])
            outs.append(val.reshape(g, SSM_STATE, q * SSM_GROUP))
        return jnp.stack(outs)

    cre, cim = carry_out(+1), carry_out(-1)

    nt = g // GPT
    qh = q * SSM_GROUP
    m_c = (mt.astype(BF16).reshape(nt, GPT, q, SSM_GROUP, qh).transpose(0, 2, 1, 3, 4)
           .reshape(nt, q * LANES, qh))
    big_m = _spread_groups(m_c, _expand_matrix(q, SSM_GROUP), SSM_GROUP, SSM_GROUP)
    wc = jnp.stack([wre[0], wim[0], wre[1], wim[1]]).astype(BF16)
    wc = (wc.reshape(N_COMP, nt, GPT, q, SSM_GROUP, SSM_STATE).transpose(1, 3, 2, 4, 0, 5)
          .reshape(nt, q * LANES, N_COMP * SSM_STATE))
    big_w = _spread_groups(wc, _expand_matrix(N_COMP, SSM_STATE), SSM_GROUP, SSM_STATE)
    cc = jnp.stack([cre[0], cim[0], cre[1], cim[1]]).astype(BF16)
    cc = (cc.reshape(N_COMP, nt, GPT, SSM_STATE, qh).transpose(1, 0, 2, 3, 4)
          .reshape(nt, N_COMP * GPT * SSM_STATE, qh))
    big_c = _spread_groups(cc, _expand_matrix(q, SSM_GROUP), SSM_STATE, SSM_GROUP)
    decay = jnp.stack([pr[0, ..., q], pi[0, ..., q], pr[1, ..., q], pi[1, ..., q]])
    decay = decay.reshape(N_COMP, nt, 1, GPT * SSM_STATE).transpose(1, 0, 2, 3)
    return big_m, big_w, big_c, decay


def _chunk_rows(u_ref):
    nc, _, nb, lanes = u_ref.shape
    return jnp.concatenate([u_ref[:, qq].reshape(nc * nb, lanes) for qq in range(CHUNK)], axis=-1)


def _s5_inject_kernel(u_ref, w_ref, s_ref):
    s = _dot(_chunk_rows(u_ref), w_ref[...])
    width = s_ref.shape[-1]
    for comp in range(N_COMP):
        s_ref[comp] = s[:, comp * width:(comp + 1) * width]


def _s5_scan_kernel(s_ref, a_ref, o_ref, *, batch, n_lat_chunks, n_ctx_chunks):
    backward = pl.program_id(1) == 1
    ar, ai = a_ref[0], a_ref[1]

    def step(i, carry):
        sr, si = carry
        k = i - n_ctx_chunks
        fwd = jnp.where(i < n_ctx_chunks, n_lat_chunks + i, k)
        bwd = jnp.where(i < n_ctx_chunks, n_lat_chunks + n_ctx_chunks - 1 - i, n_lat_chunks - 1 - k)
        rows = pl.ds(pl.multiple_of(jnp.where(backward, bwd, fwd) * batch, batch), batch)
        o_ref[0, rows, :] = sr.astype(o_ref.dtype)
        o_ref[1, rows, :] = si.astype(o_ref.dtype)
        return (ar * sr - ai * si + s_ref[0, rows, :], ar * si + ai * sr + s_ref[1, rows, :])

    z = jnp.zeros((batch, s_ref.shape[-1]), F32)
    lax.fori_loop(0, n_lat_chunks + n_ctx_chunks, step, (z, z))


def _s5_out_kernel(u_ref, sp_ref, m_ref, c_ref, d_ref, o_ref):
    nc, _, nb, lanes = u_ref.shape
    width = sp_ref.shape[-1]
    y = _dot(_chunk_rows(u_ref), m_ref[...])
    for comp in range(N_COMP):
        y = y + _dot(sp_ref[comp], c_ref[comp * width:(comp + 1) * width, :])
    for qq in range(CHUNK):
        yq = y[:, qq * lanes:(qq + 1) * lanes].reshape(nc, nb, lanes)
        o_ref[:, qq] = jax.nn.gelu(d_ref[...] * u_ref[:, qq].astype(F32) + yq).astype(o_ref.dtype)


def _s5_call(u_ctx, u_lat, d_skip, tables, layer):
    big_m, big_w, big_c, decay = tables
    b, lc, w = u_ctx.shape
    l = u_lat.shape[1]
    nt = w // LANES
    n_lat, n_ctx = l // CHUNK, lc // CHUNK
    n_chunks = n_lat + n_ctx
    rows = n_chunks * b
    sw = GPT * SSM_STATE
    u_t = jnp.concatenate([u_lat, u_ctx], axis=1).transpose(1, 0, 2).reshape(n_chunks, CHUNK, b, w)
    cb = _pick_tile(n_chunks, 18)
    rb = cb * b
    u_spec = pl.BlockSpec((cb, CHUNK, b, LANES), lambda j, r: (r, 0, 0, j))
    op_spec = lambda shape: pl.BlockSpec((None, None) + shape, lambda j, r: (layer, j, 0, 0))
    s_loc = pl.pallas_call(
        _s5_inject_kernel,
        grid=(nt, n_chunks // cb),
        in_specs=[u_spec, op_spec(big_w.shape[2:])],
        out_specs=pl.BlockSpec((None, N_COMP, rb, sw), lambda j, r: (j, 0, r, 0)),
        out_shape=jax.ShapeDtypeStruct((nt, N_COMP, rows, sw), F32),
        compiler_params=_cparams(("parallel", "parallel")),
        name="s5_inject",
    )(u_t, big_w)
    s_prev = pl.pallas_call(
        functools.partial(_s5_scan_kernel, batch=b, n_lat_chunks=n_lat, n_ctx_chunks=n_ctx),
        grid=(nt, 2),
        in_specs=[pl.BlockSpec((None, 2, rows, sw), lambda j, d: (j, d, 0, 0)),
                  pl.BlockSpec((None, None, 2, 1, sw), lambda j, d: (layer, j, d, 0, 0))],
        out_specs=pl.BlockSpec((None, 2, rows, sw), lambda j, d: (j, d, 0, 0)),
        out_shape=jax.ShapeDtypeStruct((nt, N_COMP, rows, sw), BF16),
        compiler_params=_cparams(("parallel", "parallel")),
        name="s5_scan",
    )(s_loc, decay)
    y = pl.pallas_call(
        _s5_out_kernel,
        grid=(nt, n_chunks // cb),
        in_specs=[u_spec,
                  pl.BlockSpec((None, N_COMP, rb, sw), lambda j, r: (j, 0, r, 0)),
                  op_spec(big_m.shape[2:]), op_spec(big_c.shape[2:]),
                  pl.BlockSpec((None, 1, LANES), lambda j, r: (layer, 0, j))],
        out_specs=u_spec,
        out_shape=jax.ShapeDtypeStruct((n_chunks, CHUNK, b, w), BF16),
        compiler_params=_cparams(("parallel", "parallel")),
        name="s5_out",
    )(u_t, s_prev, big_m, big_c, d_skip)
    return y.reshape(l + lc, b, w).transpose(1, 0, 2)


def _merge_kernel(*refs, with_router):
    (h_ref, yf_ref, ys_ref, ga_ref, gb_ref, g1_ref, sc2_ref, sh2_ref, n2g_ref,
     wf_ref, wa_ref, wb_ref, wo_ref) = refs[:13]
    rest = refs[13:]
    if with_router:
        wr_ref, br_ref, h1_ref, f_ref, lg_ref = rest
    else:
        h1_ref, f_ref = rest
    ys = ys_ref[...]
    ya = _dot(yf_ref[...], wf_ref[...])
    yb = _dot(ys, wa_ref[...]) * jax.nn.sigmoid(_dot(ys, wb_ref[...]))
    m = (ga_ref[...].astype(F32) * ya + gb_ref[...].astype(F32) * yb).astype(BF16)
    h1 = h_ref[...] + g1_ref[...] * _dot(m, wo_ref[...])
    h1_ref[...] = h1
    f = _rms_mod(h1, n2g_ref[...], sc2_ref[...], sh2_ref[...])
    f_ref[...] = f.astype(f_ref.dtype)
    if with_router:
        lg_ref[...] = _dot3(f, wr_ref[...]) + br_ref[...]


def _merge_call(h, yf, ys_all, ys_row0, ga, gb, mods, g1_row, sc2_row, sh2_row, n2g, wf, wa, wb, wo, layer,
                router, tm):
    b, l, d = h.shape
    w = yf.shape[-1]
    tok = lambda width: pl.BlockSpec((None, tm, width), lambda i, m: (i, m, 0))
    vec = lambda row: pl.BlockSpec((None, 1, d), lambda i, m: (row(i), 0, 0))
    full = lambda a: pl.BlockSpec(a.shape, lambda i, m: (0,) * a.ndim)
    per_layer = lambda a: pl.BlockSpec((None,) + a.shape[1:], lambda i, m: (layer,) + (0,) * (a.ndim - 1))
    blk0 = ys_row0 // tm
    assert blk0 * tm == ys_row0
    args = [h, yf, ys_all, ga, gb, mods, mods, mods, n2g, wf, wa, wb, wo]
    in_specs = [tok(d), tok(w), pl.BlockSpec((None, tm, w), lambda i, m: (i, m + blk0, 0)), tok(d), tok(d),
                vec(g1_row), vec(sc2_row), vec(sh2_row), per_layer(n2g), per_layer(wf), per_layer(wa),
                per_layer(wb), per_layer(wo)]
    out_specs = [tok(d), tok(d)]
    f_dtype = F32 if router is not None else BF16
    out_shape = [jax.ShapeDtypeStruct((b, l, d), F32), jax.ShapeDtypeStruct((b, l, d), f_dtype)]
    if router is not None:
        wr, br = router
        args += [wr, br]
        in_specs += [full(wr), full(br)]
        out_specs.append(tok(ROUTER_PAD))
        out_shape.append(jax.ShapeDtypeStruct((b, l, ROUTER_PAD), F32))
    return pl.pallas_call(
        functools.partial(_merge_kernel, with_router=router is not None),
        grid=(b, l // tm),
        in_specs=in_specs, out_specs=out_specs, out_shape=out_shape,
        compiler_params=_cparams(("parallel", "parallel")),
        name="merge",
    )(*args)


R_E1, R_E2, R_RANK1, R_RANK2, R_W1, R_W2 = range(6)


def _route_kernel(lg_ref, rec_ref, cnt_ref, carry_ref):
    @pl.when(pl.program_id(0) == 0)
    def _():
        carry_ref[...] = jnp.zeros_like(carry_ref)

    lg = lg_ref[...]
    tm = lg.shape[0]
    lane = lax.broadcasted_iota(jnp.int32, lg.shape, 1)
    m1 = jnp.max(lg, axis=-1, keepdims=True)
    i1 = jnp.min(jnp.where(lg == m1, lane, ROUTER_PAD), axis=-1, keepdims=True)
    lg2 = jnp.where(lane == i1, -jnp.inf, lg)
    m2 = jnp.max(lg2, axis=-1, keepdims=True)
    i2 = jnp.min(jnp.where(lg2 == m2, lane, ROUTER_PAD), axis=-1, keepdims=True)
    e = jnp.exp(m2 - m1)
    w1 = 1.0 / (1.0 + e)
    w2 = e * w1

    oh1 = lane == i1
    oh2 = lane == i2
    row = lax.broadcasted_iota(jnp.int32, (tm, tm), 0)
    col = lax.broadcasted_iota(jnp.int32, (tm, tm), 1)
    below = (row > col).astype(BF16)
    p1 = _dot(below, oh1.astype(BF16))
    p2 = _dot(below, oh2.astype(BF16))
    c1 = jnp.sum(oh1.astype(F32), axis=0, keepdims=True)
    c2 = jnp.sum(oh2.astype(F32), axis=0, keepdims=True)
    base = carry_ref[...]
    r1 = jnp.sum(jnp.where(oh1, p1 + base, 0.0), axis=-1, keepdims=True)
    r2 = jnp.sum(jnp.where(oh2, p2 + (base + c1), 0.0), axis=-1, keepdims=True)
    total = base + c1 + c2
    carry_ref[...] = total
    cnt_ref[...] = total

    rec = jnp.zeros_like(lg)
    for slot, val in ((R_E1, i1.astype(F32)), (R_E2, i2.astype(F32)), (R_RANK1, r1), (R_RANK2, r2),
                      (R_W1, w1), (R_W2, w2)):
        rec = jnp.where(lane == slot, val, rec)
    rec_ref[...] = rec


def _route_call(logits, tm):
    t, n = logits.shape
    return pl.pallas_call(
        _route_kernel,
        grid=(t // tm,),
        in_specs=[pl.BlockSpec((tm, n), lambda i: (i, 0))],
        out_specs=[pl.BlockSpec((tm, n), lambda i: (i, 0)), pl.BlockSpec((1, n), lambda i: (0, 0))],
        out_shape=[jax.ShapeDtypeStruct((t, n), F32), jax.ShapeDtypeStruct((1, n), F32)],
        scratch_shapes=[pltpu.VMEM((1, n), F32)],
        compiler_params=_cparams(("arbitrary",)),
        name="route",
    )(logits)


def _dispatch_kernel(pos_ref, f_ref, xs_in_ref, xs_ref, sem):
    del xs_in_ref
    tm = f_ref.shape[0]
    base = pl.program_id(0) * (2 * tm)

    def row_copy(r, k):
        return pltpu.make_async_copy(f_ref.at[pl.ds(r, 1)], xs_ref.at[pl.ds(pos_ref[base + 2 * r + k], 1)], sem)

    def issue(r, carry):
        row_copy(r, 0).start()
        row_copy(r, 1).start()
        return carry

    lax.fori_loop(0, tm, issue, 0, unroll=8)
    for _ in range(2):
        pltpu.make_async_copy(f_ref, xs_ref.at[pl.ds(0, tm)], sem).wait()


def _dispatch_call(pos, f, n_rows, tm):
    t, d = f.shape
    return pl.pallas_call(
        _dispatch_kernel,
        grid_spec=pltpu.PrefetchScalarGridSpec(
            num_scalar_prefetch=1,
            grid=(t // tm,),
            in_specs=[pl.BlockSpec((tm, d), lambda i, pos: (i, 0)),
                      pl.BlockSpec(memory_space=pl.ANY)],
            out_specs=pl.BlockSpec(memory_space=pl.ANY),
            scratch_shapes=[pltpu.SemaphoreType.DMA]),
        out_shape=jax.ShapeDtypeStruct((n_rows, d), f.dtype),
        input_output_aliases={2: 0},
        compiler_params=pltpu.CompilerParams(dimension_semantics=("arbitrary",), vmem_limit_bytes=VMEM_LIMIT,
                                             disable_bounds_checks=True),
        name="dispatch",
    )(pos, f, jnp.zeros((n_rows, d), f.dtype))


def _gffn_kernel(te_ref, nu_ref, x_ref, wg_ref, wu_ref, wd_ref, o_ref, xb_ref):
    del te_ref
    j = pl.program_id(1)

    @pl.when(pl.program_id(0) < nu_ref[0])
    def _():
        @pl.when(j == 0)
        def _():
            xb_ref[...] = x_ref[...].astype(BF16)
            o_ref[...] = jnp.zeros_like(o_ref)

        x = xb_ref[...]
        hid = jax.nn.silu(_dot(x, wg_ref[...].astype(BF16))) * _dot(x, wu_ref[...].astype(BF16))
        o_ref[...] += _dot(hid.astype(BF16), wd_ref[...].astype(BF16))


def _gffn_call(tile_expert, n_used, xs, wg, wu, wd, idx, tm, tf):
    n_rows, d = xs.shape
    f = wg.shape[-1]
    nj = f // tf

    def row_map(i, j, te, nu):
        return jnp.minimum(i, nu[0] - 1), 0

    def jj(i, j, nu):
        return jnp.where(i < nu[0], j, nj - 1)

    return pl.pallas_call(
        _gffn_kernel,
        grid_spec=pltpu.PrefetchScalarGridSpec(
            num_scalar_prefetch=2,
            grid=(n_rows // tm, nj),
            in_specs=[pl.BlockSpec((tm, d), row_map),
                      pl.BlockSpec((None, None, d, tf), lambda i, j, te, nu: (idx, te[i], 0, jj(i, j, nu))),
                      pl.BlockSpec((None, None, d, tf), lambda i, j, te, nu: (idx, te[i], 0, jj(i, j, nu))),
                      pl.BlockSpec((None, None, tf, d), lambda i, j, te, nu: (idx, te[i], jj(i, j, nu), 0))],
            out_specs=pl.BlockSpec((tm, d), row_map),
            scratch_shapes=[pltpu.VMEM((tm, d), BF16)]),
        out_shape=jax.ShapeDtypeStruct((n_rows, d), F32),
        input_output_aliases={2: 0},
        compiler_params=_cparams(("arbitrary", "arbitrary")),
        name="gffn",
    )(tile_expert, n_used, xs, wg, wu, wd)


def _combine_kernel(pos_ref, ys_ref, h_ref, rec_ref, g2_ref, fg_ref, o_ref, ybuf_ref, sem, *, final_norm):
    tm = h_ref.shape[0]
    base = pl.program_id(0) * (2 * tm)

    def row_copy(r, k):
        return pltpu.make_async_copy(ys_ref.at[pl.ds(pos_ref[base + 2 * r + k], 1)],
                                     ybuf_ref.at[k, pl.ds(r, 1)], sem)

    def issue(r, carry):
        row_copy(r, 0).start()
        row_copy(r, 1).start()
        return carry

    lax.fori_loop(0, tm, issue, 0, unroll=8)
    for k in range(2):
        pltpu.make_async_copy(ys_ref.at[pl.ds(0, tm)], ybuf_ref.at[k], sem).wait()
    rec = rec_ref[...]
    y = rec[:, R_W1:R_W1 + 1] * ybuf_ref[0] + rec[:, R_W2:R_W2 + 1] * ybuf_ref[1]
    out = h_ref[...] + g2_ref[...] * y
    if final_norm:
        out = out * lax.rsqrt(jnp.mean(out * out, axis=-1, keepdims=True) + EPS) * fg_ref[...]
    o_ref[...] = out


def _combine_call(pos, ys, h, rec, mods, g2_row, tiles_per_batch, final_g, tm):
    t, d = h.shape
    fg = (final_g if final_g is not None else jnp.ones((d,), F32)).reshape(1, d)
    return pl.pallas_call(
        functools.partial(_combine_kernel, final_norm=final_g is not None),
        grid_spec=pltpu.PrefetchScalarGridSpec(
            num_scalar_prefetch=1,
            grid=(t // tm,),
            in_specs=[pl.BlockSpec(memory_space=pl.ANY),
                      pl.BlockSpec((tm, d), lambda i, pos: (i, 0)),
                      pl.BlockSpec((tm, ROUTER_PAD), lambda i, pos: (i, 0)),
                      pl.BlockSpec((None, 1, d), lambda i, pos: (g2_row(i // tiles_per_batch), 0, 0)),
                      pl.BlockSpec((1, d), lambda i, pos: (0, 0))],
            out_specs=pl.BlockSpec((tm, d), lambda i, pos: (i, 0)),
            scratch_shapes=[pltpu.VMEM((2, tm, d), F32), pltpu.SemaphoreType.DMA]),
        out_shape=jax.ShapeDtypeStruct((t, d), F32),
        compiler_params=pltpu.CompilerParams(dimension_semantics=("arbitrary",), vmem_limit_bytes=VMEM_LIMIT,
                                             disable_bounds_checks=True),
        name="combine",
    )(pos, ys, h, rec, mods, fg)


def _moe_plan(rec, counts, tm, n_tiles):
    cnt = counts[0, :N_EXPERTS].astype(jnp.int32)
    nt = (cnt + (tm - 1)) // tm
    cum = jnp.cumsum(nt)
    start = (cum - nt) * tm
    e = rec[:, R_E1:R_E2 + 1].astype(jnp.int32)
    rank = rec[:, R_RANK1:R_RANK2 + 1].astype(jnp.int32)
    ex = lax.broadcasted_iota(jnp.int32, e.shape + (N_EXPERTS,), 2)
    pos = rank + jnp.sum(jnp.where(e[..., None] == ex, start, 0), axis=-1)
    tile = jnp.arange(n_tiles, dtype=jnp.int32)
    te = jnp.sum((tile[:, None] >= cum[None, :]).astype(jnp.int32), axis=1)
    n_used = cum[-1:]
    last_e = jnp.sum((n_used - 1 >= cum).astype(jnp.int32))
    return pos.reshape(-1), jnp.minimum(te, last_e), n_used


def _ffn_kernel(x_ref, wg_ref, wu_ref, wd_ref, h_ref, g2_ref, fg_ref, o_ref, acc_ref, *, final_norm):
    j = pl.program_id(1)

    @pl.when(j == 0)
    def _():
        acc_ref[...] = jnp.zeros_like(acc_ref)

    x = x_ref[...]
    hid = jax.nn.silu(_dot(x, wg_ref[...].astype(BF16))) * _dot(x, wu_ref[...].astype(BF16))
    acc_ref[...] += _dot(hid.astype(BF16), wd_ref[...].astype(BF16))

    @pl.when(j == pl.num_programs(1) - 1)
    def _():
        out = h_ref[...] + g2_ref[...] * acc_ref[...]
        if final_norm:
            out = out * lax.rsqrt(jnp.mean(out * out, axis=-1, keepdims=True) + EPS) * fg_ref[...]
        o_ref[...] = out


def _ffn_call(x, wg, wu, wd, idx, h, mods, g2_row, tiles_per_batch, final_g, tm, tf):
    t, d = x.shape
    f = wg.shape[-1]
    tok = lambda width: pl.BlockSpec((tm, width), lambda m, j: (m, 0))
    fg = (final_g if final_g is not None else jnp.ones((d,), F32)).reshape(1, d)
    return pl.pallas_call(
        functools.partial(_ffn_kernel, final_norm=final_g is not None),
        grid=(t // tm, f // tf),
        in_specs=[tok(d),
                  pl.BlockSpec((None, d, tf), lambda m, j: (idx, 0, j)),
                  pl.BlockSpec((None, d, tf), lambda m, j: (idx, 0, j)),
                  pl.BlockSpec((None, tf, d), lambda m, j: (idx, j, 0)),
                  tok(d),
                  pl.BlockSpec((None, 1, d), lambda m, j: (g2_row(m // tiles_per_batch), 0, 0)),
                  pl.BlockSpec((1, d), lambda m, j: (0, 0))],
        out_specs=tok(d),
        out_shape=jax.ShapeDtypeStruct((t, d), F32),
        scratch_shapes=[pltpu.VMEM((tm, d), F32)],
        compiler_params=_cparams(("parallel", "arbitrary")),
        name="ffn",
    )(x, wg, wu, wd, h, mods, fg)


def _pos_table(n, dim):
    t = np.arange(n)
    r = (t // GRID_W).astype(np.float32)
    col = (t % GRID_W).astype(np.float32)
    quarter = dim // 4
    omega = (1.0 / (POS_BASE ** (np.arange(quarter, dtype=np.float32) / quarter))).astype(np.float32)
    ar = r[:, None] * omega
    ac = col[:, None] * omega
    return np.concatenate([np.sin(ar), np.cos(ar), np.sin(ac), np.cos(ac)], axis=-1).astype(np.float32)


def _pick_tile(n, pref):
    return pref if n % pref == 0 else n


def kernel(x, c, ctx, c_ctx, w_mod, b_mod, norm1_g, norm2_g, w_in, w_four, ssm_a_re, ssm_a_im, ssm_log_dt, ssm_b_re, ssm_b_im, ssm_c_re, ssm_c_im, ssm_d, w_glu_a, w_glu_b, w_out, ffn_w_gate, ffn_w_up, ffn_w_down, moe_w_router, moe_b_router, moe_w_gate, moe_w_up, moe_w_down, final_g):
    b, l, d = x.shape
    lc = ctx.shape[1]
    depth = w_mod.shape[0]
    wf_cols = w_four.shape[1]
    ws_cols = ssm_d.shape[1]
    off_ga = wf_cols + ws_cols
    tm_lat = _pick_tile(l, 512)
    tm_ctx = _pick_tile(lc, 256)

    n_cond = b + 8
    cond = jnp.concatenate([c, jnp.broadcast_to(c_ctx[None], (n_cond - b, d))], axis=0)
    pos = jnp.asarray(_pos_table(l, d))

    mods = _mods_call(cond, w_mod, b_mod)
    w_in_b, wf, wa, wb, wo = (w.astype(BF16) for w in (w_in, w_four, w_glu_a, w_glu_b, w_out))
    n1g, n2g = norm1_g.reshape(depth, 1, d), norm2_g.reshape(depth, 1, d)
    d_skip = ssm_d.reshape(depth, 1, ws_cols)
    tables = jax.vmap(_s5_tables)(ssm_a_re, ssm_a_im, ssm_log_dt, ssm_b_re, ssm_b_im, ssm_c_re, ssm_c_im)

    def mod_rows(layer, which):
        return (lambda i: (layer * n_cond + i) * N_MODS + which,
                lambda i: (layer * n_cond + b) * N_MODS + which)

    h = x
    z = ctx
    for layer in range(depth):
        need_ctx = layer < depth - 1
        sh1, sc1, g1, sh2, sc2, g2 = (mod_rows(layer, k) for k in range(N_MODS))
        widths = (wf_cols, ws_cols, d, d)
        acts = (False, False, True, True)

        outs = _inproj_call(h, pos if layer == 0 else None, mods, sc1[0], sh1[0], n1g, w_in_b, layer, 0,
                            widths, acts, tm_lat)
        if layer == 0:
            h, outs = outs[0], outs[1:]
        pf, ps, ga, gb = outs
        if need_ctx:
            cpf, cps, cga, cgb = _inproj_call(z, None, mods, sc1[1], sh1[1], n1g, w_in_b, layer, 0,
                                              widths, acts, tm_ctx)
        else:
            (cps,) = _inproj_call(z, None, mods, sc1[1], sh1[1], n1g, w_in_b, layer, wf_cols,
                                  (ws_cols,), (False,), tm_ctx)

        ys_all = _s5_call(cps, ps, d_skip, tables, layer)
        yf = _fourier_call(pf, tm_lat)

        moe = layer % 2 == 1
        idx = layer // 2
        router = None
        if moe:
            wr = jnp.pad(moe_w_router[idx], ((0, 0), (0, ROUTER_PAD - N_EXPERTS)))
            br = jnp.pad(moe_b_router[idx][None], ((0, 0), (0, ROUTER_PAD - N_EXPERTS)), constant_values=NEG_BIG)
            router = (wr, br)
        res = _merge_call(h, yf, ys_all, 0, ga, gb, mods, g1[0], sc2[0], sh2[0], n2g, wf, wa, wb, wo, layer,
                          router, tm_lat)
        last = layer == depth - 1
        fin = final_g if last else None
        t = b * l
        if moe:
            h1, f, logits = res
            tm_moe = min(1024, max(128, t // 8))
            n_tiles = 2 * t // tm_moe + N_EXPERTS
            rec, counts = _route_call(logits.reshape(t, ROUTER_PAD), _pick_tile(t, 512))
            pos, tile_expert, n_used = _moe_plan(rec, counts, tm_moe, n_tiles)
            xs = _dispatch_call(pos, f.reshape(t, d), n_tiles * tm_moe, _pick_tile(t, 512))
            ys = _gffn_call(tile_expert, n_used, xs, moe_w_gate, moe_w_up, moe_w_down, idx,
                            tm_moe, _pick_tile(moe_w_gate.shape[-1], 256))
            tm_c = _pick_tile(l, 512)
            h = _combine_call(pos, ys, h1.reshape(t, d), rec, mods, g2[0], l // tm_c, fin, tm_c).reshape(b, l, d)
        else:
            h1, f = res
            ffn_w = (ffn_w_gate, ffn_w_up, ffn_w_down, idx)
            tf = _pick_tile(ffn_w_gate.shape[-1], 256)
            tm_f = _pick_tile(l, 1024)
            h = _ffn_call(f.reshape(t, d), *ffn_w, h1.reshape(t, d), mods, g2[0], l // tm_f, fin,
                          tm_f, tf).reshape(b, l, d)

        if need_ctx:
            if moe:
                raise NotImplementedError("context tokens through an expert layer")
            cyf = _fourier_call(cpf, tm_ctx)
            z1, cf = _merge_call(z, cyf, ys_all, l, cga, cgb, mods, g1[1], sc2[1], sh2[1], n2g, wf, wa, wb, wo,
                                 layer, None, tm_ctx)
            z = _ffn_call(cf.reshape(b * lc, d), *ffn_w, z1.reshape(b * lc, d), mods, g2[1], 1, None,
                          _pick_tile(b * lc, 1024), tf).reshape(b, lc, d)
    return h
```

```python
import functools
import math

import numpy as np
import jax
import jax.numpy as jnp
from jax import lax
from jax.experimental import pallas as pl
from jax.experimental.pallas import tpu as pltpu

F32 = jnp.float32
BF16 = jnp.bfloat16

GRID_W = 64
N_FOURIER_GROUPS = 4
FOURIER_GROUP = 128
SSM_GROUP = 16
SSM_STATE = 64
N_EXPERTS = 8
EPS = 1e-6
POS_BASE = 10000.0

CHUNK = 16
LANES = 128
GPT = LANES // SSM_GROUP
N_COMP = 4
ROUTER_PAD = LANES
NEG_BIG = -1e30
VMEM_LIMIT = 56 * 1024 * 1024


def _cparams(sem):
    return pltpu.CompilerParams(dimension_semantics=sem, vmem_limit_bytes=VMEM_LIMIT)


def _dot(a, b):
    return jnp.dot(a, b, preferred_element_type=F32)


def _split_bf16(a):
    hi = a.astype(BF16)
    lo = (a - hi.astype(F32)).astype(BF16)
    return hi, lo


def _dot3(a, b):
    ah, al = _split_bf16(a)
    bh, bl = _split_bf16(b)
    return _dot(ah, bh) + (_dot(ah, bl) + _dot(al, bh))


def _rms_mod(h, g, sc, sh):
    y = h * lax.rsqrt(jnp.mean(h * h, axis=-1, keepdims=True) + EPS)
    return (y * g) * (1.0 + sc) + sh


def _mods_kernel(c_ref, w_ref, b_ref, o_ref):
    s = jax.nn.silu(c_ref[...])
    o_ref[...] = _dot3(s, w_ref[...]) + b_ref[...]


N_MODS = 6
MOD_SHIFT1, MOD_SCALE1, MOD_GATE1, MOD_SHIFT2, MOD_SCALE2, MOD_GATE2 = range(N_MODS)


def _mods_call(cond, w_mod, b_mod):
    r, d = cond.shape
    depth, _, n = w_mod.shape
    tn = 1024
    out = pl.pallas_call(
        _mods_kernel,
        grid=(depth, n // tn),
        in_specs=[pl.BlockSpec((r, d), lambda i, j: (0, 0)),
                  pl.BlockSpec((None, d, tn), lambda i, j: (i, 0, j)),
                  pl.BlockSpec((None, 1, tn), lambda i, j: (i, 0, j))],
        out_specs=pl.BlockSpec((None, r, tn), lambda i, j: (i, 0, j)),
        out_shape=jax.ShapeDtypeStruct((depth, r, n), F32),
        compiler_params=_cparams(("arbitrary", "arbitrary")),
        name="mods",
    )(cond, w_mod, b_mod.reshape(depth, 1, n))
    return out.reshape(depth * r * N_MODS, 1, d)


def _inproj_kernel(*refs, acts, has_pos):
    it = iter(refs)
    x_ref = next(it)
    pos_ref = next(it) if has_pos else None
    sc_ref, sh_ref, g_ref, w_ref = next(it), next(it), next(it), next(it)
    h_out = next(it) if has_pos else None
    outs = list(it)
    h = x_ref[...]
    if has_pos:
        h = h + pos_ref[...]
        h_out[...] = h
    a = _rms_mod(h, g_ref[...], sc_ref[...], sh_ref[...]).astype(BF16)
    col = 0
    for o, act in zip(outs, acts):
        width = o.shape[-1]
        p = _dot(a, w_ref[:, col:col + width])
        if act:
            p = jax.nn.sigmoid(p)
        o[...] = p.astype(o.dtype)
        col += width


def _inproj_call(x, pos, mods, sc_row, sh_row, g, w, layer, col0, widths, acts, tm):
    b, l, d = x.shape
    has_pos = pos is not None
    n_cols = sum(widths)
    col_blk = col0 // n_cols
    assert col_blk * n_cols == col0
    tok = pl.BlockSpec((None, tm, d), lambda i, m: (i, m, 0))
    vec = lambda row: pl.BlockSpec((None, 1, d), lambda i, m: (row(i), 0, 0))
    in_specs = [tok]
    args = [x]
    if has_pos:
        in_specs.append(pl.BlockSpec((tm, d), lambda i, m: (m, 0)))
        args.append(pos)
    in_specs += [vec(sc_row), vec(sh_row), pl.BlockSpec((None, 1, d), lambda i, m: (layer, 0, 0)),
                 pl.BlockSpec((None, d, n_cols), lambda i, m: (layer, 0, col_blk))]
    args += [mods, mods, g, w]
    out_specs, out_shape = [], []
    if has_pos:
        out_specs.append(tok)
        out_shape.append(jax.ShapeDtypeStruct((b, l, d), F32))
    for width in widths:
        out_specs.append(pl.BlockSpec((None, tm, width), lambda i, m: (i, m, 0)))
        out_shape.append(jax.ShapeDtypeStruct((b, l, width), BF16))
    return pl.pallas_call(
        functools.partial(_inproj_kernel, acts=tuple(acts), has_pos=has_pos),
        grid=(b, l // tm),
        in_specs=in_specs, out_specs=out_specs, out_shape=out_shape,
        compiler_params=_cparams(("parallel", "parallel")),
        name="inproj",
    )(*args)


DFT_PAD_ROWS = 16


def _dft_tables(l):
    k = np.arange(l // 2 + DFT_PAD_ROWS, dtype=np.int64)
    n = np.arange(l, dtype=np.int64)
    ang = 2.0 * np.pi * ((k[:, None] * n[None, :]) % l).astype(np.float64) / l
    dl = np.concatenate([np.cos(ang), np.sin(ang)], axis=1).astype(np.float32)
    c = np.arange(FOURIER_GROUP, dtype=np.int64)
    angc = 2.0 * np.pi * ((c[:, None] * c[None, :]) % FOURIER_GROUP).astype(np.float64) / FOURIER_GROUP
    return dl, np.cos(angc).astype(np.float32), np.sin(angc).astype(np.float32)


def _fourier_kernel(u_ref, cc_ref, cs_ref, dl_ref, nxt_ref, o_ref, v_ref, *, l, scale):
    m = pl.program_id(1)
    tk = dl_ref.shape[0]

    @pl.when(m == 0)
    def _():
        for j in range(N_FOURIER_GROUPS):
            cols = slice(j * FOURIER_GROUP, (j + 1) * FOURIER_GROUP)
            uj = u_ref[:, cols]
            v_ref[0:l, cols] = _dot(uj, cc_ref[...]).astype(BF16)
            v_ref[l:2 * l, cols] = _dot(uj, cs_ref[...]).astype(BF16)

    p = _dot(dl_ref[:, :l], v_ref[0:l, :])
    q = _dot(dl_ref[:, l:], v_ref[l:2 * l, :])
    o_ref[pl.ds(pl.multiple_of(m * tk, tk), tk), :] = ((p - q) * scale).astype(o_ref.dtype)
    row = lax.broadcasted_iota(jnp.int32, (tk, tk), 0)
    col = lax.broadcasted_iota(jnp.int32, (tk, tk), 1)
    flip = (row + col == tk).astype(BF16)
    mirrored = _dot(flip, ((p + q) * scale).astype(BF16))
    nxt = _dot(nxt_ref[:, :l], v_ref[0:l, :]) + _dot(nxt_ref[:, l:], v_ref[l:2 * l, :])
    first = lax.broadcasted_iota(jnp.int32, mirrored.shape, 0) == 0
    upper = jnp.where(first, nxt[0:1] * scale, mirrored)
    o_ref[pl.ds(pl.multiple_of(l - (m + 1) * tk, tk), tk), :] = upper.astype(o_ref.dtype)


def _fourier_call(u, tm):
    b, l, w = u.shape
    dl, cc, cs = _dft_tables(l)
    dl = jnp.asarray(dl).astype(BF16)
    cc = jnp.asarray(cc).astype(BF16)
    cs = jnp.asarray(cs).astype(BF16)
    scale = 1.0 / math.sqrt(l * FOURIER_GROUP)
    tk = min(tm, l // 2)
    nxt_blocks = tk // DFT_PAD_ROWS
    return pl.pallas_call(
        functools.partial(_fourier_kernel, l=l, scale=scale),
        grid=(b, l // 2 // tk),
        in_specs=[pl.BlockSpec((None, l, w), lambda i, m: (i, 0, 0)),
                  pl.BlockSpec((FOURIER_GROUP, FOURIER_GROUP), lambda i, m: (0, 0)),
                  pl.BlockSpec((FOURIER_GROUP, FOURIER_GROUP), lambda i, m: (0, 0)),
                  pl.BlockSpec((tk, 2 * l), lambda i, m: (m, 0)),
                  pl.BlockSpec((DFT_PAD_ROWS, 2 * l), lambda i, m: ((m + 1) * nxt_blocks, 0))],
        out_specs=pl.BlockSpec((None, l, w), lambda i, m: (i, 0, 0)),
        out_shape=jax.ShapeDtypeStruct((b, l, w), BF16),
        scratch_shapes=[pltpu.VMEM((2 * l, w), BF16)],
        compiler_params=_cparams(("parallel", "arbitrary")),
        name="fourier",
    )(u, cc, cs, dl, dl)


def _expand_matrix(n_outer, inner):
    r = np.arange(n_outer * inner)[:, None]
    c = np.arange(n_outer * GPT * inner)[None, :]
    same = (r // inner == c // (GPT * inner)) & (r % inner == c % inner)
    return jnp.asarray(same.astype(np.float32)).astype(BF16)


def _spread_groups(x, expand, row_inner, col_inner):
    y = jnp.einsum('jrc,cn->jrn', x, expand, preferred_element_type=BF16)
    r = lax.broadcasted_iota(jnp.int32, y.shape, 1)
    c = lax.broadcasted_iota(jnp.int32, y.shape, 2)
    return jnp.where((r // row_inner) % GPT == (c // col_inner) % GPT, y, jnp.zeros_like(y))


def _s5_tables(a_re, a_im, log_dt, b_re, b_im, c_re, c_im):
    hp = lax.Precision.HIGHEST
    q = CHUNK
    g = a_re.shape[1]
    dt = jnp.exp(log_dt)[..., None]
    lr, li = a_re * dt, a_im * dt
    em1_r = jnp.expm1(lr) * jnp.cos(li) - 2.0 * jnp.sin(0.5 * li) ** 2
    em1_i = jnp.exp(lr) * jnp.sin(li)
    den = a_re * a_re + a_im * a_im
    fr = (em1_r * a_re + em1_i * a_im) / den
    fi = (em1_i * a_re - em1_r * a_im) / den
    bbr = fr[..., None] * b_re - fi[..., None] * b_im
    bbi = fr[..., None] * b_im + fi[..., None] * b_re
    k = jnp.arange(q + 1, dtype=F32)
    mag = jnp.exp(lr[..., None] * k)
    pr = mag * jnp.cos(li[..., None] * k)
    pi = mag * jnp.sin(li[..., None] * k)

    prk = jnp.moveaxis(pr, -1, 2)[..., None]
    pik = jnp.moveaxis(pi, -1, 2)[..., None]
    wr = prk * bbr[:, :, None] - pik * bbi[:, :, None]
    wi = prk * bbi[:, :, None] + pik * bbr[:, :, None]
    kern = (jnp.einsum('dgop,dgtph->dgtoh', c_re, wr[:, :, :q], precision=hp)
            - jnp.einsum('dgop,dgtph->dgtoh', c_im, wi[:, :, :q], precision=hp))
    qi = jnp.arange(q)
    nt = g // GPT
    kf = kern[0].transpose(0, 1, 3, 2)
    kb = kern[1].transpose(0, 1, 3, 2)
    lag = jnp.concatenate([kb[:, :0:-1], kf[:, :1] + kb[:, :1], kf[:, 1:]], axis=1).astype(BF16)
    lag = lag.reshape(nt, GPT, 2 * q - 1, SSM_GROUP, SSM_GROUP).transpose(0, 2, 1, 3, 4)
    eye = jnp.eye(GPT, dtype=BF16)
    lag_tile = (lag[..., None, :] * eye[:, None, :, None]).reshape(nt, 2 * q - 1, LANES, LANES)
    big_m = (lag_tile[:, qi[None, :] - qi[:, None] + (q - 1)].transpose(0, 1, 3, 2, 4)
             .reshape(nt, q * LANES, q * LANES))

    def inject(w):
        wf = w[0][:, q - 1 - qi]
        wb = w[1][:, qi]
        both = jnp.stack([wf, wb])
        return both.transpose(0, 1, 2, 4, 3).reshape(2, g, q * SSM_GROUP, SSM_STATE)

    wre, wim = inject(wr), inject(wi)

    def carry_out(sign):
        outs = []
        for d, idx in ((0, qi + 1), (1, q - qi)):
            ppr = pr[d][:, :, idx]
            ppi = pi[d][:, :, idx]
            cr = c_re[d].transpose(0, 2, 1)[:, :, None, :]
            ci = c_im[d].transpose(0, 2, 1)[:, :, None, :]
            if sign > 0:
                val = cr * ppr[..., None] - ci * ppi[..., None]
            else:
                val = -(cr * ppi[..., None] + ci * ppr[..., None])
            outs.append(val.reshape(g, SSM_STATE, q * SSM_GROUP))
        return jnp.stack(outs)

    cre, cim = carry_out(+1), carry_out(-1)

    qh = q * SSM_GROUP
    wc = jnp.stack([wre[0], wim[0], wre[1], wim[1]]).astype(BF16)
    wc = (wc.reshape(N_COMP, nt, GPT, q, SSM_GROUP, SSM_STATE).transpose(1, 3, 2, 4, 0, 5)
          .reshape(nt, q * LANES, N_COMP * SSM_STATE))
    big_w = _spread_groups(wc, _expand_matrix(N_COMP, SSM_STATE), SSM_GROUP, SSM_STATE)
    cc = jnp.stack([cre[0], cim[0], cre[1], cim[1]]).astype(BF16)
    cc = (cc.reshape(N_COMP, nt, GPT, SSM_STATE, qh).transpose(1, 0, 2, 3, 4)
          .reshape(nt, N_COMP * GPT * SSM_STATE, qh))
    big_c = _spread_groups(cc, _expand_matrix(q, SSM_GROUP), SSM_STATE, SSM_GROUP)
    decay = jnp.stack([pr[0, ..., q], pi[0, ..., q], pr[1, ..., q], pi[1, ..., q]])
    decay = decay.reshape(N_COMP, nt, 1, GPT * SSM_STATE).transpose(1, 0, 2, 3)
    return big_m, big_w, big_c, decay


def _chunk_rows(u_ref):
    nc, _, nb, lanes = u_ref.shape
    return jnp.concatenate([u_ref[:, qq].reshape(nc * nb, lanes) for qq in range(CHUNK)], axis=-1)


def _s5_inject_kernel(u_ref, w_ref, s_ref):
    s = _dot(_chunk_rows(u_ref), w_ref[...])
    width = s_ref.shape[-1]
    for comp in range(N_COMP):
        s_ref[comp] = s[:, comp * width:(comp + 1) * width]


def _s5_scan_kernel(s_ref, a_ref, o_ref, *, batch, n_lat_chunks, n_ctx_chunks):
    backward = pl.program_id(1) == 1
    ar, ai = a_ref[0], a_ref[1]

    def step(i, carry):
        sr, si = carry
        k = i - n_ctx_chunks
        fwd = jnp.where(i < n_ctx_chunks, n_lat_chunks + i, k)
        bwd = jnp.where(i < n_ctx_chunks, n_lat_chunks + n_ctx_chunks - 1 - i, n_lat_chunks - 1 - k)
        rows = pl.ds(pl.multiple_of(jnp.where(backward, bwd, fwd) * batch, batch), batch)
        o_ref[0, rows, :] = sr.astype(o_ref.dtype)
        o_ref[1, rows, :] = si.astype(o_ref.dtype)
        return (ar * sr - ai * si + s_ref[0, rows, :], ar * si + ai * sr + s_ref[1, rows, :])

    z = jnp.zeros((batch, s_ref.shape[-1]), F32)
    lax.fori_loop(0, n_lat_chunks + n_ctx_chunks, step, (z, z))


def _s5_out_kernel(u_ref, sp_ref, m_ref, c_ref, d_ref, o_ref):
    nc, _, nb, lanes = u_ref.shape
    width = sp_ref.shape[-1]
    y = _dot(_chunk_rows(u_ref), m_ref[...])
    for comp in range(N_COMP):
        y = y + _dot(sp_ref[comp], c_ref[comp * width:(comp + 1) * width, :])
    for qq in range(CHUNK):
        yq = y[:, qq * lanes:(qq + 1) * lanes].reshape(nc, nb, lanes)
        o_ref[:, qq] = jax.nn.gelu(d_ref[...] * u_ref[:, qq].astype(F32) + yq).astype(o_ref.dtype)


def _s5_call(u_ctx, u_lat, d_skip, tables, layer):
    big_m, big_w, big_c, decay = tables
    b, lc, w = u_ctx.shape
    l = u_lat.shape[1]
    nt = w // LANES
    n_lat, n_ctx = l // CHUNK, lc // CHUNK
    n_chunks = n_lat + n_ctx
    rows = n_chunks * b
    sw = GPT * SSM_STATE
    u_t = jnp.concatenate([u_lat, u_ctx], axis=1).transpose(1, 0, 2).reshape(n_chunks, CHUNK, b, w)
    cb = _pick_tile(n_chunks, 18)
    rb = cb * b
    u_spec = pl.BlockSpec((cb, CHUNK, b, LANES), lambda j, r: (r, 0, 0, j))
    op_spec = lambda shape: pl.BlockSpec((None, None) + shape, lambda j, r: (layer, j, 0, 0))
    s_loc = pl.pallas_call(
        _s5_inject_kernel,
        grid=(nt, n_chunks // cb),
        in_specs=[u_spec, op_spec(big_w.shape[2:])],
        out_specs=pl.BlockSpec((None, N_COMP, rb, sw), lambda j, r: (j, 0, r, 0)),
        out_shape=jax.ShapeDtypeStruct((nt, N_COMP, rows, sw), F32),
        compiler_params=_cparams(("parallel", "parallel")),
        name="s5_inject",
    )(u_t, big_w)
    s_prev = pl.pallas_call(
        functools.partial(_s5_scan_kernel, batch=b, n_lat_chunks=n_lat, n_ctx_chunks=n_ctx),
        grid=(nt, 2),
        in_specs=[pl.BlockSpec((None, 2, rows, sw), lambda j, d: (j, d, 0, 0)),
                  pl.BlockSpec((None, None, 2, 1, sw), lambda j, d: (layer, j, d, 0, 0))],
        out_specs=pl.BlockSpec((None, 2, rows, sw), lambda j, d: (j, d, 0, 0)),
        out_shape=jax.ShapeDtypeStruct((nt, N_COMP, rows, sw), BF16),
        compiler_params=_cparams(("parallel", "parallel")),
        name="s5_scan",
    )(s_loc, decay)
    y = pl.pallas_call(
        _s5_out_kernel,
        grid=(nt, n_chunks // cb),
        in_specs=[u_spec,
                  pl.BlockSpec((None, N_COMP, rb, sw), lambda j, r: (j, 0, r, 0)),
                  op_spec(big_m.shape[2:]), op_spec(big_c.shape[2:]),
                  pl.BlockSpec((None, 1, LANES), lambda j, r: (layer, 0, j))],
        out_specs=u_spec,
        out_shape=jax.ShapeDtypeStruct((n_chunks, CHUNK, b, w), BF16),
        compiler_params=_cparams(("parallel", "parallel")),
        name="s5_out",
    )(u_t, s_prev, big_m, big_c, d_skip)
    return y.reshape(l + lc, b, w).transpose(1, 0, 2)


def _merge_kernel(*refs, with_router):
    (h_ref, yf_ref, ys_ref, ga_ref, gb_ref, g1_ref, sc2_ref, sh2_ref, n2g_ref,
     wf_ref, wa_ref, wb_ref, wo_ref) = refs[:13]
    rest = refs[13:]
    if with_router:
        wr_ref, br_ref, h1_ref, f_ref, lg_ref = rest
    else:
        h1_ref, f_ref = rest
    ys = ys_ref[...]
    ya = _dot(yf_ref[...], wf_ref[...])
    yb = _dot(ys, wa_ref[...]) * jax.nn.sigmoid(_dot(ys, wb_ref[...]))
    m = (ga_ref[...].astype(F32) * ya + gb_ref[...].astype(F32) * yb).astype(BF16)
    h1 = h_ref[...] + g1_ref[...] * _dot(m, wo_ref[...])
    h1_ref[...] = h1
    f = _rms_mod(h1, n2g_ref[...], sc2_ref[...], sh2_ref[...])
    f_ref[...] = f.astype(f_ref.dtype)
    if with_router:
        lg_ref[...] = _dot3(f, wr_ref[...]) + br_ref[...]


def _merge_call(h, yf, ys_all, ys_row0, ga, gb, mods, g1_row, sc2_row, sh2_row, n2g, wf, wa, wb, wo, layer,
                router, tm):
    b, l, d = h.shape
    w = yf.shape[-1]
    tok = lambda width: pl.BlockSpec((None, tm, width), lambda i, m: (i, m, 0))
    vec = lambda row: pl.BlockSpec((None, 1, d), lambda i, m: (row(i), 0, 0))
    full = lambda a: pl.BlockSpec(a.shape, lambda i, m: (0,) * a.ndim)
    per_layer = lambda a: pl.BlockSpec((None,) + a.shape[1:], lambda i, m: (layer,) + (0,) * (a.ndim - 1))
    blk0 = ys_row0 // tm
    assert blk0 * tm == ys_row0
    args = [h, yf, ys_all, ga, gb, mods, mods, mods, n2g, wf, wa, wb, wo]
    in_specs = [tok(d), tok(w), pl.BlockSpec((None, tm, w), lambda i, m: (i, m + blk0, 0)), tok(d), tok(d),
                vec(g1_row), vec(sc2_row), vec(sh2_row), per_layer(n2g), per_layer(wf), per_layer(wa),
                per_layer(wb), per_layer(wo)]
    out_specs = [tok(d), tok(d)]
    f_dtype = F32 if router is not None else BF16
    out_shape = [jax.ShapeDtypeStruct((b, l, d), F32), jax.ShapeDtypeStruct((b, l, d), f_dtype)]
    if router is not None:
        wr, br = router
        args += [wr, br]
        in_specs += [full(wr), full(br)]
        out_specs.append(tok(ROUTER_PAD))
        out_shape.append(jax.ShapeDtypeStruct((b, l, ROUTER_PAD), F32))
    return pl.pallas_call(
        functools.partial(_merge_kernel, with_router=router is not None),
        grid=(b, l // tm),
        in_specs=in_specs, out_specs=out_specs, out_shape=out_shape,
        compiler_params=_cparams(("parallel", "parallel")),
        name="merge",
    )(*args)


R_E1, R_E2, R_RANK1, R_RANK2, R_W1, R_W2 = range(6)


def _route_kernel(lg_ref, rec_ref, cnt_ref, carry_ref):
    @pl.when(pl.program_id(0) == 0)
    def _():
        carry_ref[...] = jnp.zeros_like(carry_ref)

    lg = lg_ref[...]
    tm = lg.shape[0]
    lane = lax.broadcasted_iota(jnp.int32, lg.shape, 1)
    m1 = jnp.max(lg, axis=-1, keepdims=True)
    i1 = jnp.min(jnp.where(lg == m1, lane, ROUTER_PAD), axis=-1, keepdims=True)
    lg2 = jnp.where(lane == i1, -jnp.inf, lg)
    m2 = jnp.max(lg2, axis=-1, keepdims=True)
    i2 = jnp.min(jnp.where(lg2 == m2, lane, ROUTER_PAD), axis=-1, keepdims=True)
    e = jnp.exp(m2 - m1)
    w1 = 1.0 / (1.0 + e)
    w2 = e * w1

    oh1 = lane == i1
    oh2 = lane == i2
    row = lax.broadcasted_iota(jnp.int32, (tm, tm), 0)
    col = lax.broadcasted_iota(jnp.int32, (tm, tm), 1)
    below = (row > col).astype(BF16)
    p1 = _dot(below, oh1.astype(BF16))
    p2 = _dot(below, oh2.astype(BF16))
    c1 = jnp.sum(oh1.astype(F32), axis=0, keepdims=True)
    c2 = jnp.sum(oh2.astype(F32), axis=0, keepdims=True)
    base = carry_ref[...]
    r1 = jnp.sum(jnp.where(oh1, p1 + base, 0.0), axis=-1, keepdims=True)
    r2 = jnp.sum(jnp.where(oh2, p2 + (base + c1), 0.0), axis=-1, keepdims=True)
    total = base + c1 + c2
    carry_ref[...] = total
    cnt_ref[...] = total

    rec = jnp.zeros_like(lg)
    for slot, val in ((R_E1, i1.astype(F32)), (R_E2, i2.astype(F32)), (R_RANK1, r1), (R_RANK2, r2),
                      (R_W1, w1), (R_W2, w2)):
        rec = jnp.where(lane == slot, val, rec)
    rec_ref[...] = rec


def _route_call(logits, tm):
    t, n = logits.shape
    return pl.pallas_call(
        _route_kernel,
        grid=(t // tm,),
        in_specs=[pl.BlockSpec((tm, n), lambda i: (i, 0))],
        out_specs=[pl.BlockSpec((tm, n), lambda i: (i, 0)), pl.BlockSpec((1, n), lambda i: (0, 0))],
        out_shape=[jax.ShapeDtypeStruct((t, n), F32), jax.ShapeDtypeStruct((1, n), F32)],
        scratch_shapes=[pltpu.VMEM((1, n), F32)],
        compiler_params=_cparams(("arbitrary",)),
        name="route",
    )(logits)


def _dispatch_kernel(pos_ref, f_ref, xs_in_ref, xs_ref, sem):
    del xs_in_ref
    tm = f_ref.shape[0]
    base = pl.program_id(0) * (2 * tm)

    def row_copy(r, k):
        return pltpu.make_async_copy(f_ref.at[pl.ds(r, 1)], xs_ref.at[pl.ds(pos_ref[base + 2 * r + k], 1)], sem)

    def issue(r, carry):
        row_copy(r, 0).start()
        row_copy(r, 1).start()
        return carry

    lax.fori_loop(0, tm, issue, 0, unroll=8)
    for _ in range(2):
        pltpu.make_async_copy(f_ref, xs_ref.at[pl.ds(0, tm)], sem).wait()


def _dispatch_call(pos, f, n_rows, tm):
    t, d = f.shape
    return pl.pallas_call(
        _dispatch_kernel,
        grid_spec=pltpu.PrefetchScalarGridSpec(
            num_scalar_prefetch=1,
            grid=(t // tm,),
            in_specs=[pl.BlockSpec((tm, d), lambda i, pos: (i, 0)),
                      pl.BlockSpec(memory_space=pl.ANY)],
            out_specs=pl.BlockSpec(memory_space=pl.ANY),
            scratch_shapes=[pltpu.SemaphoreType.DMA]),
        out_shape=jax.ShapeDtypeStruct((n_rows, d), f.dtype),
        input_output_aliases={2: 0},
        compiler_params=pltpu.CompilerParams(dimension_semantics=("arbitrary",), vmem_limit_bytes=VMEM_LIMIT,
                                             disable_bounds_checks=True),
        name="dispatch",
    )(pos, f, jnp.zeros((n_rows, d), f.dtype))


def _gffn_kernel(te_ref, nu_ref, x_ref, wg_ref, wu_ref, wd_ref, o_ref, xb_ref):
    del te_ref
    j = pl.program_id(1)

    @pl.when(pl.program_id(0) < nu_ref[0])
    def _():
        @pl.when(j == 0)
        def _():
            xb_ref[...] = x_ref[...].astype(BF16)
            o_ref[...] = jnp.zeros_like(o_ref)

        x = xb_ref[...]
        hid = jax.nn.silu(_dot(x, wg_ref[...].astype(BF16))) * _dot(x, wu_ref[...].astype(BF16))
        o_ref[...] += _dot(hid.astype(BF16), wd_ref[...].astype(BF16))


def _gffn_call(tile_expert, n_used, xs, wg, wu, wd, idx, tm, tf):
    n_rows, d = xs.shape
    f = wg.shape[-1]
    nj = f // tf

    def row_map(i, j, te, nu):
        return jnp.minimum(i, nu[0] - 1), 0

    def jj(i, j, nu):
        return jnp.where(i < nu[0], j, nj - 1)

    return pl.pallas_call(
        _gffn_kernel,
        grid_spec=pltpu.PrefetchScalarGridSpec(
            num_scalar_prefetch=2,
            grid=(n_rows // tm, nj),
            in_specs=[pl.BlockSpec((tm, d), row_map),
                      pl.BlockSpec((None, None, d, tf), lambda i, j, te, nu: (idx, te[i], 0, jj(i, j, nu))),
                      pl.BlockSpec((None, None, d, tf), lambda i, j, te, nu: (idx, te[i], 0, jj(i, j, nu))),
                      pl.BlockSpec((None, None, tf, d), lambda i, j, te, nu: (idx, te[i], jj(i, j, nu), 0))],
            out_specs=pl.BlockSpec((tm, d), row_map),
            scratch_shapes=[pltpu.VMEM((tm, d), BF16)]),
        out_shape=jax.ShapeDtypeStruct((n_rows, d), F32),
        input_output_aliases={2: 0},
        compiler_params=_cparams(("arbitrary", "arbitrary")),
        name="gffn",
    )(tile_expert, n_used, xs, wg, wu, wd)


def _combine_kernel(pos_ref, ys_ref, h_ref, rec_ref, g2_ref, fg_ref, o_ref, ybuf_ref, sem, *, final_norm):
    tm = h_ref.shape[0]
    base = pl.program_id(0) * (2 * tm)

    def row_copy(r, k):
        return pltpu.make_async_copy(ys_ref.at[pl.ds(pos_ref[base + 2 * r + k], 1)],
                                     ybuf_ref.at[k, pl.ds(r, 1)], sem)

    def issue(r, carry):
        row_copy(r, 0).start()
        row_copy(r, 1).start()
        return carry

    lax.fori_loop(0, tm, issue, 0, unroll=8)
    for k in range(2):
        pltpu.make_async_copy(ys_ref.at[pl.ds(0, tm)], ybuf_ref.at[k], sem).wait()
    rec = rec_ref[...]
    y = rec[:, R_W1:R_W1 + 1] * ybuf_ref[0] + rec[:, R_W2:R_W2 + 1] * ybuf_ref[1]
    out = h_ref[...] + g2_ref[...] * y
    if final_norm:
        out = out * lax.rsqrt(jnp.mean(out * out, axis=-1, keepdims=True) + EPS) * fg_ref[...]
    o_ref[...] = out


def _combine_call(pos, ys, h, rec, mods, g2_row, tiles_per_batch, final_g, tm):
    t, d = h.shape
    fg = (final_g if final_g is not None else jnp.ones((d,), F32)).reshape(1, d)
    return pl.pallas_call(
        functools.partial(_combine_kernel, final_norm=final_g is not None),
        grid_spec=pltpu.PrefetchScalarGridSpec(
            num_scalar_prefetch=1,
            grid=(t // tm,),
            in_specs=[pl.BlockSpec(memory_space=pl.ANY),
                      pl.BlockSpec((tm, d), lambda i, pos: (i, 0)),
                      pl.BlockSpec((tm, ROUTER_PAD), lambda i, pos: (i, 0)),
                      pl.BlockSpec((None, 1, d), lambda i, pos: (g2_row(i // tiles_per_batch), 0, 0)),
                      pl.BlockSpec((1, d), lambda i, pos: (0, 0))],
            out_specs=pl.BlockSpec((tm, d), lambda i, pos: (i, 0)),
            scratch_shapes=[pltpu.VMEM((2, tm, d), F32), pltpu.SemaphoreType.DMA]),
        out_shape=jax.ShapeDtypeStruct((t, d), F32),
        compiler_params=pltpu.CompilerParams(dimension_semantics=("arbitrary",), vmem_limit_bytes=VMEM_LIMIT,
                                             disable_bounds_checks=True),
        name="combine",
    )(pos, ys, h, rec, mods, fg)


def _moe_plan(rec, counts, tm, n_tiles):
    cnt = counts[0, :N_EXPERTS].astype(jnp.int32)
    nt = (cnt + (tm - 1)) // tm
    cum = jnp.cumsum(nt)
    start = (cum - nt) * tm
    e = rec[:, R_E1:R_E2 + 1].astype(jnp.int32)
    rank = rec[:, R_RANK1:R_RANK2 + 1].astype(jnp.int32)
    ex = lax.broadcasted_iota(jnp.int32, e.shape + (N_EXPERTS,), 2)
    pos = rank + jnp.sum(jnp.where(e[..., None] == ex, start, 0), axis=-1)
    tile = jnp.arange(n_tiles, dtype=jnp.int32)
    te = jnp.sum((tile[:, None] >= cum[None, :]).astype(jnp.int32), axis=1)
    n_used = cum[-1:]
    last_e = jnp.sum((n_used - 1 >= cum).astype(jnp.int32))
    return pos.reshape(-1), jnp.minimum(te, last_e), n_used


def _ffn_kernel(x_ref, wg_ref, wu_ref, wd_ref, h_ref, g2_ref, fg_ref, o_ref, acc_ref, *, final_norm):
    j = pl.program_id(1)

    @pl.when(j == 0)
    def _():
        acc_ref[...] = jnp.zeros_like(acc_ref)

    x = x_ref[...]
    hid = jax.nn.silu(_dot(x, wg_ref[...].astype(BF16))) * _dot(x, wu_ref[...].astype(BF16))
    acc_ref[...] += _dot(hid.astype(BF16), wd_ref[...].astype(BF16))

    @pl.when(j == pl.num_programs(1) - 1)
    def _():
        out = h_ref[...] + g2_ref[...] * acc_ref[...]
        if final_norm:
            out = out * lax.rsqrt(jnp.mean(out * out, axis=-1, keepdims=True) + EPS) * fg_ref[...]
        o_ref[...] = out


def _ffn_call(x, wg, wu, wd, idx, h, mods, g2_row, tiles_per_batch, final_g, tm, tf):
    t, d = x.shape
    f = wg.shape[-1]
    tok = lambda width: pl.BlockSpec((tm, width), lambda m, j: (m, 0))
    fg = (final_g if final_g is not None else jnp.ones((d,), F32)).reshape(1, d)
    return pl.pallas_call(
        functools.partial(_ffn_kernel, final_norm=final_g is not None),
        grid=(t // tm, f // tf),
        in_specs=[tok(d),
                  pl.BlockSpec((None, d, tf), lambda m, j: (idx, 0, j)),
                  pl.BlockSpec((None, d, tf), lambda m, j: (idx, 0, j)),
                  pl.BlockSpec((None, tf, d), lambda m, j: (idx, j, 0)),
                  tok(d),
                  pl.BlockSpec((None, 1, d), lambda m, j: (g2_row(m // tiles_per_batch), 0, 0)),
                  pl.BlockSpec((1, d), lambda m, j: (0, 0))],
        out_specs=tok(d),
        out_shape=jax.ShapeDtypeStruct((t, d), F32),
        scratch_shapes=[pltpu.VMEM((tm, d), F32)],
        compiler_params=_cparams(("parallel", "arbitrary")),
        name="ffn",
    )(x, wg, wu, wd, h, mods, fg)


def _pos_table(n, dim):
    t = np.arange(n)
    r = (t // GRID_W).astype(np.float32)
    col = (t % GRID_W).astype(np.float32)
    quarter = dim // 4
    omega = (1.0 / (POS_BASE ** (np.arange(quarter, dtype=np.float32) / quarter))).astype(np.float32)
    ar = r[:, None] * omega
    ac = col[:, None] * omega
    return np.concatenate([np.sin(ar), np.cos(ar), np.sin(ac), np.cos(ac)], axis=-1).astype(np.float32)


def _pick_tile(n, pref):
    return pref if n % pref == 0 else n


def kernel(x, c, ctx, c_ctx, w_mod, b_mod, norm1_g, norm2_g, w_in, w_four, ssm_a_re, ssm_a_im, ssm_log_dt, ssm_b_re, ssm_b_im, ssm_c_re, ssm_c_im, ssm_d, w_glu_a, w_glu_b, w_out, ffn_w_gate, ffn_w_up, ffn_w_down, moe_w_router, moe_b_router, moe_w_gate, moe_w_up, moe_w_down, final_g):
    b, l, d = x.shape
    lc = ctx.shape[1]
    depth = w_mod.shape[0]
    wf_cols = w_four.shape[1]
    ws_cols = ssm_d.shape[1]
    off_ga = wf_cols + ws_cols
    tm_lat = _pick_tile(l, 512)
    tm_ctx = _pick_tile(lc, 256)

    n_cond = b + 8
    cond = jnp.concatenate([c, jnp.broadcast_to(c_ctx[None], (n_cond - b, d))], axis=0)
    pos = jnp.asarray(_pos_table(l, d))

    mods = _mods_call(cond, w_mod, b_mod)
    w_in_b, wf, wa, wb, wo = (w.astype(BF16) for w in (w_in, w_four, w_glu_a, w_glu_b, w_out))
    n1g, n2g = norm1_g.reshape(depth, 1, d), norm2_g.reshape(depth, 1, d)
    d_skip = ssm_d.reshape(depth, 1, ws_cols)
    tables = jax.vmap(_s5_tables)(ssm_a_re, ssm_a_im, ssm_log_dt, ssm_b_re, ssm_b_im, ssm_c_re, ssm_c_im)

    def mod_rows(layer, which):
        return (lambda i: (layer * n_cond + i) * N_MODS + which,
                lambda i: (layer * n_cond + b) * N_MODS + which)

    h = x
    z = ctx
    for layer in range(depth):
        need_ctx = layer < depth - 1
        sh1, sc1, g1, sh2, sc2, g2 = (mod_rows(layer, k) for k in range(N_MODS))
        widths = (wf_cols, ws_cols, d, d)
        acts = (False, False, True, True)

        outs = _inproj_call(h, pos if layer == 0 else None, mods, sc1[0], sh1[0], n1g, w_in_b, layer, 0,
                            widths, acts, tm_lat)
        if layer == 0:
            h, outs = outs[0], outs[1:]
        pf, ps, ga, gb = outs
        if need_ctx:
            cpf, cps, cga, cgb = _inproj_call(z, None, mods, sc1[1], sh1[1], n1g, w_in_b, layer, 0,
                                              widths, acts, tm_ctx)
        else:
            (cps,) = _inproj_call(z, None, mods, sc1[1], sh1[1], n1g, w_in_b, layer, wf_cols,
                                  (ws_cols,), (False,), tm_ctx)

        ys_all = _s5_call(cps, ps, d_skip, tables, layer)
        yf = _fourier_call(pf, tm_lat)

        moe = layer % 2 == 1
        idx = layer // 2
        router = None
        if moe:
            wr = jnp.pad(moe_w_router[idx], ((0, 0), (0, ROUTER_PAD - N_EXPERTS)))
            br = jnp.pad(moe_b_router[idx][None], ((0, 0), (0, ROUTER_PAD - N_EXPERTS)), constant_values=NEG_BIG)
            router = (wr, br)
        res = _merge_call(h, yf, ys_all, 0, ga, gb, mods, g1[0], sc2[0], sh2[0], n2g, wf, wa, wb, wo, layer,
                          router, tm_lat)
        last = layer == depth - 1
        fin = final_g if last else None
        t = b * l
        if moe:
            h1, f, logits = res
            tm_moe = min(1024, max(128, t // 8))
            n_tiles = 2 * t // tm_moe + N_EXPERTS
            rec, counts = _route_call(logits.reshape(t, ROUTER_PAD), _pick_tile(t, 512))
            pos, tile_expert, n_used = _moe_plan(rec, counts, tm_moe, n_tiles)
            xs = _dispatch_call(pos, f.reshape(t, d), n_tiles * tm_moe, _pick_tile(t, 512))
            ys = _gffn_call(tile_expert, n_used, xs, moe_w_gate, moe_w_up, moe_w_down, idx,
                            tm_moe, _pick_tile(moe_w_gate.shape[-1], 512))
            tm_c = _pick_tile(l, 512)
            h = _combine_call(pos, ys, h1.reshape(t, d), rec, mods, g2[0], l // tm_c, fin, tm_c).reshape(b, l, d)
        else:
            h1, f = res
            ffn_w = (ffn_w_gate, ffn_w_up, ffn_w_down, idx)
            tf = _pick_tile(ffn_w_gate.shape[-1], 256)
            tm_f = _pick_tile(l, 1024)
            h = _ffn_call(f.reshape(t, d), *ffn_w, h1.reshape(t, d), mods, g2[0], l // tm_f, fin,
                          tm_f, tf).reshape(b, l, d)

        if need_ctx:
            if moe:
                raise NotImplementedError("context tokens through an expert layer")
            cyf = _fourier_call(cpf, tm_ctx)
            z1, cf = _merge_call(z, cyf, ys_all, l, cga, cgb, mods, g1[1], sc2[1], sh2[1], n2g, wf, wa, wb, wo,
                                 layer, None, tm_ctx)
            z = _ffn_call(cf.reshape(b * lc, d), *ffn_w, z1.reshape(b * lc, d), mods, g2[1], 1, None,
                          _pick_tile(b * lc, 1024), tf).reshape(b, lc, d)
    return h
```

```python
import functools
import math

import numpy as np
import jax
import jax.numpy as jnp
from jax import lax
from jax.experimental import pallas as pl
from jax.experimental.pallas import tpu as pltpu

F32 = jnp.float32
BF16 = jnp.bfloat16

GRID_W = 64
N_FOURIER_GROUPS = 4
FOURIER_GROUP = 128
SSM_GROUP = 16
SSM_STATE = 64
N_EXPERTS = 8
EPS = 1e-6
POS_BASE = 10000.0

CHUNK = 16
LANES = 128
GPT = LANES // SSM_GROUP
N_COMP = 4
FFN_TILE = 512
ROUTER_PAD = LANES
NEG_BIG = -1e30
VMEM_LIMIT = 56 * 1024 * 1024


def _cparams(sem):
    return pltpu.CompilerParams(dimension_semantics=sem, vmem_limit_bytes=VMEM_LIMIT)


def _dot(a, b):
    return jnp.dot(a, b, preferred_element_type=F32)


def _split_bf16(a):
    hi = a.astype(BF16)
    lo = (a - hi.astype(F32)).astype(BF16)
    return hi, lo


def _dot3(a, b):
    ah, al = _split_bf16(a)
    bh, bl = _split_bf16(b)
    return _dot(ah, bh) + (_dot(ah, bl) + _dot(al, bh))


def _rms_mod(h, g, sc, sh):
    y = h * lax.rsqrt(jnp.mean(h * h, axis=-1, keepdims=True) + EPS)
    return (y * g) * (1.0 + sc) + sh


def _mods_kernel(c_ref, w_ref, b_ref, o_ref):
    s = jax.nn.silu(c_ref[...])
    o_ref[...] = _dot3(s, w_ref[...]) + b_ref[...]


N_MODS = 6
MOD_SHIFT1, MOD_SCALE1, MOD_GATE1, MOD_SHIFT2, MOD_SCALE2, MOD_GATE2 = range(N_MODS)


def _mods_call(cond, w_mod, b_mod):
    r, d = cond.shape
    depth, _, n = w_mod.shape
    tn = 1024
    out = pl.pallas_call(
        _mods_kernel,
        grid=(depth, n // tn),
        in_specs=[pl.BlockSpec((r, d), lambda i, j: (0, 0)),
                  pl.BlockSpec((None, d, tn), lambda i, j: (i, 0, j)),
                  pl.BlockSpec((None, 1, tn), lambda i, j: (i, 0, j))],
        out_specs=pl.BlockSpec((None, r, tn), lambda i, j: (i, 0, j)),
        out_shape=jax.ShapeDtypeStruct((depth, r, n), F32),
        compiler_params=_cparams(("arbitrary", "arbitrary")),
        name="mods",
    )(cond, w_mod, b_mod.reshape(depth, 1, n))
    return out.reshape(depth * r * N_MODS, 1, d)


def _inproj_kernel(*refs, acts, has_pos):
    it = iter(refs)
    x_ref = next(it)
    pos_ref = next(it) if has_pos else None
    sc_ref, sh_ref, g_ref, w_ref = next(it), next(it), next(it), next(it)
    h_out = next(it) if has_pos else None
    outs = list(it)
    h = x_ref[...]
    if has_pos:
        h = h + pos_ref[...]
        h_out[...] = h
    a = _rms_mod(h, g_ref[...], sc_ref[...], sh_ref[...]).astype(BF16)
    col = 0
    for o, act in zip(outs, acts):
        width = o.shape[-1]
        p = _dot(a, w_ref[:, col:col + width])
        if act:
            p = jax.nn.sigmoid(p)
        o[...] = p.astype(o.dtype)
        col += width


def _inproj_call(x, pos, mods, sc_row, sh_row, g, w, layer, col0, widths, acts, tm):
    b, l, d = x.shape
    has_pos = pos is not None
    n_cols = sum(widths)
    col_blk = col0 // n_cols
    assert col_blk * n_cols == col0
    tok = pl.BlockSpec((None, tm, d), lambda i, m: (i, m, 0))
    vec = lambda row: pl.BlockSpec((None, 1, d), lambda i, m: (row(i), 0, 0))
    in_specs = [tok]
    args = [x]
    if has_pos:
        in_specs.append(pl.BlockSpec((tm, d), lambda i, m: (m, 0)))
        args.append(pos)
    in_specs += [vec(sc_row), vec(sh_row), pl.BlockSpec((None, 1, d), lambda i, m: (layer, 0, 0)),
                 pl.BlockSpec((None, d, n_cols), lambda i, m: (layer, 0, col_blk))]
    args += [mods, mods, g, w]
    out_specs, out_shape = [], []
    if has_pos:
        out_specs.append(tok)
        out_shape.append(jax.ShapeDtypeStruct((b, l, d), F32))
    for width in widths:
        out_specs.append(pl.BlockSpec((None, tm, width), lambda i, m: (i, m, 0)))
        out_shape.append(jax.ShapeDtypeStruct((b, l, width), BF16))
    return pl.pallas_call(
        functools.partial(_inproj_kernel, acts=tuple(acts), has_pos=has_pos),
        grid=(b, l // tm),
        in_specs=in_specs, out_specs=out_specs, out_shape=out_shape,
        compiler_params=_cparams(("parallel", "parallel")),
        name="inproj",
    )(*args)


DFT_PAD_ROWS = 16


def _dft_tables(l):
    k = np.arange(l // 2 + DFT_PAD_ROWS, dtype=np.int64)
    n = np.arange(l, dtype=np.int64)
    ang = 2.0 * np.pi * ((k[:, None] * n[None, :]) % l).astype(np.float64) / l
    dl = np.concatenate([np.cos(ang), np.sin(ang)], axis=1).astype(np.float32)
    c = np.arange(FOURIER_GROUP, dtype=np.int64)
    angc = 2.0 * np.pi * ((c[:, None] * c[None, :]) % FOURIER_GROUP).astype(np.float64) / FOURIER_GROUP
    return dl, np.cos(angc).astype(np.float32), np.sin(angc).astype(np.float32)


def _fourier_kernel(u_ref, cc_ref, cs_ref, dl_ref, nxt_ref, o_ref, v_ref, *, l, scale):
    m = pl.program_id(1)
    tk = dl_ref.shape[0]

    @pl.when(m == 0)
    def _():
        for j in range(N_FOURIER_GROUPS):
            cols = slice(j * FOURIER_GROUP, (j + 1) * FOURIER_GROUP)
            uj = u_ref[:, cols]
            v_ref[0:l, cols] = _dot(uj, cc_ref[...]).astype(BF16)
            v_ref[l:2 * l, cols] = _dot(uj, cs_ref[...]).astype(BF16)

    p = _dot(dl_ref[:, :l], v_ref[0:l, :])
    q = _dot(dl_ref[:, l:], v_ref[l:2 * l, :])
    o_ref[pl.ds(pl.multiple_of(m * tk, tk), tk), :] = ((p - q) * scale).astype(o_ref.dtype)
    row = lax.broadcasted_iota(jnp.int32, (tk, tk), 0)
    col = lax.broadcasted_iota(jnp.int32, (tk, tk), 1)
    flip = (row + col == tk).astype(BF16)
    mirrored = _dot(flip, ((p + q) * scale).astype(BF16))
    nxt = _dot(nxt_ref[:, :l], v_ref[0:l, :]) + _dot(nxt_ref[:, l:], v_ref[l:2 * l, :])
    first = lax.broadcasted_iota(jnp.int32, mirrored.shape, 0) == 0
    upper = jnp.where(first, nxt[0:1] * scale, mirrored)
    o_ref[pl.ds(pl.multiple_of(l - (m + 1) * tk, tk), tk), :] = upper.astype(o_ref.dtype)


def _fourier_call(u, tm):
    b, l, w = u.shape
    dl, cc, cs = _dft_tables(l)
    dl = jnp.asarray(dl).astype(BF16)
    cc = jnp.asarray(cc).astype(BF16)
    cs = jnp.asarray(cs).astype(BF16)
    scale = 1.0 / math.sqrt(l * FOURIER_GROUP)
    tk = min(tm, l // 2)
    nxt_blocks = tk // DFT_PAD_ROWS
    return pl.pallas_call(
        functools.partial(_fourier_kernel, l=l, scale=scale),
        grid=(b, l // 2 // tk),
        in_specs=[pl.BlockSpec((None, l, w), lambda i, m: (i, 0, 0)),
                  pl.BlockSpec((FOURIER_GROUP, FOURIER_GROUP), lambda i, m: (0, 0)),
                  pl.BlockSpec((FOURIER_GROUP, FOURIER_GROUP), lambda i, m: (0, 0)),
                  pl.BlockSpec((tk, 2 * l), lambda i, m: (m, 0)),
                  pl.BlockSpec((DFT_PAD_ROWS, 2 * l), lambda i, m: ((m + 1) * nxt_blocks, 0))],
        out_specs=pl.BlockSpec((None, l, w), lambda i, m: (i, 0, 0)),
        out_shape=jax.ShapeDtypeStruct((b, l, w), BF16),
        scratch_shapes=[pltpu.VMEM((2 * l, w), BF16)],
        compiler_params=_cparams(("parallel", "arbitrary")),
        name="fourier",
    )(u, cc, cs, dl, dl)


def _expand_matrix(n_outer, inner):
    r = np.arange(n_outer * inner)[:, None]
    c = np.arange(n_outer * GPT * inner)[None, :]
    same = (r // inner == c // (GPT * inner)) & (r % inner == c % inner)
    return jnp.asarray(same.astype(np.float32)).astype(BF16)


def _spread_groups(x, expand, row_inner, col_inner):
    y = jnp.einsum('jrc,cn->jrn', x, expand, preferred_element_type=BF16)
    r = lax.broadcasted_iota(jnp.int32, y.shape, 1)
    c = lax.broadcasted_iota(jnp.int32, y.shape, 2)
    return jnp.where((r // row_inner) % GPT == (c // col_inner) % GPT, y, jnp.zeros_like(y))


def _s5_tables(a_re, a_im, log_dt, b_re, b_im, c_re, c_im):
    hp = lax.Precision.HIGHEST
    q = CHUNK
    g = a_re.shape[1]
    dt = jnp.exp(log_dt)[..., None]
    lr, li = a_re * dt, a_im * dt
    em1_r = jnp.expm1(lr) * jnp.cos(li) - 2.0 * jnp.sin(0.5 * li) ** 2
    em1_i = jnp.exp(lr) * jnp.sin(li)
    den = a_re * a_re + a_im * a_im
    fr = (em1_r * a_re + em1_i * a_im) / den
    fi = (em1_i * a_re - em1_r * a_im) / den
    bbr = fr[..., None] * b_re - fi[..., None] * b_im
    bbi = fr[..., None] * b_im + fi[..., None] * b_re
    k = jnp.arange(q + 1, dtype=F32)
    mag = jnp.exp(lr[..., None] * k)
    pr = mag * jnp.cos(li[..., None] * k)
    pi = mag * jnp.sin(li[..., None] * k)

    prk = jnp.moveaxis(pr, -1, 2)[..., None]
    pik = jnp.moveaxis(pi, -1, 2)[..., None]
    wr = prk * bbr[:, :, None] - pik * bbi[:, :, None]
    wi = prk * bbi[:, :, None] + pik * bbr[:, :, None]
    kern = (jnp.einsum('dgop,dgtph->dgtoh', c_re, wr[:, :, :q], precision=hp)
            - jnp.einsum('dgop,dgtph->dgtoh', c_im, wi[:, :, :q], precision=hp))
    qi = jnp.arange(q)
    nt = g // GPT
    kf = kern[0].transpose(0, 1, 3, 2)
    kb = kern[1].transpose(0, 1, 3, 2)
    lag = jnp.concatenate([kb[:, :0:-1], kf[:, :1] + kb[:, :1], kf[:, 1:]], axis=1).astype(BF16)
    lag = (lag.reshape(nt, GPT, 2 * q - 1, SSM_GROUP, SSM_GROUP).transpose(0, 2, 1, 3, 4)
           .reshape(nt * (2 * q - 1), LANES, SSM_GROUP))
    lag_tile = _spread_groups(lag, _expand_matrix(1, SSM_GROUP), SSM_GROUP, SSM_GROUP)
    lag_tile = lag_tile.reshape(nt, 2 * q - 1, LANES, LANES)
    big_m = (lag_tile[:, qi[None, :] - qi[:, None] + (q - 1)].transpose(0, 1, 3, 2, 4)
             .reshape(nt, q * LANES, q * LANES))

    def inject(w):
        wf = w[0][:, q - 1 - qi]
        wb = w[1][:, qi]
        both = jnp.stack([wf, wb])
        return both.transpose(0, 1, 2, 4, 3).reshape(2, g, q * SSM_GROUP, SSM_STATE)

    wre, wim = inject(wr), inject(wi)

    def carry_out(sign):
        outs = []
        for d, idx in ((0, qi + 1), (1, q - qi)):
            ppr = pr[d][:, :, idx]
            ppi = pi[d][:, :, idx]
            cr = c_re[d].transpose(0, 2, 1)[:, :, None, :]
            ci = c_im[d].transpose(0, 2, 1)[:, :, None, :]
            if sign > 0:
                val = cr * ppr[..., None] - ci * ppi[..., None]
            else:
                val = -(cr * ppi[..., None] + ci * ppr[..., None])
            outs.append(val.reshape(g, SSM_STATE, q * SSM_GROUP))
        return jnp.stack(outs)

    cre, cim = carry_out(+1), carry_out(-1)

    qh = q * SSM_GROUP
    wc = jnp.stack([wre[0], wim[0], wre[1], wim[1]]).astype(BF16)
    wc = (wc.reshape(N_COMP, nt, GPT, q, SSM_GROUP, SSM_STATE).transpose(1, 3, 2, 4, 0, 5)
          .reshape(nt, q * LANES, N_COMP * SSM_STATE))
    big_w = _spread_groups(wc, _expand_matrix(N_COMP, SSM_STATE), SSM_GROUP, SSM_STATE)
    cc = jnp.stack([cre[0], cim[0], cre[1], cim[1]]).astype(BF16)
    cc = (cc.reshape(N_COMP, nt, GPT, SSM_STATE, qh).transpose(1, 0, 2, 3, 4)
          .reshape(nt, N_COMP * GPT * SSM_STATE, qh))
    big_c = _spread_groups(cc, _expand_matrix(q, SSM_GROUP), SSM_STATE, SSM_GROUP)
    decay = jnp.stack([pr[0, ..., q], pi[0, ..., q], pr[1, ..., q], pi[1, ..., q]])
    decay = decay.reshape(N_COMP, nt, 1, GPT * SSM_STATE).transpose(1, 0, 2, 3)
    return big_m, big_w, big_c, decay


def _chunk_rows(u_ref):
    nc, _, nb, lanes = u_ref.shape
    return jnp.concatenate([u_ref[:, qq].reshape(nc * nb, lanes) for qq in range(CHUNK)], axis=-1)


def _s5_inject_kernel(u_ref, w_ref, s_ref):
    s = _dot(_chunk_rows(u_ref), w_ref[...])
    width = s_ref.shape[-1]
    for comp in range(N_COMP):
        s_ref[comp] = s[:, comp * width:(comp + 1) * width]


def _s5_scan_kernel(s_ref, a_ref, o_ref, *, batch, n_lat_chunks, n_ctx_chunks):
    backward = pl.program_id(1) == 1
    ar, ai = a_ref[0], a_ref[1]

    def step(i, carry):
        sr, si = carry
        k = i - n_ctx_chunks
        fwd = jnp.where(i < n_ctx_chunks, n_lat_chunks + i, k)
        bwd = jnp.where(i < n_ctx_chunks, n_lat_chunks + n_ctx_chunks - 1 - i, n_lat_chunks - 1 - k)
        rows = pl.ds(pl.multiple_of(jnp.where(backward, bwd, fwd) * batch, batch), batch)
        o_ref[0, rows, :] = sr.astype(o_ref.dtype)
        o_ref[1, rows, :] = si.astype(o_ref.dtype)
        return (ar * sr - ai * si + s_ref[0, rows, :], ar * si + ai * sr + s_ref[1, rows, :])

    z = jnp.zeros((batch, s_ref.shape[-1]), F32)
    lax.fori_loop(0, n_lat_chunks + n_ctx_chunks, step, (z, z))


def _s5_out_kernel(u_ref, sp_ref, m_ref, c_ref, d_ref, o_ref):
    nc, _, nb, lanes = u_ref.shape
    width = sp_ref.shape[-1]
    y = _dot(_chunk_rows(u_ref), m_ref[...])
    for comp in range(N_COMP):
        y = y + _dot(sp_ref[comp], c_ref[comp * width:(comp + 1) * width, :])
    for qq in range(CHUNK):
        yq = y[:, qq * lanes:(qq + 1) * lanes].reshape(nc, nb, lanes)
        o_ref[:, qq] = jax.nn.gelu(d_ref[...] * u_ref[:, qq].astype(F32) + yq).astype(o_ref.dtype)


def _s5_call(u_ctx, u_lat, d_skip, tables, layer):
    big_m, big_w, big_c, decay = tables
    b, lc, w = u_ctx.shape
    l = u_lat.shape[1]
    nt = w // LANES
    n_lat, n_ctx = l // CHUNK, lc // CHUNK
    n_chunks = n_lat + n_ctx
    rows = n_chunks * b
    sw = GPT * SSM_STATE
    u_t = jnp.concatenate([u_lat, u_ctx], axis=1).transpose(1, 0, 2).reshape(n_chunks, CHUNK, b, w)
    cb = _pick_tile(n_chunks, 18)
    rb = cb * b
    u_spec = pl.BlockSpec((cb, CHUNK, b, LANES), lambda j, r: (r, 0, 0, j))
    op_spec = lambda shape: pl.BlockSpec((None, None) + shape, lambda j, r: (layer, j, 0, 0))
    s_loc = pl.pallas_call(
        _s5_inject_kernel,
        grid=(nt, n_chunks // cb),
        in_specs=[u_spec, op_spec(big_w.shape[2:])],
        out_specs=pl.BlockSpec((None, N_COMP, rb, sw), lambda j, r: (j, 0, r, 0)),
        out_shape=jax.ShapeDtypeStruct((nt, N_COMP, rows, sw), F32),
        compiler_params=_cparams(("parallel", "parallel")),
        name="s5_inject",
    )(u_t, big_w)
    s_prev = pl.pallas_call(
        functools.partial(_s5_scan_kernel, batch=b, n_lat_chunks=n_lat, n_ctx_chunks=n_ctx),
        grid=(nt, 2),
        in_specs=[pl.BlockSpec((None, 2, rows, sw), lambda j, d: (j, d, 0, 0)),
                  pl.BlockSpec((None, None, 2, 1, sw), lambda j, d: (layer, j, d, 0, 0))],
        out_specs=pl.BlockSpec((None, 2, rows, sw), lambda j, d: (j, d, 0, 0)),
        out_shape=jax.ShapeDtypeStruct((nt, N_COMP, rows, sw), BF16),
        compiler_params=_cparams(("parallel", "parallel")),
        name="s5_scan",
    )(s_loc, decay)
    y = pl.pallas_call(
        _s5_out_kernel,
        grid=(nt, n_chunks // cb),
        in_specs=[u_spec,
                  pl.BlockSpec((None, N_COMP, rb, sw), lambda j, r: (j, 0, r, 0)),
                  op_spec(big_m.shape[2:]), op_spec(big_c.shape[2:]),
                  pl.BlockSpec((None, 1, LANES), lambda j, r: (layer, 0, j))],
        out_specs=u_spec,
        out_shape=jax.ShapeDtypeStruct((n_chunks, CHUNK, b, w), BF16),
        compiler_params=_cparams(("parallel", "parallel")),
        name="s5_out",
    )(u_t, s_prev, big_m, big_c, d_skip)
    return y.reshape(l + lc, b, w).transpose(1, 0, 2)


def _merge_kernel(*refs, with_router):
    (h_ref, yf_ref, ys_ref, ga_ref, gb_ref, g1_ref, sc2_ref, sh2_ref, n2g_ref,
     wf_ref, wa_ref, wb_ref, wo_ref) = refs[:13]
    rest = refs[13:]
    if with_router:
        wr_ref, br_ref, h1_ref, f_ref, lg_ref = rest
    else:
        h1_ref, f_ref = rest
    ys = ys_ref[...]
    ya = _dot(yf_ref[...], wf_ref[...])
    yb = _dot(ys, wa_ref[...]) * jax.nn.sigmoid(_dot(ys, wb_ref[...]))
    m = (ga_ref[...].astype(F32) * ya + gb_ref[...].astype(F32) * yb).astype(BF16)
    h1 = h_ref[...] + g1_ref[...] * _dot(m, wo_ref[...])
    h1_ref[...] = h1
    f = _rms_mod(h1, n2g_ref[...], sc2_ref[...], sh2_ref[...])
    f_ref[...] = f.astype(f_ref.dtype)
    if with_router:
        lg_ref[...] = _dot3(f, wr_ref[...]) + br_ref[...]


def _merge_call(h, yf, ys_all, ys_row0, ga, gb, mods, g1_row, sc2_row, sh2_row, n2g, wf, wa, wb, wo, layer,
                router, tm):
    b, l, d = h.shape
    w = yf.shape[-1]
    tok = lambda width: pl.BlockSpec((None, tm, width), lambda i, m: (i, m, 0))
    vec = lambda row: pl.BlockSpec((None, 1, d), lambda i, m: (row(i), 0, 0))
    full = lambda a: pl.BlockSpec(a.shape, lambda i, m: (0,) * a.ndim)
    per_layer = lambda a: pl.BlockSpec((None,) + a.shape[1:], lambda i, m: (layer,) + (0,) * (a.ndim - 1))
    blk0 = ys_row0 // tm
    assert blk0 * tm == ys_row0
    args = [h, yf, ys_all, ga, gb, mods, mods, mods, n2g, wf, wa, wb, wo]
    in_specs = [tok(d), tok(w), pl.BlockSpec((None, tm, w), lambda i, m: (i, m + blk0, 0)), tok(d), tok(d),
                vec(g1_row), vec(sc2_row), vec(sh2_row), per_layer(n2g), per_layer(wf), per_layer(wa),
                per_layer(wb), per_layer(wo)]
    out_specs = [tok(d), tok(d)]
    f_dtype = F32 if router is not None else BF16
    out_shape = [jax.ShapeDtypeStruct((b, l, d), F32), jax.ShapeDtypeStruct((b, l, d), f_dtype)]
    if router is not None:
        wr, br = router
        args += [wr, br]
        in_specs += [full(wr), full(br)]
        out_specs.append(tok(ROUTER_PAD))
        out_shape.append(jax.ShapeDtypeStruct((b, l, ROUTER_PAD), F32))
    return pl.pallas_call(
        functools.partial(_merge_kernel, with_router=router is not None),
        grid=(b, l // tm),
        in_specs=in_specs, out_specs=out_specs, out_shape=out_shape,
        compiler_params=_cparams(("parallel", "parallel")),
        name="merge",
    )(*args)


R_E1, R_E2, R_RANK1, R_RANK2, R_W1, R_W2 = range(6)


def _route_kernel(lg_ref, rec_ref, cnt_ref, carry_ref):
    @pl.when(pl.program_id(0) == 0)
    def _():
        carry_ref[...] = jnp.zeros_like(carry_ref)

    lg = lg_ref[...]
    tm = lg.shape[0]
    lane = lax.broadcasted_iota(jnp.int32, lg.shape, 1)
    m1 = jnp.max(lg, axis=-1, keepdims=True)
    i1 = jnp.min(jnp.where(lg == m1, lane, ROUTER_PAD), axis=-1, keepdims=True)
    lg2 = jnp.where(lane == i1, -jnp.inf, lg)
    m2 = jnp.max(lg2, axis=-1, keepdims=True)
    i2 = jnp.min(jnp.where(lg2 == m2, lane, ROUTER_PAD), axis=-1, keepdims=True)
    e = jnp.exp(m2 - m1)
    w1 = 1.0 / (1.0 + e)
    w2 = e * w1

    oh1 = lane == i1
    oh2 = lane == i2
    row = lax.broadcasted_iota(jnp.int32, (tm, tm), 0)
    col = lax.broadcasted_iota(jnp.int32, (tm, tm), 1)
    below = (row > col).astype(BF16)
    p1 = _dot(below, oh1.astype(BF16))
    p2 = _dot(below, oh2.astype(BF16))
    c1 = jnp.sum(oh1.astype(F32), axis=0, keepdims=True)
    c2 = jnp.sum(oh2.astype(F32), axis=0, keepdims=True)
    base = carry_ref[...]
    r1 = jnp.sum(jnp.where(oh1, p1 + base, 0.0), axis=-1, keepdims=True)
    r2 = jnp.sum(jnp.where(oh2, p2 + (base + c1), 0.0), axis=-1, keepdims=True)
    total = base + c1 + c2
    carry_ref[...] = total
    cnt_ref[...] = total

    rec = jnp.zeros_like(lg)
    for slot, val in ((R_E1, i1.astype(F32)), (R_E2, i2.astype(F32)), (R_RANK1, r1), (R_RANK2, r2),
                      (R_W1, w1), (R_W2, w2)):
        rec = jnp.where(lane == slot, val, rec)
    rec_ref[...] = rec


def _route_call(logits, tm):
    t, n = logits.shape
    return pl.pallas_call(
        _route_kernel,
        grid=(t // tm,),
        in_specs=[pl.BlockSpec((tm, n), lambda i: (i, 0))],
        out_specs=[pl.BlockSpec((tm, n), lambda i: (i, 0)), pl.BlockSpec((1, n), lambda i: (0, 0))],
        out_shape=[jax.ShapeDtypeStruct((t, n), F32), jax.ShapeDtypeStruct((1, n), F32)],
        scratch_shapes=[pltpu.VMEM((1, n), F32)],
        compiler_params=_cparams(("arbitrary",)),
        name="route",
    )(logits)


def _dispatch_kernel(pos_ref, f_ref, xs_in_ref, xs_ref, sem):
    del xs_in_ref
    tm = f_ref.shape[0]
    base = pl.program_id(0) * (2 * tm)

    def row_copy(r, k):
        return pltpu.make_async_copy(f_ref.at[pl.ds(r, 1)], xs_ref.at[pl.ds(pos_ref[base + 2 * r + k], 1)], sem)

    def issue(r, carry):
        row_copy(r, 0).start(priority=0)
        row_copy(r, 1).start(priority=1)
        return carry

    lax.fori_loop(0, tm, issue, 0, unroll=8)
    for _ in range(2):
        pltpu.make_async_copy(f_ref, xs_ref.at[pl.ds(0, tm)], sem).wait()


def _dispatch_call(pos, f, n_rows, tm):
    t, d = f.shape
    return pl.pallas_call(
        _dispatch_kernel,
        grid_spec=pltpu.PrefetchScalarGridSpec(
            num_scalar_prefetch=1,
            grid=(t // tm,),
            in_specs=[pl.BlockSpec((tm, d), lambda i, pos: (i, 0)),
                      pl.BlockSpec(memory_space=pl.ANY)],
            out_specs=pl.BlockSpec(memory_space=pl.ANY),
            scratch_shapes=[pltpu.SemaphoreType.DMA]),
        out_shape=jax.ShapeDtypeStruct((n_rows, d), f.dtype),
        input_output_aliases={2: 0},
        compiler_params=pltpu.CompilerParams(dimension_semantics=("arbitrary",), vmem_limit_bytes=VMEM_LIMIT,
                                             disable_bounds_checks=True),
        name="dispatch",
    )(pos, f, jnp.zeros((n_rows, d), f.dtype))


def _gffn_kernel(te_ref, nu_ref, x_ref, wg_ref, wu_ref, wd_ref, o_ref, xb_ref):
    del te_ref
    j = pl.program_id(1)

    @pl.when(pl.program_id(0) < nu_ref[0])
    def _():
        @pl.when(j == 0)
        def _():
            xb_ref[...] = x_ref[...].astype(BF16)
            o_ref[...] = jnp.zeros_like(o_ref)

        x = xb_ref[...]
        hid = jax.nn.silu(_dot(x, wg_ref[...].astype(BF16))) * _dot(x, wu_ref[...].astype(BF16))
        o_ref[...] += _dot(hid.astype(BF16), wd_ref[...].astype(BF16))


def _gffn_call(tile_expert, n_used, xs, wg, wu, wd, idx, tm, tf):
    n_rows, d = xs.shape
    f = wg.shape[-1]
    nj = f // tf

    def row_map(i, j, te, nu):
        return jnp.minimum(i, nu[0] - 1), 0

    def jj(i, j, nu):
        return jnp.where(i < nu[0], j, nj - 1)

    return pl.pallas_call(
        _gffn_kernel,
        grid_spec=pltpu.PrefetchScalarGridSpec(
            num_scalar_prefetch=2,
            grid=(n_rows // tm, nj),
            in_specs=[pl.BlockSpec((tm, d), row_map),
                      pl.BlockSpec((None, None, d, tf), lambda i, j, te, nu: (idx, te[i], 0, jj(i, j, nu))),
                      pl.BlockSpec((None, None, d, tf), lambda i, j, te, nu: (idx, te[i], 0, jj(i, j, nu))),
                      pl.BlockSpec((None, None, tf, d), lambda i, j, te, nu: (idx, te[i], jj(i, j, nu), 0))],
            out_specs=pl.BlockSpec((tm, d), row_map),
            scratch_shapes=[pltpu.VMEM((tm, d), BF16)]),
        out_shape=jax.ShapeDtypeStruct((n_rows, d), F32),
        input_output_aliases={2: 0},
        compiler_params=_cparams(("arbitrary", "arbitrary")),
        name="gffn",
    )(tile_expert, n_used, xs, wg, wu, wd)


def _combine_kernel(pos_ref, ys_ref, h_ref, rec_ref, g2_ref, fg_ref, o_ref, ybuf_ref, sem, *, final_norm):
    tm = h_ref.shape[0]
    base = pl.program_id(0) * (2 * tm)

    def row_copy(r, k):
        return pltpu.make_async_copy(ys_ref.at[pl.ds(pos_ref[base + 2 * r + k], 1)],
                                     ybuf_ref.at[k, pl.ds(r, 1)], sem)

    def issue(r, carry):
        row_copy(r, 0).start(priority=0)
        row_copy(r, 1).start(priority=1)
        return carry

    lax.fori_loop(0, tm, issue, 0, unroll=8)
    for k in range(2):
        pltpu.make_async_copy(ys_ref.at[pl.ds(0, tm)], ybuf_ref.at[k], sem).wait()
    rec = rec_ref[...]
    y = rec[:, R_W1:R_W1 + 1] * ybuf_ref[0] + rec[:, R_W2:R_W2 + 1] * ybuf_ref[1]
    out = h_ref[...] + g2_ref[...] * y
    if final_norm:
        out = out * lax.rsqrt(jnp.mean(out * out, axis=-1, keepdims=True) + EPS) * fg_ref[...]
    o_ref[...] = out


def _combine_call(pos, ys, h, rec, mods, g2_row, tiles_per_batch, final_g, tm):
    t, d = h.shape
    fg = (final_g if final_g is not None else jnp.ones((d,), F32)).reshape(1, d)
    return pl.pallas_call(
        functools.partial(_combine_kernel, final_norm=final_g is not None),
        grid_spec=pltpu.PrefetchScalarGridSpec(
            num_scalar_prefetch=1,
            grid=(t // tm,),
            in_specs=[pl.BlockSpec(memory_space=pl.ANY),
                      pl.BlockSpec((tm, d), lambda i, pos: (i, 0)),
                      pl.BlockSpec((tm, ROUTER_PAD), lambda i, pos: (i, 0)),
                      pl.BlockSpec((None, 1, d), lambda i, pos: (g2_row(i // tiles_per_batch), 0, 0)),
                      pl.BlockSpec((1, d), lambda i, pos: (0, 0))],
            out_specs=pl.BlockSpec((tm, d), lambda i, pos: (i, 0)),
            scratch_shapes=[pltpu.VMEM((2, tm, d), F32), pltpu.SemaphoreType.DMA]),
        out_shape=jax.ShapeDtypeStruct((t, d), F32),
        compiler_params=pltpu.CompilerParams(dimension_semantics=("arbitrary",), vmem_limit_bytes=VMEM_LIMIT,
                                             disable_bounds_checks=True),
        name="combine",
    )(pos, ys, h, rec, mods, fg)


def _moe_plan(rec, counts, tm, n_tiles):
    cnt = counts[0, :N_EXPERTS].astype(jnp.int32)
    nt = (cnt + (tm - 1)) // tm
    cum = jnp.cumsum(nt)
    start = (cum - nt) * tm
    e = rec[:, R_E1:R_E2 + 1].astype(jnp.int32)
    rank = rec[:, R_RANK1:R_RANK2 + 1].astype(jnp.int32)
    ex = lax.broadcasted_iota(jnp.int32, e.shape + (N_EXPERTS,), 2)
    pos = rank + jnp.sum(jnp.where(e[..., None] == ex, start, 0), axis=-1)
    tile = jnp.arange(n_tiles, dtype=jnp.int32)
    te = jnp.sum((tile[:, None] >= cum[None, :]).astype(jnp.int32), axis=1)
    n_used = cum[-1:]
    last_e = jnp.sum((n_used - 1 >= cum).astype(jnp.int32))
    return pos.reshape(-1), jnp.minimum(te, last_e), n_used


def _ffn_kernel(x_ref, wg_ref, wu_ref, wd_ref, h_ref, g2_ref, fg_ref, o_ref, acc_ref, *, final_norm):
    j = pl.program_id(1)

    @pl.when(j == 0)
    def _():
        acc_ref[...] = jnp.zeros_like(acc_ref)

    x = x_ref[...]
    hid = jax.nn.silu(_dot(x, wg_ref[...].astype(BF16))) * _dot(x, wu_ref[...].astype(BF16))
    acc_ref[...] += _dot(hid.astype(BF16), wd_ref[...].astype(BF16))

    @pl.when(j == pl.num_programs(1) - 1)
    def _():
        out = h_ref[...] + g2_ref[...] * acc_ref[...]
        if final_norm:
            out = out * lax.rsqrt(jnp.mean(out * out, axis=-1, keepdims=True) + EPS) * fg_ref[...]
        o_ref[...] = out


def _ffn_call(x, wg, wu, wd, idx, h, mods, g2_row, tiles_per_batch, final_g, tm, tf):
    t, d = x.shape
    f = wg.shape[-1]
    tok = lambda width: pl.BlockSpec((tm, width), lambda m, j: (m, 0))
    fg = (final_g if final_g is not None else jnp.ones((d,), F32)).reshape(1, d)
    return pl.pallas_call(
        functools.partial(_ffn_kernel, final_norm=final_g is not None),
        grid=(t // tm, f // tf),
        in_specs=[tok(d),
                  pl.BlockSpec((None, d, tf), lambda m, j: (idx, 0, j)),
                  pl.BlockSpec((None, d, tf), lambda m, j: (idx, 0, j)),
                  pl.BlockSpec((None, tf, d), lambda m, j: (idx, j, 0)),
                  tok(d),
                  pl.BlockSpec((None, 1, d), lambda m, j: (g2_row(m // tiles_per_batch), 0, 0)),
                  pl.BlockSpec((1, d), lambda m, j: (0, 0))],
        out_specs=tok(d),
        out_shape=jax.ShapeDtypeStruct((t, d), F32),
        scratch_shapes=[pltpu.VMEM((tm, d), F32)],
        compiler_params=_cparams(("parallel", "arbitrary")),
        name="ffn",
    )(x, wg, wu, wd, h, mods, fg)


def _pos_table(n, dim):
    t = np.arange(n)
    r = (t // GRID_W).astype(np.float32)
    col = (t % GRID_W).astype(np.float32)
    quarter = dim // 4
    omega = (1.0 / (POS_BASE ** (np.arange(quarter, dtype=np.float32) / quarter))).astype(np.float32)
    ar = r[:, None] * omega
    ac = col[:, None] * omega
    return np.concatenate([np.sin(ar), np.cos(ar), np.sin(ac), np.cos(ac)], axis=-1).astype(np.float32)


def _pick_tile(n, pref):
    return pref if n % pref == 0 else n


def kernel(x, c, ctx, c_ctx, w_mod, b_mod, norm1_g, norm2_g, w_in, w_four, ssm_a_re, ssm_a_im, ssm_log_dt, ssm_b_re, ssm_b_im, ssm_c_re, ssm_c_im, ssm_d, w_glu_a, w_glu_b, w_out, ffn_w_gate, ffn_w_up, ffn_w_down, moe_w_router, moe_b_router, moe_w_gate, moe_w_up, moe_w_down, final_g):
    b, l, d = x.shape
    lc = ctx.shape[1]
    depth = w_mod.shape[0]
    wf_cols = w_four.shape[1]
    ws_cols = ssm_d.shape[1]
    off_ga = wf_cols + ws_cols
    tm_lat = _pick_tile(l, 512)
    tm_ctx = _pick_tile(lc, 256)

    n_cond = b + 8
    cond = jnp.concatenate([c, jnp.broadcast_to(c_ctx[None], (n_cond - b, d))], axis=0)
    pos = jnp.asarray(_pos_table(l, d))

    mods = _mods_call(cond, w_mod, b_mod)
    w_in_b, wf, wa, wb, wo = (w.astype(BF16) for w in (w_in, w_four, w_glu_a, w_glu_b, w_out))
    n1g, n2g = norm1_g.reshape(depth, 1, d), norm2_g.reshape(depth, 1, d)
    d_skip = ssm_d.reshape(depth, 1, ws_cols)
    tables = jax.vmap(_s5_tables)(ssm_a_re, ssm_a_im, ssm_log_dt, ssm_b_re, ssm_b_im, ssm_c_re, ssm_c_im)

    f_pad = -ffn_w_gate.shape[-1] % FFN_TILE
    dense_w = (jnp.pad(ffn_w_gate, ((0, 0), (0, 0), (0, f_pad))), jnp.pad(ffn_w_up, ((0, 0), (0, 0), (0, f_pad))),
               jnp.pad(ffn_w_down, ((0, 0), (0, f_pad), (0, 0))))

    def mod_rows(layer, which):
        return (lambda i: (layer * n_cond + i) * N_MODS + which,
                lambda i: (layer * n_cond + b) * N_MODS + which)

    h = x
    z = ctx
    for layer in range(depth):
        need_ctx = layer < depth - 1
        sh1, sc1, g1, sh2, sc2, g2 = (mod_rows(layer, k) for k in range(N_MODS))
        widths = (wf_cols, ws_cols, d, d)
        acts = (False, False, True, True)

        outs = _inproj_call(h, pos if layer == 0 else None, mods, sc1[0], sh1[0], n1g, w_in_b, layer, 0,
                            widths, acts, tm_lat)
        if layer == 0:
            h, outs = outs[0], outs[1:]
        pf, ps, ga, gb = outs
        if need_ctx:
            cpf, cps, cga, cgb = _inproj_call(z, None, mods, sc1[1], sh1[1], n1g, w_in_b, layer, 0,
                                              widths, acts, tm_ctx)
        else:
            (cps,) = _inproj_call(z, None, mods, sc1[1], sh1[1], n1g, w_in_b, layer, wf_cols,
                                  (ws_cols,), (False,), tm_ctx)

        ys_all = _s5_call(cps, ps, d_skip, tables, layer)
        yf = _fourier_call(pf, tm_lat)

        moe = layer % 2 == 1
        idx = layer // 2
        router = None
        if moe:
            wr = jnp.pad(moe_w_router[idx], ((0, 0), (0, ROUTER_PAD - N_EXPERTS)))
            br = jnp.pad(moe_b_router[idx][None], ((0, 0), (0, ROUTER_PAD - N_EXPERTS)), constant_values=NEG_BIG)
            router = (wr, br)
        res = _merge_call(h, yf, ys_all, 0, ga, gb, mods, g1[0], sc2[0], sh2[0], n2g, wf, wa, wb, wo, layer,
                          router, tm_lat)
        last = layer == depth - 1
        fin = final_g if last else None
        t = b * l
        if moe:
            h1, f, logits = res
            tm_moe = min(1024, max(128, t // 8))
            n_tiles = 2 * t // tm_moe + N_EXPERTS
            rec, counts = _route_call(logits.reshape(t, ROUTER_PAD), _pick_tile(t, 512))
            pos, tile_expert, n_used = _moe_plan(rec, counts, tm_moe, n_tiles)
            xs = _dispatch_call(pos, f.reshape(t, d), n_tiles * tm_moe, _pick_tile(t, 512))
            ys = _gffn_call(tile_expert, n_used, xs, moe_w_gate, moe_w_up, moe_w_down, idx,
                            tm_moe, _pick_tile(moe_w_gate.shape[-1], FFN_TILE))
            tm_c = _pick_tile(l, 512)
            h = _combine_call(pos, ys, h1.reshape(t, d), rec, mods, g2[0], l // tm_c, fin, tm_c).reshape(b, l, d)
        else:
            h1, f = res
            ffn_w = (*dense_w, idx)
            tf = FFN_TILE
            tm_f = _pick_tile(l, 1024)
            h = _ffn_call(f.reshape(t, d), *ffn_w, h1.reshape(t, d), mods, g2[0], l // tm_f, fin,
                          tm_f, tf).reshape(b, l, d)

        if need_ctx:
            if moe:
                raise NotImplementedError("context tokens through an expert layer")
            cyf = _fourier_call(cpf, tm_ctx)
            z1, cf = _merge_call(z, cyf, ys_all, l, cga, cgb, mods, g1[1], sc2[1], sh2[1], n2g, wf, wa, wb, wo,
                                 layer, None, tm_ctx)
            z = _ffn_call(cf.reshape(b * lc, d), *ffn_w, z1.reshape(b * lc, d), mods, g2[1], 1, None,
                          _pick_tile(b * lc, 1024), tf).reshape(b, lc, d)
    return h
```

```python
import functools
import math

import numpy as np
import jax
import jax.numpy as jnp
from jax import lax
from jax.experimental import pallas as pl
from jax.experimental.pallas import tpu as pltpu

F32 = jnp.float32
BF16 = jnp.bfloat16

GRID_W = 64
N_FOURIER_GROUPS = 4
FOURIER_GROUP = 128
SSM_GROUP = 16
SSM_STATE = 64
N_EXPERTS = 8
EPS = 1e-6
POS_BASE = 10000.0

CHUNK = 16
LANES = 128
GPT = LANES // SSM_GROUP
N_COMP = 4
FFN_TILE = 512
ROUTER_PAD = LANES
NEG_BIG = -1e30
VMEM_LIMIT = 56 * 1024 * 1024


def _cparams(sem):
    return pltpu.CompilerParams(dimension_semantics=sem, vmem_limit_bytes=VMEM_LIMIT)


def _dot(a, b):
    return jnp.dot(a, b, preferred_element_type=F32)


def _split_bf16(a):
    hi = a.astype(BF16)
    lo = (a - hi.astype(F32)).astype(BF16)
    return hi, lo


def _dot3(a, b):
    ah, al = _split_bf16(a)
    bh, bl = _split_bf16(b)
    return _dot(ah, bh) + (_dot(ah, bl) + _dot(al, bh))


def _rms_mod(h, g, sc, sh):
    y = h * lax.rsqrt(jnp.mean(h * h, axis=-1, keepdims=True) + EPS)
    return (y * g) * (1.0 + sc) + sh


def _mods_kernel(c_ref, w_ref, b_ref, o_ref):
    s = jax.nn.silu(c_ref[...])
    o_ref[...] = _dot3(s, w_ref[...]) + b_ref[...]


N_MODS = 6
MOD_SHIFT1, MOD_SCALE1, MOD_GATE1, MOD_SHIFT2, MOD_SCALE2, MOD_GATE2 = range(N_MODS)


def _mods_call(cond, w_mod, b_mod):
    r, d = cond.shape
    depth, _, n = w_mod.shape
    tn = 1024
    out = pl.pallas_call(
        _mods_kernel,
        grid=(depth, n // tn),
        in_specs=[pl.BlockSpec((r, d), lambda i, j: (0, 0)),
                  pl.BlockSpec((None, d, tn), lambda i, j: (i, 0, j)),
                  pl.BlockSpec((None, 1, tn), lambda i, j: (i, 0, j))],
        out_specs=pl.BlockSpec((None, r, tn), lambda i, j: (i, 0, j)),
        out_shape=jax.ShapeDtypeStruct((depth, r, n), F32),
        compiler_params=_cparams(("arbitrary", "arbitrary")),
        name="mods",
    )(cond, w_mod, b_mod.reshape(depth, 1, n))
    return out.reshape(depth * r * N_MODS, 1, d)


def _inproj_kernel(*refs, acts, has_pos):
    it = iter(refs)
    x_ref = next(it)
    pos_ref = next(it) if has_pos else None
    sc_ref, sh_ref, g_ref, w_ref = next(it), next(it), next(it), next(it)
    h_out = next(it) if has_pos else None
    outs = list(it)
    h = x_ref[...]
    if has_pos:
        h = h + pos_ref[...]
        h_out[...] = h
    a = _rms_mod(h, g_ref[...], sc_ref[...], sh_ref[...]).astype(BF16)
    col = 0
    for o, act in zip(outs, acts):
        width = o.shape[-1]
        p = _dot(a, w_ref[:, col:col + width])
        if act:
            p = jax.nn.sigmoid(p)
        o[...] = p.astype(o.dtype)
        col += width


def _inproj_call(x, pos, mods, sc_row, sh_row, g, w, layer, col0, widths, acts, tm):
    b, l, d = x.shape
    has_pos = pos is not None
    n_cols = sum(widths)
    col_blk = col0 // n_cols
    assert col_blk * n_cols == col0
    tok = pl.BlockSpec((None, tm, d), lambda i, m: (i, m, 0))
    vec = lambda row: pl.BlockSpec((None, 1, d), lambda i, m: (row(i), 0, 0))
    in_specs = [tok]
    args = [x]
    if has_pos:
        in_specs.append(pl.BlockSpec((tm, d), lambda i, m: (m, 0)))
        args.append(pos)
    in_specs += [vec(sc_row), vec(sh_row), pl.BlockSpec((None, 1, d), lambda i, m: (layer, 0, 0)),
                 pl.BlockSpec((None, d, n_cols), lambda i, m: (layer, 0, col_blk))]
    args += [mods, mods, g, w]
    out_specs, out_shape = [], []
    if has_pos:
        out_specs.append(tok)
        out_shape.append(jax.ShapeDtypeStruct((b, l, d), F32))
    for width in widths:
        out_specs.append(pl.BlockSpec((None, tm, width), lambda i, m: (i, m, 0)))
        out_shape.append(jax.ShapeDtypeStruct((b, l, width), BF16))
    return pl.pallas_call(
        functools.partial(_inproj_kernel, acts=tuple(acts), has_pos=has_pos),
        grid=(b, l // tm),
        in_specs=in_specs, out_specs=out_specs, out_shape=out_shape,
        compiler_params=_cparams(("parallel", "parallel")),
        name="inproj",
    )(*args)


DFT_PAD_ROWS = 16


def _dft_tables(l):
    k = np.arange(l // 2 + DFT_PAD_ROWS, dtype=np.int64)
    n = np.arange(l, dtype=np.int64)
    ang = 2.0 * np.pi * ((k[:, None] * n[None, :]) % l).astype(np.float64) / l
    dl = np.concatenate([np.cos(ang), np.sin(ang)], axis=1).astype(np.float32)
    c = np.arange(FOURIER_GROUP, dtype=np.int64)
    angc = 2.0 * np.pi * ((c[:, None] * c[None, :]) % FOURIER_GROUP).astype(np.float64) / FOURIER_GROUP
    return dl, np.cos(angc).astype(np.float32), np.sin(angc).astype(np.float32)


def _fourier_kernel(u_ref, cc_ref, cs_ref, dl_ref, nxt_ref, o_ref, v_ref, *, l, scale):
    m = pl.program_id(1)
    tk = dl_ref.shape[0]

    @pl.when(m == 0)
    def _():
        for j in range(N_FOURIER_GROUPS):
            cols = slice(j * FOURIER_GROUP, (j + 1) * FOURIER_GROUP)
            uj = u_ref[:, cols]
            v_ref[0:l, cols] = _dot(uj, cc_ref[...]).astype(BF16)
            v_ref[l:2 * l, cols] = _dot(uj, cs_ref[...]).astype(BF16)

    p = _dot(dl_ref[:, :l], v_ref[0:l, :])
    q = _dot(dl_ref[:, l:], v_ref[l:2 * l, :])
    o_ref[pl.ds(pl.multiple_of(m * tk, tk), tk), :] = ((p - q) * scale).astype(o_ref.dtype)
    row = lax.broadcasted_iota(jnp.int32, (tk, tk), 0)
    col = lax.broadcasted_iota(jnp.int32, (tk, tk), 1)
    flip = (row + col == tk).astype(BF16)
    mirrored = _dot(flip, ((p + q) * scale).astype(BF16))
    nxt = _dot(nxt_ref[:, :l], v_ref[0:l, :]) + _dot(nxt_ref[:, l:], v_ref[l:2 * l, :])
    first = lax.broadcasted_iota(jnp.int32, mirrored.shape, 0) == 0
    upper = jnp.where(first, nxt[0:1] * scale, mirrored)
    o_ref[pl.ds(pl.multiple_of(l - (m + 1) * tk, tk), tk), :] = upper.astype(o_ref.dtype)


def _fourier_call(u, tm):
    b, l, w = u.shape
    dl, cc, cs = _dft_tables(l)
    dl = jnp.asarray(dl).astype(BF16)
    cc = jnp.asarray(cc).astype(BF16)
    cs = jnp.asarray(cs).astype(BF16)
    scale = 1.0 / math.sqrt(l * FOURIER_GROUP)
    tk = min(tm, l // 2)
    nxt_blocks = tk // DFT_PAD_ROWS
    return pl.pallas_call(
        functools.partial(_fourier_kernel, l=l, scale=scale),
        grid=(b, l // 2 // tk),
        in_specs=[pl.BlockSpec((None, l, w), lambda i, m: (i, 0, 0)),
                  pl.BlockSpec((FOURIER_GROUP, FOURIER_GROUP), lambda i, m: (0, 0)),
                  pl.BlockSpec((FOURIER_GROUP, FOURIER_GROUP), lambda i, m: (0, 0)),
                  pl.BlockSpec((tk, 2 * l), lambda i, m: (m, 0)),
                  pl.BlockSpec((DFT_PAD_ROWS, 2 * l), lambda i, m: ((m + 1) * nxt_blocks, 0))],
        out_specs=pl.BlockSpec((None, l, w), lambda i, m: (i, 0, 0)),
        out_shape=jax.ShapeDtypeStruct((b, l, w), BF16),
        scratch_shapes=[pltpu.VMEM((2 * l, w), BF16)],
        compiler_params=_cparams(("parallel", "arbitrary")),
        name="fourier",
    )(u, cc, cs, dl, dl)


def _expand_matrix(n_outer, inner):
    r = np.arange(n_outer * inner)[:, None]
    c = np.arange(n_outer * GPT * inner)[None, :]
    same = (r // inner == c // (GPT * inner)) & (r % inner == c % inner)
    return jnp.asarray(same.astype(np.float32)).astype(BF16)


def _spread_groups(x, expand, row_inner, col_inner):
    y = jnp.einsum('jrc,cn->jrn', x, expand, preferred_element_type=BF16)
    r = lax.broadcasted_iota(jnp.int32, y.shape, 1)
    c = lax.broadcasted_iota(jnp.int32, y.shape, 2)
    return jnp.where((r // row_inner) % GPT == (c // col_inner) % GPT, y, jnp.zeros_like(y))


def _s5_tables(a_re, a_im, log_dt, b_re, b_im, c_re, c_im):
    hp = lax.Precision.HIGHEST
    q = CHUNK
    g = a_re.shape[1]
    dt = jnp.exp(log_dt)[..., None]
    lr, li = a_re * dt, a_im * dt
    em1_r = jnp.expm1(lr) * jnp.cos(li) - 2.0 * jnp.sin(0.5 * li) ** 2
    em1_i = jnp.exp(lr) * jnp.sin(li)
    den = a_re * a_re + a_im * a_im
    fr = (em1_r * a_re + em1_i * a_im) / den
    fi = (em1_i * a_re - em1_r * a_im) / den
    bbr = fr[..., None] * b_re - fi[..., None] * b_im
    bbi = fr[..., None] * b_im + fi[..., None] * b_re
    k = jnp.arange(q + 1, dtype=F32)
    mag = jnp.exp(lr[..., None] * k)
    pr = mag * jnp.cos(li[..., None] * k)
    pi = mag * jnp.sin(li[..., None] * k)

    prk = jnp.moveaxis(pr, -1, 2)[..., None]
    pik = jnp.moveaxis(pi, -1, 2)[..., None]
    wr = prk * bbr[:, :, None] - pik * bbi[:, :, None]
    wi = prk * bbi[:, :, None] + pik * bbr[:, :, None]
    kern = (jnp.einsum('dgop,dgtph->dgtoh', c_re, wr[:, :, :q], precision=hp)
            - jnp.einsum('dgop,dgtph->dgtoh', c_im, wi[:, :, :q], precision=hp))
    qi = jnp.arange(q)
    nt = g // GPT
    kf = kern[0].transpose(0, 1, 3, 2)
    kb = kern[1].transpose(0, 1, 3, 2)
    lag = jnp.concatenate([kb[:, :0:-1], kf[:, :1] + kb[:, :1], kf[:, 1:]], axis=1).astype(BF16)
    lag = (lag.reshape(nt, GPT, 2 * q - 1, SSM_GROUP, SSM_GROUP).transpose(0, 2, 1, 3, 4)
           .reshape(nt * (2 * q - 1), LANES, SSM_GROUP))
    lag_tile = _spread_groups(lag, _expand_matrix(1, SSM_GROUP), SSM_GROUP, SSM_GROUP)
    lag_tile = lag_tile.reshape(nt, 2 * q - 1, LANES, LANES)

    def inject(w):
        wf = w[0][:, q - 1 - qi]
        wb = w[1][:, qi]
        both = jnp.stack([wf, wb])
        return both.transpose(0, 1, 2, 4, 3).reshape(2, g, q * SSM_GROUP, SSM_STATE)

    wre, wim = inject(wr), inject(wi)

    def carry_out(sign):
        outs = []
        for d, idx in ((0, qi + 1), (1, q - qi)):
            ppr = pr[d][:, :, idx]
            ppi = pi[d][:, :, idx]
            cr = c_re[d].transpose(0, 2, 1)[:, :, None, :]
            ci = c_im[d].transpose(0, 2, 1)[:, :, None, :]
            if sign > 0:
                val = cr * ppr[..., None] - ci * ppi[..., None]
            else:
                val = -(cr * ppi[..., None] + ci * ppr[..., None])
            outs.append(val.reshape(g, SSM_STATE, q * SSM_GROUP))
        return jnp.stack(outs)

    cre, cim = carry_out(+1), carry_out(-1)

    qh = q * SSM_GROUP
    wc = jnp.stack([wre[0], wim[0], wre[1], wim[1]]).astype(BF16)
    wc = (wc.reshape(N_COMP, nt, GPT, q, SSM_GROUP, SSM_STATE).transpose(1, 3, 2, 4, 0, 5)
          .reshape(nt, q * LANES, N_COMP * SSM_STATE))
    cc = jnp.stack([cre[0], cim[0], cre[1], cim[1]]).astype(BF16)
    cc = (cc.reshape(N_COMP, nt, GPT, SSM_STATE, qh).transpose(1, 0, 2, 3, 4)
          .reshape(nt, N_COMP * GPT * SSM_STATE, qh))
    decay = jnp.stack([pr[0, ..., q], pi[0, ..., q], pr[1, ..., q], pi[1, ..., q]])
    decay = decay.reshape(N_COMP, nt, 1, GPT * SSM_STATE).transpose(1, 0, 2, 3)
    return lag_tile, wc, cc, decay


def _chunk_rows(u_ref):
    nc, _, nb, lanes = u_ref.shape
    return jnp.concatenate([u_ref[:, qq].reshape(nc * nb, lanes) for qq in range(CHUNK)], axis=-1)


SPREAD_COLS = 512


def _spread_into(dst_ref, compact_ref, expand_ref, row_inner, col_inner):
    rows, cols = dst_ref.shape
    rsh, csh = row_inner.bit_length() - 1, col_inner.bit_length() - 1
    assert 1 << rsh == row_inner and 1 << csh == col_inner
    r = lax.broadcasted_iota(jnp.int32, (rows, SPREAD_COLS), 0)
    c = lax.broadcasted_iota(jnp.int32, (rows, SPREAD_COLS), 1)
    r_gi = lax.shift_right_logical(r, rsh) & (GPT - 1)
    for c0 in range(0, cols, SPREAD_COLS):
        y = _dot(compact_ref[...], expand_ref[:, c0:c0 + SPREAD_COLS])
        c_gi = lax.shift_right_logical(c + c0, csh) & (GPT - 1)
        dst_ref[:, c0:c0 + SPREAD_COLS] = jnp.where(r_gi == c_gi, y, 0.0).astype(dst_ref.dtype)


def _s5_inject_kernel(u_ref, wc_ref, exp_ref, s_ref, w_ref):
    @pl.when(pl.program_id(1) == 0)
    def _():
        _spread_into(w_ref, wc_ref, exp_ref, SSM_GROUP, SSM_STATE)

    s = _dot(_chunk_rows(u_ref), w_ref[...])
    width = s_ref.shape[-1]
    for comp in range(N_COMP):
        s_ref[comp] = s[:, comp * width:(comp + 1) * width]


def _s5_scan_kernel(s_ref, a_ref, o_ref, *, batch, n_lat_chunks, n_ctx_chunks):
    backward = pl.program_id(1) == 1
    ar, ai = a_ref[0], a_ref[1]

    def step(i, carry):
        sr, si = carry
        k = i - n_ctx_chunks
        fwd = jnp.where(i < n_ctx_chunks, n_lat_chunks + i, k)
        bwd = jnp.where(i < n_ctx_chunks, n_lat_chunks + n_ctx_chunks - 1 - i, n_lat_chunks - 1 - k)
        rows = pl.ds(pl.multiple_of(jnp.where(backward, bwd, fwd) * batch, batch), batch)
        o_ref[0, rows, :] = sr.astype(o_ref.dtype)
        o_ref[1, rows, :] = si.astype(o_ref.dtype)
        return (ar * sr - ai * si + s_ref[0, rows, :], ar * si + ai * sr + s_ref[1, rows, :])

    z = jnp.zeros((batch, s_ref.shape[-1]), F32)
    lax.fori_loop(0, n_lat_chunks + n_ctx_chunks, step, (z, z))


def _s5_out_kernel(u_ref, sp_ref, lag_ref, cc_ref, exp_ref, d_ref, o_ref, m_ref, c_ref):
    nc, _, nb, lanes = u_ref.shape
    width = sp_ref.shape[-1]

    @pl.when(pl.program_id(1) == 0)
    def _():
        for q_in in range(CHUNK):
            for q_out in range(CHUNK):
                m_ref[q_in * lanes:(q_in + 1) * lanes, q_out * lanes:(q_out + 1) * lanes] = (
                    lag_ref[q_out - q_in + CHUNK - 1])
        _spread_into(c_ref, cc_ref, exp_ref, SSM_STATE, SSM_GROUP)

    y = _dot(_chunk_rows(u_ref), m_ref[...])
    for comp in range(N_COMP):
        y = y + _dot(sp_ref[comp], c_ref[comp * width:(comp + 1) * width, :])
    for qq in range(CHUNK):
        yq = y[:, qq * lanes:(qq + 1) * lanes].reshape(nc, nb, lanes)
        o_ref[:, qq] = jax.nn.gelu(d_ref[...] * u_ref[:, qq].astype(F32) + yq).astype(o_ref.dtype)


def _s5_call(u_ctx, u_lat, d_skip, tables, layer):
    lag_tile, wc, cc, decay = tables
    exp_w = _expand_matrix(N_COMP, SSM_STATE)
    exp_c = _expand_matrix(CHUNK, SSM_GROUP)
    op_rows = CHUNK * LANES
    full2 = lambda a: pl.BlockSpec(a.shape, lambda j, r: (0, 0))
    b, lc, w = u_ctx.shape
    l = u_lat.shape[1]
    nt = w // LANES
    n_lat, n_ctx = l // CHUNK, lc // CHUNK
    n_chunks = n_lat + n_ctx
    rows = n_chunks * b
    sw = GPT * SSM_STATE
    u_t = jnp.concatenate([u_lat, u_ctx], axis=1).transpose(1, 0, 2).reshape(n_chunks, CHUNK, b, w)
    cb = _pick_tile(n_chunks, 18)
    rb = cb * b
    u_spec = pl.BlockSpec((cb, CHUNK, b, LANES), lambda j, r: (r, 0, 0, j))
    op_spec = lambda shape: pl.BlockSpec((None, None) + shape, lambda j, r: (layer, j, 0, 0))
    s_loc = pl.pallas_call(
        _s5_inject_kernel,
        grid=(nt, n_chunks // cb),
        in_specs=[u_spec, op_spec(wc.shape[2:]), full2(exp_w)],
        out_specs=pl.BlockSpec((None, N_COMP, rb, sw), lambda j, r: (j, 0, r, 0)),
        out_shape=jax.ShapeDtypeStruct((nt, N_COMP, rows, sw), F32),
        scratch_shapes=[pltpu.VMEM((op_rows, N_COMP * sw), BF16)],
        compiler_params=_cparams(("parallel", "arbitrary")),
        name="s5_inject",
    )(u_t, wc, exp_w)
    s_prev = pl.pallas_call(
        functools.partial(_s5_scan_kernel, batch=b, n_lat_chunks=n_lat, n_ctx_chunks=n_ctx),
        grid=(nt, 2),
        in_specs=[pl.BlockSpec((None, 2, rows, sw), lambda j, d: (j, d, 0, 0)),
                  pl.BlockSpec((None, None, 2, 1, sw), lambda j, d: (layer, j, d, 0, 0))],
        out_specs=pl.BlockSpec((None, 2, rows, sw), lambda j, d: (j, d, 0, 0)),
        out_shape=jax.ShapeDtypeStruct((nt, N_COMP, rows, sw), BF16),
        compiler_params=_cparams(("parallel", "parallel")),
        name="s5_scan",
    )(s_loc, decay)
    y = pl.pallas_call(
        _s5_out_kernel,
        grid=(nt, n_chunks // cb),
        in_specs=[u_spec,
                  pl.BlockSpec((None, N_COMP, rb, sw), lambda j, r: (j, 0, r, 0)),
                  pl.BlockSpec((None, None) + lag_tile.shape[2:], lambda j, r: (layer, j, 0, 0, 0)),
                  op_spec(cc.shape[2:]), full2(exp_c),
                  pl.BlockSpec((None, 1, LANES), lambda j, r: (layer, 0, j))],
        out_specs=u_spec,
        out_shape=jax.ShapeDtypeStruct((n_chunks, CHUNK, b, w), BF16),
        scratch_shapes=[pltpu.VMEM((op_rows, op_rows), BF16), pltpu.VMEM((N_COMP * sw, op_rows), BF16)],
        compiler_params=_cparams(("parallel", "arbitrary")),
        name="s5_out",
    )(u_t, s_prev, lag_tile, cc, exp_c, d_skip)
    return y.reshape(l + lc, b, w).transpose(1, 0, 2)


def _merge_kernel(*refs, with_router):
    (h_ref, yf_ref, ys_ref, ga_ref, gb_ref, g1_ref, sc2_ref, sh2_ref, n2g_ref,
     wf_ref, wa_ref, wb_ref, wo_ref) = refs[:13]
    rest = refs[13:]
    if with_router:
        wr_ref, br_ref, h1_ref, f_ref, lg_ref = rest
    else:
        h1_ref, f_ref = rest
    ys = ys_ref[...]
    ya = _dot(yf_ref[...], wf_ref[...])
    yb = _dot(ys, wa_ref[...]) * jax.nn.sigmoid(_dot(ys, wb_ref[...]))
    m = (ga_ref[...].astype(F32) * ya + gb_ref[...].astype(F32) * yb).astype(BF16)
    h1 = h_ref[...] + g1_ref[...] * _dot(m, wo_ref[...])
    h1_ref[...] = h1
    f = _rms_mod(h1, n2g_ref[...], sc2_ref[...], sh2_ref[...])
    f_ref[...] = f.astype(f_ref.dtype)
    if with_router:
        lg_ref[...] = _dot3(f, wr_ref[...]) + br_ref[...]


def _merge_call(h, yf, ys_all, ys_row0, ga, gb, mods, g1_row, sc2_row, sh2_row, n2g, wf, wa, wb, wo, layer,
                router, tm):
    b, l, d = h.shape
    w = yf.shape[-1]
    tok = lambda width: pl.BlockSpec((None, tm, width), lambda i, m: (i, m, 0))
    vec = lambda row: pl.BlockSpec((None, 1, d), lambda i, m: (row(i), 0, 0))
    full = lambda a: pl.BlockSpec(a.shape, lambda i, m: (0,) * a.ndim)
    per_layer = lambda a: pl.BlockSpec((None,) + a.shape[1:], lambda i, m: (layer,) + (0,) * (a.ndim - 1))
    blk0 = ys_row0 // tm
    assert blk0 * tm == ys_row0
    args = [h, yf, ys_all, ga, gb, mods, mods, mods, n2g, wf, wa, wb, wo]
    in_specs = [tok(d), tok(w), pl.BlockSpec((None, tm, w), lambda i, m: (i, m + blk0, 0)), tok(d), tok(d),
                vec(g1_row), vec(sc2_row), vec(sh2_row), per_layer(n2g), per_layer(wf), per_layer(wa),
                per_layer(wb), per_layer(wo)]
    out_specs = [tok(d), tok(d)]
    f_dtype = F32 if router is not None else BF16
    out_shape = [jax.ShapeDtypeStruct((b, l, d), F32), jax.ShapeDtypeStruct((b, l, d), f_dtype)]
    if router is not None:
        wr, br = router
        args += [wr, br]
        in_specs += [full(wr), full(br)]
        out_specs.append(tok(ROUTER_PAD))
        out_shape.append(jax.ShapeDtypeStruct((b, l, ROUTER_PAD), F32))
    return pl.pallas_call(
        functools.partial(_merge_kernel, with_router=router is not None),
        grid=(b, l // tm),
        in_specs=in_specs, out_specs=out_specs, out_shape=out_shape,
        compiler_params=_cparams(("parallel", "parallel")),
        name="merge",
    )(*args)


R_E1, R_E2, R_RANK1, R_RANK2, R_W1, R_W2 = range(6)


def _route_kernel(lg_ref, rec_ref, cnt_ref, carry_ref):
    @pl.when(pl.program_id(0) == 0)
    def _():
        carry_ref[...] = jnp.zeros_like(carry_ref)

    lg = lg_ref[...]
    tm = lg.shape[0]
    lane = lax.broadcasted_iota(jnp.int32, lg.shape, 1)
    m1 = jnp.max(lg, axis=-1, keepdims=True)
    i1 = jnp.min(jnp.where(lg == m1, lane, ROUTER_PAD), axis=-1, keepdims=True)
    lg2 = jnp.where(lane == i1, -jnp.inf, lg)
    m2 = jnp.max(lg2, axis=-1, keepdims=True)
    i2 = jnp.min(jnp.where(lg2 == m2, lane, ROUTER_PAD), axis=-1, keepdims=True)
    e = jnp.exp(m2 - m1)
    w1 = 1.0 / (1.0 + e)
    w2 = e * w1

    oh1 = lane == i1
    oh2 = lane == i2
    row = lax.broadcasted_iota(jnp.int32, (tm, tm), 0)
    col = lax.broadcasted_iota(jnp.int32, (tm, tm), 1)
    below = (row > col).astype(BF16)
    p1 = _dot(below, oh1.astype(BF16))
    p2 = _dot(below, oh2.astype(BF16))
    c1 = jnp.sum(oh1.astype(F32), axis=0, keepdims=True)
    c2 = jnp.sum(oh2.astype(F32), axis=0, keepdims=True)
    base = carry_ref[...]
    r1 = jnp.sum(jnp.where(oh1, p1 + base, 0.0), axis=-1, keepdims=True)
    r2 = jnp.sum(jnp.where(oh2, p2 + (base + c1), 0.0), axis=-1, keepdims=True)
    total = base + c1 + c2
    carry_ref[...] = total
    cnt_ref[...] = total

    rec = jnp.zeros_like(lg)
    for slot, val in ((R_E1, i1.astype(F32)), (R_E2, i2.astype(F32)), (R_RANK1, r1), (R_RANK2, r2),
                      (R_W1, w1), (R_W2, w2)):
        rec = jnp.where(lane == slot, val, rec)
    rec_ref[...] = rec


def _route_call(logits, tm):
    t, n = logits.shape
    return pl.pallas_call(
        _route_kernel,
        grid=(t // tm,),
        in_specs=[pl.BlockSpec((tm, n), lambda i: (i, 0))],
        out_specs=[pl.BlockSpec((tm, n), lambda i: (i, 0)), pl.BlockSpec((1, n), lambda i: (0, 0))],
        out_shape=[jax.ShapeDtypeStruct((t, n), F32), jax.ShapeDtypeStruct((1, n), F32)],
        scratch_shapes=[pltpu.VMEM((1, n), F32)],
        compiler_params=_cparams(("arbitrary",)),
        name="route",
    )(logits)


def _dispatch_kernel(pos_ref, f_ref, xs_in_ref, xs_ref, sem):
    del xs_in_ref
    tm = f_ref.shape[0]
    base = pl.program_id(0) * (2 * tm)

    def row_copy(r, k):
        return pltpu.make_async_copy(f_ref.at[pl.ds(r, 1)], xs_ref.at[pl.ds(pos_ref[base + 2 * r + k], 1)], sem)

    def issue(r, carry):
        row_copy(r, 0).start()
        row_copy(r, 1).start()
        return carry

    lax.fori_loop(0, tm, issue, 0, unroll=8)
    for _ in range(2):
        pltpu.make_async_copy(f_ref, xs_ref.at[pl.ds(0, tm)], sem).wait()


def _dispatch_call(pos, f, n_rows, tm):
    t, d = f.shape
    return pl.pallas_call(
        _dispatch_kernel,
        grid_spec=pltpu.PrefetchScalarGridSpec(
            num_scalar_prefetch=1,
            grid=(t // tm,),
            in_specs=[pl.BlockSpec((tm, d), lambda i, pos: (i, 0)),
                      pl.BlockSpec(memory_space=pl.ANY)],
            out_specs=pl.BlockSpec(memory_space=pl.ANY),
            scratch_shapes=[pltpu.SemaphoreType.DMA]),
        out_shape=jax.ShapeDtypeStruct((n_rows, d), f.dtype),
        input_output_aliases={2: 0},
        compiler_params=pltpu.CompilerParams(dimension_semantics=("arbitrary",), vmem_limit_bytes=VMEM_LIMIT,
                                             disable_bounds_checks=True),
        name="dispatch",
    )(pos, f, jnp.zeros((n_rows, d), f.dtype))


def _gffn_kernel(te_ref, nu_ref, x_ref, wg_ref, wu_ref, wd_ref, o_ref, xb_ref):
    del te_ref
    j = pl.program_id(1)

    @pl.when(pl.program_id(0) < nu_ref[0])
    def _():
        @pl.when(j == 0)
        def _():
            xb_ref[...] = x_ref[...].astype(BF16)
            o_ref[...] = jnp.zeros_like(o_ref)

        x = xb_ref[...]
        hid = jax.nn.silu(_dot(x, wg_ref[...].astype(BF16))) * _dot(x, wu_ref[...].astype(BF16))
        o_ref[...] += _dot(hid.astype(BF16), wd_ref[...].astype(BF16))


def _gffn_call(tile_expert, n_used, xs, wg, wu, wd, idx, tm, tf):
    n_rows, d = xs.shape
    f = wg.shape[-1]
    nj = f // tf

    def row_map(i, j, te, nu):
        return jnp.minimum(i, nu[0] - 1), 0

    def jj(i, j, nu):
        return jnp.where(i < nu[0], j, nj - 1)

    return pl.pallas_call(
        _gffn_kernel,
        grid_spec=pltpu.PrefetchScalarGridSpec(
            num_scalar_prefetch=2,
            grid=(n_rows // tm, nj),
            in_specs=[pl.BlockSpec((tm, d), row_map),
                      pl.BlockSpec((None, None, d, tf), lambda i, j, te, nu: (idx, te[i], 0, jj(i, j, nu))),
                      pl.BlockSpec((None, None, d, tf), lambda i, j, te, nu: (idx, te[i], 0, jj(i, j, nu))),
                      pl.BlockSpec((None, None, tf, d), lambda i, j, te, nu: (idx, te[i], jj(i, j, nu), 0))],
            out_specs=pl.BlockSpec((tm, d), row_map),
            scratch_shapes=[pltpu.VMEM((tm, d), BF16)]),
        out_shape=jax.ShapeDtypeStruct((n_rows, d), F32),
        input_output_aliases={2: 0},
        compiler_params=_cparams(("arbitrary", "arbitrary")),
        name="gffn",
    )(tile_expert, n_used, xs, wg, wu, wd)


def _combine_kernel(pos_ref, ys_ref, h_ref, rec_ref, g2_ref, fg_ref, o_ref, ybuf_ref, sem, *, final_norm):
    tm = h_ref.shape[0]
    base = pl.program_id(0) * (2 * tm)

    def row_copy(r, k):
        return pltpu.make_async_copy(ys_ref.at[pl.ds(pos_ref[base + 2 * r + k], 1)],
                                     ybuf_ref.at[k, pl.ds(r, 1)], sem)

    def issue(r, carry):
        row_copy(r, 0).start()
        row_copy(r, 1).start()
        return carry

    lax.fori_loop(0, tm, issue, 0, unroll=8)
    for k in range(2):
        pltpu.make_async_copy(ys_ref.at[pl.ds(0, tm)], ybuf_ref.at[k], sem).wait()
    rec = rec_ref[...]
    y = rec[:, R_W1:R_W1 + 1] * ybuf_ref[0] + rec[:, R_W2:R_W2 + 1] * ybuf_ref[1]
    out = h_ref[...] + g2_ref[...] * y
    if final_norm:
        out = out * lax.rsqrt(jnp.mean(out * out, axis=-1, keepdims=True) + EPS) * fg_ref[...]
    o_ref[...] = out


def _combine_call(pos, ys, h, rec, mods, g2_row, tiles_per_batch, final_g, tm):
    t, d = h.shape
    fg = (final_g if final_g is not None else jnp.ones((d,), F32)).reshape(1, d)
    return pl.pallas_call(
        functools.partial(_combine_kernel, final_norm=final_g is not None),
        grid_spec=pltpu.PrefetchScalarGridSpec(
            num_scalar_prefetch=1,
            grid=(t // tm,),
            in_specs=[pl.BlockSpec(memory_space=pl.ANY),
                      pl.BlockSpec((tm, d), lambda i, pos: (i, 0)),
                      pl.BlockSpec((tm, ROUTER_PAD), lambda i, pos: (i, 0)),
                      pl.BlockSpec((None, 1, d), lambda i, pos: (g2_row(i // tiles_per_batch), 0, 0)),
                      pl.BlockSpec((1, d), lambda i, pos: (0, 0))],
            out_specs=pl.BlockSpec((tm, d), lambda i, pos: (i, 0)),
            scratch_shapes=[pltpu.VMEM((2, tm, d), F32), pltpu.SemaphoreType.DMA]),
        out_shape=jax.ShapeDtypeStruct((t, d), F32),
        compiler_params=pltpu.CompilerParams(dimension_semantics=("arbitrary",), vmem_limit_bytes=VMEM_LIMIT,
                                             disable_bounds_checks=True),
        name="combine",
    )(pos, ys, h, rec, mods, fg)


def _moe_plan(rec, counts, tm, n_tiles):
    cnt = counts[0, :N_EXPERTS].astype(jnp.int32)
    nt = (cnt + (tm - 1)) // tm
    cum = jnp.cumsum(nt)
    start = (cum - nt) * tm
    e = rec[:, R_E1:R_E2 + 1].astype(jnp.int32)
    rank = rec[:, R_RANK1:R_RANK2 + 1].astype(jnp.int32)
    ex = lax.broadcasted_iota(jnp.int32, e.shape + (N_EXPERTS,), 2)
    pos = rank + jnp.sum(jnp.where(e[..., None] == ex, start, 0), axis=-1)
    tile = jnp.arange(n_tiles, dtype=jnp.int32)
    te = jnp.sum((tile[:, None] >= cum[None, :]).astype(jnp.int32), axis=1)
    n_used = cum[-1:]
    last_e = jnp.sum((n_used - 1 >= cum).astype(jnp.int32))
    return pos.reshape(-1), jnp.minimum(te, last_e), n_used


def _ffn_kernel(x_ref, wg_ref, wu_ref, wd_ref, h_ref, g2_ref, fg_ref, o_ref, acc_ref, *, final_norm):
    j = pl.program_id(1)

    @pl.when(j == 0)
    def _():
        acc_ref[...] = jnp.zeros_like(acc_ref)

    x = x_ref[...]
    hid = jax.nn.silu(_dot(x, wg_ref[...].astype(BF16))) * _dot(x, wu_ref[...].astype(BF16))
    acc_ref[...] += _dot(hid.astype(BF16), wd_ref[...].astype(BF16))

    @pl.when(j == pl.num_programs(1) - 1)
    def _():
        out = h_ref[...] + g2_ref[...] * acc_ref[...]
        if final_norm:
            out = out * lax.rsqrt(jnp.mean(out * out, axis=-1, keepdims=True) + EPS) * fg_ref[...]
        o_ref[...] = out


def _ffn_call(x, wg, wu, wd, idx, h, mods, g2_row, tiles_per_batch, final_g, tm, tf):
    t, d = x.shape
    f = wg.shape[-1]
    tok = lambda width: pl.BlockSpec((tm, width), lambda m, j: (m, 0))
    fg = (final_g if final_g is not None else jnp.ones((d,), F32)).reshape(1, d)
    return pl.pallas_call(
        functools.partial(_ffn_kernel, final_norm=final_g is not None),
        grid=(t // tm, f // tf),
        in_specs=[tok(d),
                  pl.BlockSpec((None, d, tf), lambda m, j: (idx, 0, j)),
                  pl.BlockSpec((None, d, tf), lambda m, j: (idx, 0, j)),
                  pl.BlockSpec((None, tf, d), lambda m, j: (idx, j, 0)),
                  tok(d),
                  pl.BlockSpec((None, 1, d), lambda m, j: (g2_row(m // tiles_per_batch), 0, 0)),
                  pl.BlockSpec((1, d), lambda m, j: (0, 0))],
        out_specs=tok(d),
        out_shape=jax.ShapeDtypeStruct((t, d), F32),
        scratch_shapes=[pltpu.VMEM((tm, d), F32)],
        compiler_params=_cparams(("parallel", "arbitrary")),
        name="ffn",
    )(x, wg, wu, wd, h, mods, fg)


def _pos_table(n, dim):
    t = np.arange(n)
    r = (t // GRID_W).astype(np.float32)
    col = (t % GRID_W).astype(np.float32)
    quarter = dim // 4
    omega = (1.0 / (POS_BASE ** (np.arange(quarter, dtype=np.float32) / quarter))).astype(np.float32)
    ar = r[:, None] * omega
    ac = col[:, None] * omega
    return np.concatenate([np.sin(ar), np.cos(ar), np.sin(ac), np.cos(ac)], axis=-1).astype(np.float32)


def _pick_tile(n, pref):
    return pref if n % pref == 0 else n


def kernel(x, c, ctx, c_ctx, w_mod, b_mod, norm1_g, norm2_g, w_in, w_four, ssm_a_re, ssm_a_im, ssm_log_dt, ssm_b_re, ssm_b_im, ssm_c_re, ssm_c_im, ssm_d, w_glu_a, w_glu_b, w_out, ffn_w_gate, ffn_w_up, ffn_w_down, moe_w_router, moe_b_router, moe_w_gate, moe_w_up, moe_w_down, final_g):
    b, l, d = x.shape
    lc = ctx.shape[1]
    depth = w_mod.shape[0]
    wf_cols = w_four.shape[1]
    ws_cols = ssm_d.shape[1]
    off_ga = wf_cols + ws_cols
    tm_lat = _pick_tile(l, 512)
    tm_ctx = _pick_tile(lc, 256)

    n_cond = b + 8
    cond = jnp.concatenate([c, jnp.broadcast_to(c_ctx[None], (n_cond - b, d))], axis=0)
    pos = jnp.asarray(_pos_table(l, d))

    mods = _mods_call(cond, w_mod, b_mod)
    w_in_b, wf, wa, wb, wo = (w.astype(BF16) for w in (w_in, w_four, w_glu_a, w_glu_b, w_out))
    n1g, n2g = norm1_g.reshape(depth, 1, d), norm2_g.reshape(depth, 1, d)
    d_skip = ssm_d.reshape(depth, 1, ws_cols)
    tables = jax.vmap(_s5_tables)(ssm_a_re, ssm_a_im, ssm_log_dt, ssm_b_re, ssm_b_im, ssm_c_re, ssm_c_im)

    def mod_rows(layer, which):
        return (lambda i: (layer * n_cond + i) * N_MODS + which,
                lambda i: (layer * n_cond + b) * N_MODS + which)

    h = x
    z = ctx
    for layer in range(depth):
        need_ctx = layer < depth - 1
        sh1, sc1, g1, sh2, sc2, g2 = (mod_rows(layer, k) for k in range(N_MODS))
        widths = (wf_cols, ws_cols, d, d)
        acts = (False, False, True, True)

        outs = _inproj_call(h, pos if layer == 0 else None, mods, sc1[0], sh1[0], n1g, w_in_b, layer, 0,
                            widths, acts, tm_lat)
        if layer == 0:
            h, outs = outs[0], outs[1:]
        pf, ps, ga, gb = outs
        if need_ctx:
            cpf, cps, cga, cgb = _inproj_call(z, None, mods, sc1[1], sh1[1], n1g, w_in_b, layer, 0,
                                              widths, acts, tm_ctx)
        else:
            (cps,) = _inproj_call(z, None, mods, sc1[1], sh1[1], n1g, w_in_b, layer, wf_cols,
                                  (ws_cols,), (False,), tm_ctx)

        ys_all = _s5_call(cps, ps, d_skip, tables, layer)
        yf = _fourier_call(pf, tm_lat)

        moe = layer % 2 == 1
        idx = layer // 2
        router = None
        if moe:
            wr = jnp.pad(moe_w_router[idx], ((0, 0), (0, ROUTER_PAD - N_EXPERTS)))
            br = jnp.pad(moe_b_router[idx][None], ((0, 0), (0, ROUTER_PAD - N_EXPERTS)), constant_values=NEG_BIG)
            router = (wr, br)
        res = _merge_call(h, yf, ys_all, 0, ga, gb, mods, g1[0], sc2[0], sh2[0], n2g, wf, wa, wb, wo, layer,
                          router, tm_lat)
        last = layer == depth - 1
        fin = final_g if last else None
        t = b * l
        if moe:
            h1, f, logits = res
            tm_moe = min(1024, max(128, t // 8))
            n_tiles = 2 * t // tm_moe + N_EXPERTS
            rec, counts = _route_call(logits.reshape(t, ROUTER_PAD), _pick_tile(t, 512))
            pos, tile_expert, n_used = _moe_plan(rec, counts, tm_moe, n_tiles)
            xs = _dispatch_call(pos, f.reshape(t, d), n_tiles * tm_moe, _pick_tile(t, 512))
            ys = _gffn_call(tile_expert, n_used, xs, moe_w_gate, moe_w_up, moe_w_down, idx,
                            tm_moe, _pick_tile(moe_w_gate.shape[-1], FFN_TILE))
            tm_c = _pick_tile(l, 512)
            h = _combine_call(pos, ys, h1.reshape(t, d), rec, mods, g2[0], l // tm_c, fin, tm_c).reshape(b, l, d)
        else:
            h1, f = res
            ffn_w = (ffn_w_gate, ffn_w_up, ffn_w_down, idx)
            tf = _pick_tile(ffn_w_gate.shape[-1], 256)
            tm_f = _pick_tile(l, 1024)
            h = _ffn_call(f.reshape(t, d), *ffn_w, h1.reshape(t, d), mods, g2[0], l // tm_f, fin,
                          tm_f, tf).reshape(b, l, d)

        if need_ctx:
            if moe:
                raise NotImplementedError("context tokens through an expert layer")
            cyf = _fourier_call(cpf, tm_ctx)
            z1, cf = _merge_call(z, cyf, ys_all, l, cga, cgb, mods, g1[1], sc2[1], sh2[1], n2g, wf, wa, wb, wo,
                                 layer, None, tm_ctx)
            z = _ffn_call(cf.reshape(b * lc, d), *ffn_w, z1.reshape(b * lc, d), mods, g2[1], 1, None,
                          _pick_tile(b * lc, 1024), tf).reshape(b, lc, d)
    return h
```

```python
import functools
import math

import numpy as np
import jax
import jax.numpy as jnp
from jax import lax
from jax.experimental import pallas as pl
from jax.experimental.pallas import tpu as pltpu

F32 = jnp.float32
BF16 = jnp.bfloat16

GRID_W = 64
N_FOURIER_GROUPS = 4
FOURIER_GROUP = 128
SSM_GROUP = 16
SSM_STATE = 64
N_EXPERTS = 8
EPS = 1e-6
POS_BASE = 10000.0

CHUNK = 16
LANES = 128
GPT = LANES // SSM_GROUP
N_COMP = 4
FFN_TILE = 512
ROUTER_PAD = LANES
NEG_BIG = -1e30
VMEM_LIMIT = 56 * 1024 * 1024


def _cparams(sem):
    return pltpu.CompilerParams(dimension_semantics=sem, vmem_limit_bytes=VMEM_LIMIT)


def _dot(a, b):
    return jnp.dot(a, b, preferred_element_type=F32)


def _split_bf16(a):
    hi = a.astype(BF16)
    lo = (a - hi.astype(F32)).astype(BF16)
    return hi, lo


def _dot3(a, b):
    ah, al = _split_bf16(a)
    bh, bl = _split_bf16(b)
    return _dot(ah, bh) + (_dot(ah, bl) + _dot(al, bh))


def _rms_mod(h, g, sc, sh):
    y = h * lax.rsqrt(jnp.mean(h * h, axis=-1, keepdims=True) + EPS)
    return (y * g) * (1.0 + sc) + sh


def _mods_kernel(c_ref, w_ref, b_ref, o_ref):
    s = jax.nn.silu(c_ref[...])
    o_ref[...] = _dot3(s, w_ref[...]) + b_ref[...]


N_MODS = 6
MOD_SHIFT1, MOD_SCALE1, MOD_GATE1, MOD_SHIFT2, MOD_SCALE2, MOD_GATE2 = range(N_MODS)


def _mods_call(cond, w_mod, b_mod):
    r, d = cond.shape
    depth, _, n = w_mod.shape
    tn = 1024
    out = pl.pallas_call(
        _mods_kernel,
        grid=(depth, n // tn),
        in_specs=[pl.BlockSpec((r, d), lambda i, j: (0, 0)),
                  pl.BlockSpec((None, d, tn), lambda i, j: (i, 0, j)),
                  pl.BlockSpec((None, 1, tn), lambda i, j: (i, 0, j))],
        out_specs=pl.BlockSpec((None, r, tn), lambda i, j: (i, 0, j)),
        out_shape=jax.ShapeDtypeStruct((depth, r, n), F32),
        compiler_params=_cparams(("arbitrary", "arbitrary")),
        name="mods",
    )(cond, w_mod, b_mod.reshape(depth, 1, n))
    return out.reshape(depth * r * N_MODS, 1, d)


def _inproj_kernel(*refs, acts, has_pos):
    it = iter(refs)
    x_ref = next(it)
    pos_ref = next(it) if has_pos else None
    sc_ref, sh_ref, g_ref, w_ref = next(it), next(it), next(it), next(it)
    h_out = next(it) if has_pos else None
    outs = list(it)
    h = x_ref[...]
    if has_pos:
        h = h + pos_ref[...]
        h_out[...] = h
    a = _rms_mod(h, g_ref[...], sc_ref[...], sh_ref[...]).astype(BF16)
    col = 0
    for o, act in zip(outs, acts):
        width = o.shape[-1]
        p = _dot(a, w_ref[:, col:col + width])
        if act:
            p = jax.nn.sigmoid(p)
        o[...] = p.astype(o.dtype)
        col += width


def _inproj_call(x, pos, mods, sc_row, sh_row, g, w, layer, col0, widths, acts, tm):
    b, l, d = x.shape
    has_pos = pos is not None
    n_cols = sum(widths)
    col_blk = col0 // n_cols
    assert col_blk * n_cols == col0
    tok = pl.BlockSpec((None, tm, d), lambda i, m: (i, m, 0))
    vec = lambda row: pl.BlockSpec((None, 1, d), lambda i, m: (row(i), 0, 0))
    in_specs = [tok]
    args = [x]
    if has_pos:
        in_specs.append(pl.BlockSpec((tm, d), lambda i, m: (m, 0)))
        args.append(pos)
    in_specs += [vec(sc_row), vec(sh_row), pl.BlockSpec((None, 1, d), lambda i, m: (layer, 0, 0)),
                 pl.BlockSpec((None, d, n_cols), lambda i, m: (layer, 0, col_blk))]
    args += [mods, mods, g, w]
    out_specs, out_shape = [], []
    if has_pos:
        out_specs.append(tok)
        out_shape.append(jax.ShapeDtypeStruct((b, l, d), F32))
    for width in widths:
        out_specs.append(pl.BlockSpec((None, tm, width), lambda i, m: (i, m, 0)))
        out_shape.append(jax.ShapeDtypeStruct((b, l, width), BF16))
    return pl.pallas_call(
        functools.partial(_inproj_kernel, acts=tuple(acts), has_pos=has_pos),
        grid=(b, l // tm),
        in_specs=in_specs, out_specs=out_specs, out_shape=out_shape,
        compiler_params=_cparams(("parallel", "parallel")),
        name="inproj",
    )(*args)


DFT_PAD_ROWS = 16


def _dft_tables(l):
    k = np.arange(l // 2 + DFT_PAD_ROWS, dtype=np.int64)
    n = np.arange(l, dtype=np.int64)
    ang = 2.0 * np.pi * ((k[:, None] * n[None, :]) % l).astype(np.float64) / l
    dl = np.concatenate([np.cos(ang), np.sin(ang)], axis=1).astype(np.float32)
    c = np.arange(FOURIER_GROUP, dtype=np.int64)
    angc = 2.0 * np.pi * ((c[:, None] * c[None, :]) % FOURIER_GROUP).astype(np.float64) / FOURIER_GROUP
    return dl, np.cos(angc).astype(np.float32), np.sin(angc).astype(np.float32)


def _fourier_kernel(u_ref, cc_ref, cs_ref, dl_ref, nxt_ref, o_ref, v_ref, *, l, scale):
    m = pl.program_id(1)
    tk = dl_ref.shape[0]

    @pl.when(m == 0)
    def _():
        for j in range(N_FOURIER_GROUPS):
            cols = slice(j * FOURIER_GROUP, (j + 1) * FOURIER_GROUP)
            uj = u_ref[:, cols]
            v_ref[0:l, cols] = _dot(uj, cc_ref[...]).astype(BF16)
            v_ref[l:2 * l, cols] = _dot(uj, cs_ref[...]).astype(BF16)

    p = _dot(dl_ref[:, :l], v_ref[0:l, :])
    q = _dot(dl_ref[:, l:], v_ref[l:2 * l, :])
    o_ref[pl.ds(pl.multiple_of(m * tk, tk), tk), :] = ((p - q) * scale).astype(o_ref.dtype)
    row = lax.broadcasted_iota(jnp.int32, (tk, tk), 0)
    col = lax.broadcasted_iota(jnp.int32, (tk, tk), 1)
    flip = (row + col == tk).astype(BF16)
    mirrored = _dot(flip, ((p + q) * scale).astype(BF16))
    nxt = _dot(nxt_ref[:, :l], v_ref[0:l, :]) + _dot(nxt_ref[:, l:], v_ref[l:2 * l, :])
    first = lax.broadcasted_iota(jnp.int32, mirrored.shape, 0) == 0
    upper = jnp.where(first, nxt[0:1] * scale, mirrored)
    o_ref[pl.ds(pl.multiple_of(l - (m + 1) * tk, tk), tk), :] = upper.astype(o_ref.dtype)


def _fourier_call(u, tm):
    b, l, w = u.shape
    dl, cc, cs = _dft_tables(l)
    dl = jnp.asarray(dl).astype(BF16)
    cc = jnp.asarray(cc).astype(BF16)
    cs = jnp.asarray(cs).astype(BF16)
    scale = 1.0 / math.sqrt(l * FOURIER_GROUP)
    tk = min(tm, l // 2)
    nxt_blocks = tk // DFT_PAD_ROWS
    return pl.pallas_call(
        functools.partial(_fourier_kernel, l=l, scale=scale),
        grid=(b, l // 2 // tk),
        in_specs=[pl.BlockSpec((None, l, w), lambda i, m: (i, 0, 0)),
                  pl.BlockSpec((FOURIER_GROUP, FOURIER_GROUP), lambda i, m: (0, 0)),
                  pl.BlockSpec((FOURIER_GROUP, FOURIER_GROUP), lambda i, m: (0, 0)),
                  pl.BlockSpec((tk, 2 * l), lambda i, m: (m, 0)),
                  pl.BlockSpec((DFT_PAD_ROWS, 2 * l), lambda i, m: ((m + 1) * nxt_blocks, 0))],
        out_specs=pl.BlockSpec((None, l, w), lambda i, m: (i, 0, 0)),
        out_shape=jax.ShapeDtypeStruct((b, l, w), BF16),
        scratch_shapes=[pltpu.VMEM((2 * l, w), BF16)],
        compiler_params=_cparams(("parallel", "arbitrary")),
        name="fourier",
    )(u, cc, cs, dl, dl)


def _expand_matrix(n_outer, inner):
    r = np.arange(n_outer * inner)[:, None]
    c = np.arange(n_outer * GPT * inner)[None, :]
    same = (r // inner == c // (GPT * inner)) & (r % inner == c % inner)
    return jnp.asarray(same.astype(np.float32)).astype(BF16)


def _spread_groups(x, expand, row_inner, col_inner):
    y = jnp.einsum('jrc,cn->jrn', x, expand, preferred_element_type=BF16)
    r = lax.broadcasted_iota(jnp.int32, y.shape, 1)
    c = lax.broadcasted_iota(jnp.int32, y.shape, 2)
    return jnp.where((r // row_inner) % GPT == (c // col_inner) % GPT, y, jnp.zeros_like(y))


def _s5_tables(a_re, a_im, log_dt, b_re, b_im, c_re, c_im):
    hp = lax.Precision.HIGHEST
    q = CHUNK
    g = a_re.shape[1]
    dt = jnp.exp(log_dt)[..., None]
    lr, li = a_re * dt, a_im * dt
    em1_r = jnp.expm1(lr) * jnp.cos(li) - 2.0 * jnp.sin(0.5 * li) ** 2
    em1_i = jnp.exp(lr) * jnp.sin(li)
    den = a_re * a_re + a_im * a_im
    fr = (em1_r * a_re + em1_i * a_im) / den
    fi = (em1_i * a_re - em1_r * a_im) / den
    bbr = fr[..., None] * b_re - fi[..., None] * b_im
    bbi = fr[..., None] * b_im + fi[..., None] * b_re
    k = jnp.arange(q + 1, dtype=F32)
    mag = jnp.exp(lr[..., None] * k)
    pr = mag * jnp.cos(li[..., None] * k)
    pi = mag * jnp.sin(li[..., None] * k)

    prk = jnp.moveaxis(pr, -1, 2)[..., None]
    pik = jnp.moveaxis(pi, -1, 2)[..., None]
    wr = prk * bbr[:, :, None] - pik * bbi[:, :, None]
    wi = prk * bbi[:, :, None] + pik * bbr[:, :, None]
    kern = (jnp.einsum('dgop,dgtph->dgtoh', c_re, wr[:, :, :q], precision=hp)
            - jnp.einsum('dgop,dgtph->dgtoh', c_im, wi[:, :, :q], precision=hp))
    qi = jnp.arange(q)
    nt = g // GPT
    kf = kern[0].transpose(0, 1, 3, 2)
    kb = kern[1].transpose(0, 1, 3, 2)
    lag = jnp.concatenate([kb[:, :0:-1], kf[:, :1] + kb[:, :1], kf[:, 1:]], axis=1).astype(BF16)
    lag = (lag.reshape(nt, GPT, 2 * q - 1, SSM_GROUP, SSM_GROUP).transpose(0, 2, 1, 3, 4)
           .reshape(nt * (2 * q - 1), LANES, SSM_GROUP))
    lag_tile = _spread_groups(lag, _expand_matrix(1, SSM_GROUP), SSM_GROUP, SSM_GROUP)
    lag_tile = lag_tile.reshape(nt, 2 * q - 1, LANES, LANES)

    def inject(w):
        wf = w[0][:, q - 1 - qi]
        wb = w[1][:, qi]
        both = jnp.stack([wf, wb])
        return both.transpose(0, 1, 2, 4, 3).reshape(2, g, q * SSM_GROUP, SSM_STATE)

    wre, wim = inject(wr), inject(wi)

    def carry_out(sign):
        outs = []
        for d, idx in ((0, qi + 1), (1, q - qi)):
            ppr = pr[d][:, :, idx]
            ppi = pi[d][:, :, idx]
            cr = c_re[d].transpose(0, 2, 1)[:, :, None, :]
            ci = c_im[d].transpose(0, 2, 1)[:, :, None, :]
            if sign > 0:
                val = cr * ppr[..., None] - ci * ppi[..., None]
            else:
                val = -(cr * ppi[..., None] + ci * ppr[..., None])
            outs.append(val.reshape(g, SSM_STATE, q * SSM_GROUP))
        return jnp.stack(outs)

    cre, cim = carry_out(+1), carry_out(-1)

    qh = q * SSM_GROUP
    wc = jnp.stack([wre[0], wim[0], wre[1], wim[1]]).astype(BF16)
    wc = (wc.reshape(N_COMP, nt, GPT, q, SSM_GROUP, SSM_STATE).transpose(1, 3, 2, 4, 0, 5)
          .reshape(nt, q * LANES, N_COMP * SSM_STATE))
    cc = jnp.stack([cre[0], cim[0], cre[1], cim[1]]).astype(BF16)
    cc = (cc.reshape(N_COMP, nt, GPT, SSM_STATE, qh).transpose(1, 0, 2, 3, 4)
          .reshape(nt, N_COMP * GPT * SSM_STATE, qh))
    decay = jnp.stack([pr[0, ..., q], pi[0, ..., q], pr[1, ..., q], pi[1, ..., q]])
    decay = decay.reshape(N_COMP, nt, 1, GPT * SSM_STATE).transpose(1, 0, 2, 3)
    return lag_tile, wc, cc, decay


def _chunk_rows(u_ref):
    nc, _, nb, lanes = u_ref.shape
    return jnp.concatenate([u_ref[:, qq].reshape(nc * nb, lanes) for qq in range(CHUNK)], axis=-1)


SPREAD_COLS = 512


def _spread_into(dst_ref, compact_ref, expand_ref, row_inner, col_inner):
    rows, cols = dst_ref.shape
    rsh, csh = row_inner.bit_length() - 1, col_inner.bit_length() - 1
    assert 1 << rsh == row_inner and 1 << csh == col_inner
    r = lax.broadcasted_iota(jnp.int32, (rows, SPREAD_COLS), 0)
    c = lax.broadcasted_iota(jnp.int32, (rows, SPREAD_COLS), 1)
    r_gi = lax.shift_right_logical(r, rsh) & (GPT - 1)
    for c0 in range(0, cols, SPREAD_COLS):
        y = _dot(compact_ref[...], expand_ref[:, c0:c0 + SPREAD_COLS])
        c_gi = lax.shift_right_logical(c + c0, csh) & (GPT - 1)
        dst_ref[:, c0:c0 + SPREAD_COLS] = jnp.where(r_gi == c_gi, y, 0.0).astype(dst_ref.dtype)


def _s5_inject_kernel(u_ref, wc_ref, exp_ref, s_ref, w_ref):
    @pl.when(pl.program_id(1) == 0)
    def _():
        _spread_into(w_ref, wc_ref, exp_ref, SSM_GROUP, SSM_STATE)

    s = _dot(_chunk_rows(u_ref), w_ref[...])
    width = s_ref.shape[-1]
    for comp in range(N_COMP):
        s_ref[comp] = s[:, comp * width:(comp + 1) * width]


def _s5_scan_kernel(s_ref, a_ref, o_ref, *, batch, n_lat_chunks, n_ctx_chunks):
    backward = pl.program_id(1) == 1
    ar, ai = a_ref[0], a_ref[1]

    def step(i, carry):
        sr, si = carry
        k = i - n_ctx_chunks
        fwd = jnp.where(i < n_ctx_chunks, n_lat_chunks + i, k)
        bwd = jnp.where(i < n_ctx_chunks, n_lat_chunks + n_ctx_chunks - 1 - i, n_lat_chunks - 1 - k)
        rows = pl.ds(pl.multiple_of(jnp.where(backward, bwd, fwd) * batch, batch), batch)
        o_ref[0, rows, :] = sr.astype(o_ref.dtype)
        o_ref[1, rows, :] = si.astype(o_ref.dtype)
        return (ar * sr - ai * si + s_ref[0, rows, :], ar * si + ai * sr + s_ref[1, rows, :])

    z = jnp.zeros((batch, s_ref.shape[-1]), F32)
    lax.fori_loop(0, n_lat_chunks + n_ctx_chunks, step, (z, z))


def _s5_out_kernel(u_ref, sp_ref, lag_ref, cc_ref, exp_ref, d_ref, o_ref, m_ref, c_ref):
    nc, _, nb, lanes = u_ref.shape
    width = sp_ref.shape[-1]

    @pl.when(pl.program_id(1) == 0)
    def _():
        for q_in in range(CHUNK):
            for q_out in range(CHUNK):
                m_ref[q_in * lanes:(q_in + 1) * lanes, q_out * lanes:(q_out + 1) * lanes] = (
                    lag_ref[q_out - q_in + CHUNK - 1])
        _spread_into(c_ref, cc_ref, exp_ref, SSM_STATE, SSM_GROUP)

    y = _dot(_chunk_rows(u_ref), m_ref[...])
    for comp in range(N_COMP):
        y = y + _dot(sp_ref[comp], c_ref[comp * width:(comp + 1) * width, :])
    for qq in range(CHUNK):
        yq = y[:, qq * lanes:(qq + 1) * lanes].reshape(nc, nb, lanes)
        o_ref[:, qq] = jax.nn.gelu(d_ref[...] * u_ref[:, qq].astype(F32) + yq).astype(o_ref.dtype)


def _s5_call(u_ctx, u_lat, d_skip, tables, layer):
    lag_tile, wc, cc, decay = tables
    exp_w = _expand_matrix(N_COMP, SSM_STATE)
    exp_c = _expand_matrix(CHUNK, SSM_GROUP)
    op_rows = CHUNK * LANES
    full2 = lambda a: pl.BlockSpec(a.shape, lambda j, r: (0, 0))
    b, lc, w = u_ctx.shape
    l = u_lat.shape[1]
    nt = w // LANES
    n_lat, n_ctx = l // CHUNK, lc // CHUNK
    n_chunks = n_lat + n_ctx
    rows = n_chunks * b
    sw = GPT * SSM_STATE
    u_t = jnp.concatenate([u_lat, u_ctx], axis=1).transpose(1, 0, 2).reshape(n_chunks, CHUNK, b, w)
    cb = _pick_tile(n_chunks, 18)
    rb = cb * b
    u_spec = pl.BlockSpec((cb, CHUNK, b, LANES), lambda j, r: (r, 0, 0, j))
    op_spec = lambda shape: pl.BlockSpec((None, None) + shape, lambda j, r: (layer, j, 0, 0))
    s_loc = pl.pallas_call(
        _s5_inject_kernel,
        grid=(nt, n_chunks // cb),
        in_specs=[u_spec, op_spec(wc.shape[2:]), full2(exp_w)],
        out_specs=pl.BlockSpec((None, N_COMP, rb, sw), lambda j, r: (j, 0, r, 0)),
        out_shape=jax.ShapeDtypeStruct((nt, N_COMP, rows, sw), F32),
        scratch_shapes=[pltpu.VMEM((op_rows, N_COMP * sw), BF16)],
        compiler_params=_cparams(("parallel", "arbitrary")),
        name="s5_inject",
    )(u_t, wc, exp_w)
    s_prev = pl.pallas_call(
        functools.partial(_s5_scan_kernel, batch=b, n_lat_chunks=n_lat, n_ctx_chunks=n_ctx),
        grid=(nt, 2),
        in_specs=[pl.BlockSpec((None, 2, rows, sw), lambda j, d: (j, d, 0, 0)),
                  pl.BlockSpec((None, None, 2, 1, sw), lambda j, d: (layer, j, d, 0, 0))],
        out_specs=pl.BlockSpec((None, 2, rows, sw), lambda j, d: (j, d, 0, 0)),
        out_shape=jax.ShapeDtypeStruct((nt, N_COMP, rows, sw), BF16),
        compiler_params=_cparams(("parallel", "parallel")),
        name="s5_scan",
    )(s_loc, decay)
    y = pl.pallas_call(
        _s5_out_kernel,
        grid=(nt, n_chunks // cb),
        in_specs=[u_spec,
                  pl.BlockSpec((None, N_COMP, rb, sw), lambda j, r: (j, 0, r, 0)),
                  pl.BlockSpec((None, None) + lag_tile.shape[2:], lambda j, r: (layer, j, 0, 0, 0)),
                  op_spec(cc.shape[2:]), full2(exp_c),
                  pl.BlockSpec((None, 1, LANES), lambda j, r: (layer, 0, j))],
        out_specs=u_spec,
        out_shape=jax.ShapeDtypeStruct((n_chunks, CHUNK, b, w), BF16),
        scratch_shapes=[pltpu.VMEM((op_rows, op_rows), BF16), pltpu.VMEM((N_COMP * sw, op_rows), BF16)],
        compiler_params=_cparams(("parallel", "arbitrary")),
        name="s5_out",
    )(u_t, s_prev, lag_tile, cc, exp_c, d_skip)
    return y.reshape(l + lc, b, w).transpose(1, 0, 2)


def _merge_kernel(*refs, with_router):
    (h_ref, yf_ref, ys_ref, ga_ref, gb_ref, g1_ref, sc2_ref, sh2_ref, n2g_ref,
     wf_ref, wa_ref, wb_ref, wo_ref) = refs[:13]
    rest = refs[13:]
    if with_router:
        wr_ref, br_ref, h1_ref, f_ref, lg_ref = rest
    else:
        h1_ref, f_ref = rest
    ys = ys_ref[...]
    ya = _dot(yf_ref[...], wf_ref[...])
    yb = _dot(ys, wa_ref[...]) * jax.nn.sigmoid(_dot(ys, wb_ref[...]))
    m = (ga_ref[...].astype(F32) * ya + gb_ref[...].astype(F32) * yb).astype(BF16)
    h1 = h_ref[...] + g1_ref[...] * _dot(m, wo_ref[...])
    h1_ref[...] = h1
    f = _rms_mod(h1, n2g_ref[...], sc2_ref[...], sh2_ref[...])
    f_ref[...] = f.astype(f_ref.dtype)
    if with_router:
        lg_ref[...] = _dot3(f, wr_ref[...]) + br_ref[...]


def _merge_call(h, yf, ys_all, ys_row0, ga, gb, mods, g1_row, sc2_row, sh2_row, n2g, wf, wa, wb, wo, layer,
                router, tm):
    b, l, d = h.shape
    w = yf.shape[-1]
    tok = lambda width: pl.BlockSpec((None, tm, width), lambda i, m: (i, m, 0))
    vec = lambda row: pl.BlockSpec((None, 1, d), lambda i, m: (row(i), 0, 0))
    full = lambda a: pl.BlockSpec(a.shape, lambda i, m: (0,) * a.ndim)
    per_layer = lambda a: pl.BlockSpec((None,) + a.shape[1:], lambda i, m: (layer,) + (0,) * (a.ndim - 1))
    blk0 = ys_row0 // tm
    assert blk0 * tm == ys_row0
    args = [h, yf, ys_all, ga, gb, mods, mods, mods, n2g, wf, wa, wb, wo]
    in_specs = [tok(d), tok(w), pl.BlockSpec((None, tm, w), lambda i, m: (i, m + blk0, 0)), tok(d), tok(d),
                vec(g1_row), vec(sc2_row), vec(sh2_row), per_layer(n2g), per_layer(wf), per_layer(wa),
                per_layer(wb), per_layer(wo)]
    out_specs = [tok(d), tok(d)]
    f_dtype = F32 if router is not None else BF16
    out_shape = [jax.ShapeDtypeStruct((b, l, d), F32), jax.ShapeDtypeStruct((b, l, d), f_dtype)]
    if router is not None:
        wr, br = router
        args += [wr, br]
        in_specs += [full(wr), full(br)]
        out_specs.append(tok(ROUTER_PAD))
        out_shape.append(jax.ShapeDtypeStruct((b, l, ROUTER_PAD), F32))
    return pl.pallas_call(
        functools.partial(_merge_kernel, with_router=router is not None),
        grid=(b, l // tm),
        in_specs=in_specs, out_specs=out_specs, out_shape=out_shape,
        compiler_params=_cparams(("parallel", "parallel")),
        name="merge",
    )(*args)


R_E1, R_E2, R_RANK1, R_RANK2, R_W1, R_W2 = range(6)


def _route_kernel(lg_ref, rec_ref, cnt_ref, carry_ref):
    @pl.when(pl.program_id(0) == 0)
    def _():
        carry_ref[...] = jnp.zeros_like(carry_ref)

    lg = lg_ref[...]
    tm = lg.shape[0]
    lane = lax.broadcasted_iota(jnp.int32, lg.shape, 1)
    m1 = jnp.max(lg, axis=-1, keepdims=True)
    i1 = jnp.min(jnp.where(lg == m1, lane, ROUTER_PAD), axis=-1, keepdims=True)
    lg2 = jnp.where(lane == i1, -jnp.inf, lg)
    m2 = jnp.max(lg2, axis=-1, keepdims=True)
    i2 = jnp.min(jnp.where(lg2 == m2, lane, ROUTER_PAD), axis=-1, keepdims=True)
    e = jnp.exp(m2 - m1)
    w1 = 1.0 / (1.0 + e)
    w2 = e * w1

    oh1 = lane == i1
    oh2 = lane == i2
    row = lax.broadcasted_iota(jnp.int32, (tm, tm), 0)
    col = lax.broadcasted_iota(jnp.int32, (tm, tm), 1)
    below = (row > col).astype(BF16)
    p1 = _dot(below, oh1.astype(BF16))
    p2 = _dot(below, oh2.astype(BF16))
    c1 = jnp.sum(oh1.astype(F32), axis=0, keepdims=True)
    c2 = jnp.sum(oh2.astype(F32), axis=0, keepdims=True)
    base = carry_ref[...]
    r1 = jnp.sum(jnp.where(oh1, p1 + base, 0.0), axis=-1, keepdims=True)
    r2 = jnp.sum(jnp.where(oh2, p2 + (base + c1), 0.0), axis=-1, keepdims=True)
    total = base + c1 + c2
    carry_ref[...] = total
    cnt_ref[...] = total

    rec = jnp.zeros_like(lg)
    for slot, val in ((R_E1, i1.astype(F32)), (R_E2, i2.astype(F32)), (R_RANK1, r1), (R_RANK2, r2),
                      (R_W1, w1), (R_W2, w2)):
        rec = jnp.where(lane == slot, val, rec)
    rec_ref[...] = rec


def _route_call(logits, tm):
    t, n = logits.shape
    return pl.pallas_call(
        _route_kernel,
        grid=(t // tm,),
        in_specs=[pl.BlockSpec((tm, n), lambda i: (i, 0))],
        out_specs=[pl.BlockSpec((tm, n), lambda i: (i, 0)), pl.BlockSpec((1, n), lambda i: (0, 0))],
        out_shape=[jax.ShapeDtypeStruct((t, n), F32), jax.ShapeDtypeStruct((1, n), F32)],
        scratch_shapes=[pltpu.VMEM((1, n), F32)],
        compiler_params=_cparams(("arbitrary",)),
        name="route",
    )(logits)


def _dispatch_kernel(pos_ref, f_ref, xs_in_ref, xs_ref, sem):
    del xs_in_ref
    tm = f_ref.shape[0]
    base = pl.program_id(0) * (2 * tm)

    def row_copy(r, k):
        return pltpu.make_async_copy(f_ref.at[pl.ds(r, 1)], xs_ref.at[pl.ds(pos_ref[base + 2 * r + k], 1)], sem)

    def issue(r, carry):
        row_copy(r, 0).start()
        row_copy(r, 1).start()
        return carry

    lax.fori_loop(0, tm, issue, 0, unroll=8)
    for _ in range(2):
        pltpu.make_async_copy(f_ref, xs_ref.at[pl.ds(0, tm)], sem).wait()


def _dispatch_call(pos, f, n_rows, tm):
    t, d = f.shape
    return pl.pallas_call(
        _dispatch_kernel,
        grid_spec=pltpu.PrefetchScalarGridSpec(
            num_scalar_prefetch=1,
            grid=(t // tm,),
            in_specs=[pl.BlockSpec((tm, d), lambda i, pos: (i, 0)),
                      pl.BlockSpec(memory_space=pl.ANY)],
            out_specs=pl.BlockSpec(memory_space=pl.ANY),
            scratch_shapes=[pltpu.SemaphoreType.DMA]),
        out_shape=jax.ShapeDtypeStruct((n_rows, d), f.dtype),
        input_output_aliases={2: 0},
        compiler_params=pltpu.CompilerParams(dimension_semantics=("arbitrary",), vmem_limit_bytes=VMEM_LIMIT,
                                             disable_bounds_checks=True),
        name="dispatch",
    )(pos, f, jnp.zeros((n_rows, d), f.dtype))


def _gffn_kernel(te_ref, nu_ref, x_ref, wg_ref, wu_ref, wd_ref, o_ref, xb_ref):
    del te_ref
    j = pl.program_id(1)

    @pl.when(pl.program_id(0) < nu_ref[0])
    def _():
        @pl.when(j == 0)
        def _():
            xb_ref[...] = x_ref[...].astype(BF16)
            o_ref[...] = jnp.zeros_like(o_ref)

        x = xb_ref[...]
        hid = jax.nn.silu(_dot(x, wg_ref[...])) * _dot(x, wu_ref[...])
        o_ref[...] += _dot(hid.astype(BF16), wd_ref[...])


def _gffn_call(tile_expert, n_used, xs, wg, wu, wd, idx, tm, tf):
    n_rows, d = xs.shape
    f = wg.shape[-1]
    nj = f // tf

    def row_map(i, j, te, nu):
        return jnp.minimum(i, nu[0] - 1), 0

    def jj(i, j, nu):
        return jnp.where(i < nu[0], j, nj - 1)

    return pl.pallas_call(
        _gffn_kernel,
        grid_spec=pltpu.PrefetchScalarGridSpec(
            num_scalar_prefetch=2,
            grid=(n_rows // tm, nj),
            in_specs=[pl.BlockSpec((tm, d), row_map),
                      pl.BlockSpec((None, None, d, tf), lambda i, j, te, nu: (idx, te[i], 0, jj(i, j, nu))),
                      pl.BlockSpec((None, None, d, tf), lambda i, j, te, nu: (idx, te[i], 0, jj(i, j, nu))),
                      pl.BlockSpec((None, None, tf, d), lambda i, j, te, nu: (idx, te[i], jj(i, j, nu), 0))],
            out_specs=pl.BlockSpec((tm, d), row_map),
            scratch_shapes=[pltpu.VMEM((tm, d), BF16)]),
        out_shape=jax.ShapeDtypeStruct((n_rows, d), F32),
        input_output_aliases={2: 0},
        compiler_params=_cparams(("arbitrary", "arbitrary")),
        name="gffn",
    )(tile_expert, n_used, xs, wg, wu, wd)


def _combine_kernel(pos_ref, ys_ref, h_ref, rec_ref, g2_ref, fg_ref, o_ref, ybuf_ref, sem, *, final_norm):
    tm = h_ref.shape[0]
    base = pl.program_id(0) * (2 * tm)

    def row_copy(r, k):
        return pltpu.make_async_copy(ys_ref.at[pl.ds(pos_ref[base + 2 * r + k], 1)],
                                     ybuf_ref.at[k, pl.ds(r, 1)], sem)

    def issue(r, carry):
        row_copy(r, 0).start()
        row_copy(r, 1).start()
        return carry

    lax.fori_loop(0, tm, issue, 0, unroll=8)
    for k in range(2):
        pltpu.make_async_copy(ys_ref.at[pl.ds(0, tm)], ybuf_ref.at[k], sem).wait()
    rec = rec_ref[...]
    y = rec[:, R_W1:R_W1 + 1] * ybuf_ref[0] + rec[:, R_W2:R_W2 + 1] * ybuf_ref[1]
    out = h_ref[...] + g2_ref[...] * y
    if final_norm:
        out = out * lax.rsqrt(jnp.mean(out * out, axis=-1, keepdims=True) + EPS) * fg_ref[...]
    o_ref[...] = out


def _combine_call(pos, ys, h, rec, mods, g2_row, tiles_per_batch, final_g, tm):
    t, d = h.shape
    fg = (final_g if final_g is not None else jnp.ones((d,), F32)).reshape(1, d)
    return pl.pallas_call(
        functools.partial(_combine_kernel, final_norm=final_g is not None),
        grid_spec=pltpu.PrefetchScalarGridSpec(
            num_scalar_prefetch=1,
            grid=(t // tm,),
            in_specs=[pl.BlockSpec(memory_space=pl.ANY),
                      pl.BlockSpec((tm, d), lambda i, pos: (i, 0)),
                      pl.BlockSpec((tm, ROUTER_PAD), lambda i, pos: (i, 0)),
                      pl.BlockSpec((None, 1, d), lambda i, pos: (g2_row(i // tiles_per_batch), 0, 0)),
                      pl.BlockSpec((1, d), lambda i, pos: (0, 0))],
            out_specs=pl.BlockSpec((tm, d), lambda i, pos: (i, 0)),
            scratch_shapes=[pltpu.VMEM((2, tm, d), F32), pltpu.SemaphoreType.DMA]),
        out_shape=jax.ShapeDtypeStruct((t, d), F32),
        compiler_params=pltpu.CompilerParams(dimension_semantics=("arbitrary",), vmem_limit_bytes=VMEM_LIMIT,
                                             disable_bounds_checks=True),
        name="combine",
    )(pos, ys, h, rec, mods, fg)


def _moe_plan(rec, counts, tm, n_tiles):
    cnt = counts[0, :N_EXPERTS].astype(jnp.int32)
    nt = (cnt + (tm - 1)) // tm
    cum = jnp.cumsum(nt)
    start = (cum - nt) * tm
    e = rec[:, R_E1:R_E2 + 1].astype(jnp.int32)
    rank = rec[:, R_RANK1:R_RANK2 + 1].astype(jnp.int32)
    ex = lax.broadcasted_iota(jnp.int32, e.shape + (N_EXPERTS,), 2)
    pos = rank + jnp.sum(jnp.where(e[..., None] == ex, start, 0), axis=-1)
    tile = jnp.arange(n_tiles, dtype=jnp.int32)
    te = jnp.sum((tile[:, None] >= cum[None, :]).astype(jnp.int32), axis=1)
    n_used = cum[-1:]
    last_e = jnp.sum((n_used - 1 >= cum).astype(jnp.int32))
    return pos.reshape(-1), jnp.minimum(te, last_e), n_used


def _ffn_kernel(x_ref, wg_ref, wu_ref, wd_ref, h_ref, g2_ref, fg_ref, o_ref, acc_ref, *, final_norm):
    j = pl.program_id(1)

    @pl.when(j == 0)
    def _():
        acc_ref[...] = jnp.zeros_like(acc_ref)

    x = x_ref[...]
    hid = jax.nn.silu(_dot(x, wg_ref[...])) * _dot(x, wu_ref[...])
    acc_ref[...] += _dot(hid.astype(BF16), wd_ref[...])

    @pl.when(j == pl.num_programs(1) - 1)
    def _():
        out = h_ref[...] + g2_ref[...] * acc_ref[...]
        if final_norm:
            out = out * lax.rsqrt(jnp.mean(out * out, axis=-1, keepdims=True) + EPS) * fg_ref[...]
        o_ref[...] = out


def _ffn_call(x, wg, wu, wd, idx, h, mods, g2_row, tiles_per_batch, final_g, tm, tf):
    t, d = x.shape
    f = wg.shape[-1]
    tok = lambda width: pl.BlockSpec((tm, width), lambda m, j: (m, 0))
    fg = (final_g if final_g is not None else jnp.ones((d,), F32)).reshape(1, d)
    return pl.pallas_call(
        functools.partial(_ffn_kernel, final_norm=final_g is not None),
        grid=(t // tm, f // tf),
        in_specs=[tok(d),
                  pl.BlockSpec((None, d, tf), lambda m, j: (idx, 0, j)),
                  pl.BlockSpec((None, d, tf), lambda m, j: (idx, 0, j)),
                  pl.BlockSpec((None, tf, d), lambda m, j: (idx, j, 0)),
                  tok(d),
                  pl.BlockSpec((None, 1, d), lambda m, j: (g2_row(m // tiles_per_batch), 0, 0)),
                  pl.BlockSpec((1, d), lambda m, j: (0, 0))],
        out_specs=tok(d),
        out_shape=jax.ShapeDtypeStruct((t, d), F32),
        scratch_shapes=[pltpu.VMEM((tm, d), F32)],
        compiler_params=_cparams(("parallel", "arbitrary")),
        name="ffn",
    )(x, wg, wu, wd, h, mods, fg)


def _pos_table(n, dim):
    t = np.arange(n)
    r = (t // GRID_W).astype(np.float32)
    col = (t % GRID_W).astype(np.float32)
    quarter = dim // 4
    omega = (1.0 / (POS_BASE ** (np.arange(quarter, dtype=np.float32) / quarter))).astype(np.float32)
    ar = r[:, None] * omega
    ac = col[:, None] * omega
    return np.concatenate([np.sin(ar), np.cos(ar), np.sin(ac), np.cos(ac)], axis=-1).astype(np.float32)


def _pick_tile(n, pref):
    return pref if n % pref == 0 else n


def kernel(x, c, ctx, c_ctx, w_mod, b_mod, norm1_g, norm2_g, w_in, w_four, ssm_a_re, ssm_a_im, ssm_log_dt, ssm_b_re, ssm_b_im, ssm_c_re, ssm_c_im, ssm_d, w_glu_a, w_glu_b, w_out, ffn_w_gate, ffn_w_up, ffn_w_down, moe_w_router, moe_b_router, moe_w_gate, moe_w_up, moe_w_down, final_g):
    b, l, d = x.shape
    lc = ctx.shape[1]
    depth = w_mod.shape[0]
    wf_cols = w_four.shape[1]
    ws_cols = ssm_d.shape[1]
    off_ga = wf_cols + ws_cols
    tm_lat = _pick_tile(l, 512)
    tm_ctx = _pick_tile(lc, 256)

    n_cond = b + 8
    cond = jnp.concatenate([c, jnp.broadcast_to(c_ctx[None], (n_cond - b, d))], axis=0)
    pos = jnp.asarray(_pos_table(l, d))

    mods = _mods_call(cond, w_mod, b_mod)
    w_in_b, wf, wa, wb, wo = (w.astype(BF16) for w in (w_in, w_four, w_glu_a, w_glu_b, w_out))
    ffn_w_gate, ffn_w_up, ffn_w_down = (w.astype(BF16) for w in (ffn_w_gate, ffn_w_up, ffn_w_down))
    moe_w_gate, moe_w_up, moe_w_down = (w.astype(BF16) for w in (moe_w_gate, moe_w_up, moe_w_down))
    n1g, n2g = norm1_g.reshape(depth, 1, d), norm2_g.reshape(depth, 1, d)
    d_skip = ssm_d.reshape(depth, 1, ws_cols)
    tables = jax.vmap(_s5_tables)(ssm_a_re, ssm_a_im, ssm_log_dt, ssm_b_re, ssm_b_im, ssm_c_re, ssm_c_im)

    def mod_rows(layer, which):
        return (lambda i: (layer * n_cond + i) * N_MODS + which,
                lambda i: (layer * n_cond + b) * N_MODS + which)

    h = x
    z = ctx
    for layer in range(depth):
        need_ctx = layer < depth - 1
        sh1, sc1, g1, sh2, sc2, g2 = (mod_rows(layer, k) for k in range(N_MODS))
        widths = (wf_cols, ws_cols, d, d)
        acts = (False, False, True, True)

        outs = _inproj_call(h, pos if layer == 0 else None, mods, sc1[0], sh1[0], n1g, w_in_b, layer, 0,
                            widths, acts, tm_lat)
        if layer == 0:
            h, outs = outs[0], outs[1:]
        pf, ps, ga, gb = outs
        if need_ctx:
            cpf, cps, cga, cgb = _inproj_call(z, None, mods, sc1[1], sh1[1], n1g, w_in_b, layer, 0,
                                              widths, acts, tm_ctx)
        else:
            (cps,) = _inproj_call(z, None, mods, sc1[1], sh1[1], n1g, w_in_b, layer, wf_cols,
                                  (ws_cols,), (False,), tm_ctx)

        ys_all = _s5_call(cps, ps, d_skip, tables, layer)
        yf = _fourier_call(pf, tm_lat)

        moe = layer % 2 == 1
        idx = layer // 2
        router = None
        if moe:
            wr = jnp.pad(moe_w_router[idx], ((0, 0), (0, ROUTER_PAD - N_EXPERTS)))
            br = jnp.pad(moe_b_router[idx][None], ((0, 0), (0, ROUTER_PAD - N_EXPERTS)), constant_values=NEG_BIG)
            router = (wr, br)
        res = _merge_call(h, yf, ys_all, 0, ga, gb, mods, g1[0], sc2[0], sh2[0], n2g, wf, wa, wb, wo, layer,
                          router, tm_lat)
        last = layer == depth - 1
        fin = final_g if last else None
        t = b * l
        if moe:
            h1, f, logits = res
            tm_moe = min(1024, max(128, t // 8))
            n_tiles = 2 * t // tm_moe + N_EXPERTS
            rec, counts = _route_call(logits.reshape(t, ROUTER_PAD), _pick_tile(t, 512))
            pos, tile_expert, n_used = _moe_plan(rec, counts, tm_moe, n_tiles)
            xs = _dispatch_call(pos, f.reshape(t, d), n_tiles * tm_moe, _pick_tile(t, 512))
            ys = _gffn_call(tile_expert, n_used, xs, moe_w_gate, moe_w_up, moe_w_down, idx,
                            tm_moe, _pick_tile(moe_w_gate.shape[-1], FFN_TILE))
            tm_c = _pick_tile(l, 512)
            h = _combine_call(pos, ys, h1.reshape(t, d), rec, mods, g2[0], l // tm_c, fin, tm_c).reshape(b, l, d)
        else:
            h1, f = res
            ffn_w = (ffn_w_gate, ffn_w_up, ffn_w_down, idx)
            tf = _pick_tile(ffn_w_gate.shape[-1], 256)
            tm_f = _pick_tile(l, 1024)
            h = _ffn_call(f.reshape(t, d), *ffn_w, h1.reshape(t, d), mods, g2[0], l // tm_f, fin,
                          tm_f, tf).reshape(b, l, d)

        if need_ctx:
            if moe:
                raise NotImplementedError("context tokens through an expert layer")
            cyf = _fourier_call(cpf, tm_ctx)
            z1, cf = _merge_call(z, cyf, ys_all, l, cga, cgb, mods, g1[1], sc2[1], sh2[1], n2g, wf, wa, wb, wo,
                                 layer, None, tm_ctx)
            z = _ffn_call(cf.reshape(b * lc, d), *ffn_w, z1.reshape(b * lc, d), mods, g2[1], 1, None,
                          _pick_tile(b * lc, 1024), tf).reshape(b, lc, d)
    return h
```

```python
import functools
import math

import numpy as np
import jax
import jax.numpy as jnp
from jax import lax
from jax.experimental import pallas as pl
from jax.experimental.pallas import tpu as pltpu

F32 = jnp.float32
BF16 = jnp.bfloat16

GRID_W = 64
N_FOURIER_GROUPS = 4
FOURIER_GROUP = 128
SSM_GROUP = 16
SSM_STATE = 64
N_EXPERTS = 8
EPS = 1e-6
POS_BASE = 10000.0

CHUNK = 16
LANES = 128
GPT = LANES // SSM_GROUP
N_COMP = 4
FFN_TILE = 512
ROUTER_PAD = LANES
NEG_BIG = -1e30
VMEM_LIMIT = 56 * 1024 * 1024


def _cparams(sem):
    return pltpu.CompilerParams(dimension_semantics=sem, vmem_limit_bytes=VMEM_LIMIT)


def _dot(a, b):
    return jnp.dot(a, b, preferred_element_type=F32)


def _split_bf16(a):
    hi = a.astype(BF16)
    lo = (a - hi.astype(F32)).astype(BF16)
    return hi, lo


def _dot3(a, b):
    ah, al = _split_bf16(a)
    bh, bl = _split_bf16(b)
    return _dot(ah, bh) + (_dot(ah, bl) + _dot(al, bh))


def _rms_mod(h, g, sc, sh):
    y = h * lax.rsqrt(jnp.mean(h * h, axis=-1, keepdims=True) + EPS)
    return (y * g) * (1.0 + sc) + sh


def _mods_kernel(c_ref, w_ref, b_ref, o_ref):
    s = jax.nn.silu(c_ref[...])
    o_ref[...] = _dot3(s, w_ref[...]) + b_ref[...]


N_MODS = 6
MOD_SHIFT1, MOD_SCALE1, MOD_GATE1, MOD_SHIFT2, MOD_SCALE2, MOD_GATE2 = range(N_MODS)


def _mods_call(cond, w_mod, b_mod):
    r, d = cond.shape
    depth, _, n = w_mod.shape
    tn = 1024
    out = pl.pallas_call(
        _mods_kernel,
        grid=(depth, n // tn),
        in_specs=[pl.BlockSpec((r, d), lambda i, j: (0, 0)),
                  pl.BlockSpec((None, d, tn), lambda i, j: (i, 0, j)),
                  pl.BlockSpec((None, 1, tn), lambda i, j: (i, 0, j))],
        out_specs=pl.BlockSpec((None, r, tn), lambda i, j: (i, 0, j)),
        out_shape=jax.ShapeDtypeStruct((depth, r, n), F32),
        compiler_params=_cparams(("arbitrary", "arbitrary")),
        name="mods",
    )(cond, w_mod, b_mod.reshape(depth, 1, n))
    return out.reshape(depth * r * N_MODS, 1, d)


def _inproj_kernel(*refs, acts, has_pos):
    it = iter(refs)
    x_ref = next(it)
    pos_ref = next(it) if has_pos else None
    sc_ref, sh_ref, g_ref, w_ref = next(it), next(it), next(it), next(it)
    h_out = next(it) if has_pos else None
    outs = list(it)
    h = x_ref[...]
    if has_pos:
        h = h + pos_ref[...]
        h_out[...] = h
    a = _rms_mod(h, g_ref[...], sc_ref[...], sh_ref[...]).astype(BF16)
    col = 0
    for o, act in zip(outs, acts):
        width = o.shape[-1]
        p = _dot(a, w_ref[:, col:col + width])
        if act:
            p = jax.nn.sigmoid(p)
        o[...] = p.astype(o.dtype)
        col += width


def _inproj_call(x, pos, mods, sc_row, sh_row, g, w, layer, col0, widths, acts, tm):
    b, l, d = x.shape
    has_pos = pos is not None
    n_cols = sum(widths)
    col_blk = col0 // n_cols
    assert col_blk * n_cols == col0
    tok = pl.BlockSpec((None, tm, d), lambda i, m: (i, m, 0))
    vec = lambda row: pl.BlockSpec((None, 1, d), lambda i, m: (row(i), 0, 0))
    in_specs = [tok]
    args = [x]
    if has_pos:
        in_specs.append(pl.BlockSpec((tm, d), lambda i, m: (m, 0)))
        args.append(pos)
    in_specs += [vec(sc_row), vec(sh_row), pl.BlockSpec((None, 1, d), lambda i, m: (layer, 0, 0)),
                 pl.BlockSpec((None, d, n_cols), lambda i, m: (layer, 0, col_blk))]
    args += [mods, mods, g, w]
    out_specs, out_shape = [], []
    if has_pos:
        out_specs.append(tok)
        out_shape.append(jax.ShapeDtypeStruct((b, l, d), F32))
    for width in widths:
        out_specs.append(pl.BlockSpec((None, tm, width), lambda i, m: (i, m, 0)))
        out_shape.append(jax.ShapeDtypeStruct((b, l, width), BF16))
    return pl.pallas_call(
        functools.partial(_inproj_kernel, acts=tuple(acts), has_pos=has_pos),
        grid=(b, l // tm),
        in_specs=in_specs, out_specs=out_specs, out_shape=out_shape,
        compiler_params=_cparams(("parallel", "parallel")),
        name="inproj",
    )(*args)


DFT_PAD_ROWS = 16


def _dft_tables(l):
    k = np.arange(l // 2 + DFT_PAD_ROWS, dtype=np.int64)
    n = np.arange(l, dtype=np.int64)
    ang = 2.0 * np.pi * ((k[:, None] * n[None, :]) % l).astype(np.float64) / l
    dl = np.concatenate([np.cos(ang), np.sin(ang)], axis=1).astype(np.float32)
    c = np.arange(FOURIER_GROUP, dtype=np.int64)
    angc = 2.0 * np.pi * ((c[:, None] * c[None, :]) % FOURIER_GROUP).astype(np.float64) / FOURIER_GROUP
    return dl, np.cos(angc).astype(np.float32), np.sin(angc).astype(np.float32)


def _fourier_kernel(u_ref, cc_ref, cs_ref, dl_ref, nxt_ref, o_ref, v_ref, *, l, scale):
    m = pl.program_id(1)
    tk = dl_ref.shape[0]

    @pl.when(m == 0)
    def _():
        for j in range(N_FOURIER_GROUPS):
            cols = slice(j * FOURIER_GROUP, (j + 1) * FOURIER_GROUP)
            uj = u_ref[:, cols]
            v_ref[0:l, cols] = _dot(uj, cc_ref[...]).astype(BF16)
            v_ref[l:2 * l, cols] = _dot(uj, cs_ref[...]).astype(BF16)

    p = _dot(dl_ref[:, :l], v_ref[0:l, :])
    q = _dot(dl_ref[:, l:], v_ref[l:2 * l, :])
    o_ref[pl.ds(pl.multiple_of(m * tk, tk), tk), :] = ((p - q) * scale).astype(o_ref.dtype)
    row = lax.broadcasted_iota(jnp.int32, (tk, tk), 0)
    col = lax.broadcasted_iota(jnp.int32, (tk, tk), 1)
    flip = (row + col == tk).astype(BF16)
    mirrored = _dot(flip, ((p + q) * scale).astype(BF16))
    nxt = _dot(nxt_ref[:, :l], v_ref[0:l, :]) + _dot(nxt_ref[:, l:], v_ref[l:2 * l, :])
    first = lax.broadcasted_iota(jnp.int32, mirrored.shape, 0) == 0
    upper = jnp.where(first, nxt[0:1] * scale, mirrored)
    o_ref[pl.ds(pl.multiple_of(l - (m + 1) * tk, tk), tk), :] = upper.astype(o_ref.dtype)


def _fourier_call(u, tm):
    b, l, w = u.shape
    dl, cc, cs = _dft_tables(l)
    dl = jnp.asarray(dl).astype(BF16)
    cc = jnp.asarray(cc).astype(BF16)
    cs = jnp.asarray(cs).astype(BF16)
    scale = 1.0 / math.sqrt(l * FOURIER_GROUP)
    tk = min(tm, l // 2)
    nxt_blocks = tk // DFT_PAD_ROWS
    return pl.pallas_call(
        functools.partial(_fourier_kernel, l=l, scale=scale),
        grid=(b, l // 2 // tk),
        in_specs=[pl.BlockSpec((None, l, w), lambda i, m: (i, 0, 0)),
                  pl.BlockSpec((FOURIER_GROUP, FOURIER_GROUP), lambda i, m: (0, 0)),
                  pl.BlockSpec((FOURIER_GROUP, FOURIER_GROUP), lambda i, m: (0, 0)),
                  pl.BlockSpec((tk, 2 * l), lambda i, m: (m, 0)),
                  pl.BlockSpec((DFT_PAD_ROWS, 2 * l), lambda i, m: ((m + 1) * nxt_blocks, 0))],
        out_specs=pl.BlockSpec((None, l, w), lambda i, m: (i, 0, 0)),
        out_shape=jax.ShapeDtypeStruct((b, l, w), BF16),
        scratch_shapes=[pltpu.VMEM((2 * l, w), BF16)],
        compiler_params=_cparams(("parallel", "arbitrary")),
        name="fourier",
    )(u, cc, cs, dl, dl)


def _expand_matrix(n_outer, inner):
    r = np.arange(n_outer * inner)[:, None]
    c = np.arange(n_outer * GPT * inner)[None, :]
    same = (r // inner == c // (GPT * inner)) & (r % inner == c % inner)
    return jnp.asarray(same.astype(np.float32)).astype(BF16)


def _spread_groups(x, expand, row_inner, col_inner):
    y = jnp.einsum('jrc,cn->jrn', x, expand, preferred_element_type=BF16)
    r = lax.broadcasted_iota(jnp.int32, y.shape, 1)
    c = lax.broadcasted_iota(jnp.int32, y.shape, 2)
    return jnp.where((r // row_inner) % GPT == (c // col_inner) % GPT, y, jnp.zeros_like(y))


def _s5_tables(a_re, a_im, log_dt, b_re, b_im, c_re, c_im):
    hp = lax.Precision.HIGHEST
    q = CHUNK
    g = a_re.shape[1]
    dt = jnp.exp(log_dt)[..., None]
    lr, li = a_re * dt, a_im * dt
    em1_r = jnp.expm1(lr) * jnp.cos(li) - 2.0 * jnp.sin(0.5 * li) ** 2
    em1_i = jnp.exp(lr) * jnp.sin(li)
    den = a_re * a_re + a_im * a_im
    fr = (em1_r * a_re + em1_i * a_im) / den
    fi = (em1_i * a_re - em1_r * a_im) / den
    bbr = fr[..., None] * b_re - fi[..., None] * b_im
    bbi = fr[..., None] * b_im + fi[..., None] * b_re
    k = jnp.arange(q + 1, dtype=F32)
    mag = jnp.exp(lr[..., None] * k)
    pr = mag * jnp.cos(li[..., None] * k)
    pi = mag * jnp.sin(li[..., None] * k)

    prk = jnp.moveaxis(pr, -1, 2)[..., None]
    pik = jnp.moveaxis(pi, -1, 2)[..., None]
    wr = prk * bbr[:, :, None] - pik * bbi[:, :, None]
    wi = prk * bbi[:, :, None] + pik * bbr[:, :, None]
    kern = (jnp.einsum('dgop,dgtph->dgtoh', c_re, wr[:, :, :q], precision=hp)
            - jnp.einsum('dgop,dgtph->dgtoh', c_im, wi[:, :, :q], precision=hp))
    qi = jnp.arange(q)
    nt = g // GPT
    kf = kern[0].transpose(0, 1, 3, 2)
    kb = kern[1].transpose(0, 1, 3, 2)
    lag = jnp.concatenate([kb[:, :0:-1], kf[:, :1] + kb[:, :1], kf[:, 1:]], axis=1).astype(BF16)
    lag = (lag.reshape(nt, GPT, 2 * q - 1, SSM_GROUP, SSM_GROUP).transpose(0, 2, 1, 3, 4)
           .reshape(nt * (2 * q - 1), LANES, SSM_GROUP))
    lag_tile = _spread_groups(lag, _expand_matrix(1, SSM_GROUP), SSM_GROUP, SSM_GROUP)
    lag_tile = lag_tile.reshape(nt, 2 * q - 1, LANES, LANES)

    def inject(w):
        wf = w[0][:, q - 1 - qi]
        wb = w[1][:, qi]
        both = jnp.stack([wf, wb])
        return both.transpose(0, 1, 2, 4, 3).reshape(2, g, q * SSM_GROUP, SSM_STATE)

    wre, wim = inject(wr), inject(wi)

    def carry_out(sign):
        outs = []
        for d, idx in ((0, qi + 1), (1, q - qi)):
            ppr = pr[d][:, :, idx]
            ppi = pi[d][:, :, idx]
            cr = c_re[d].transpose(0, 2, 1)[:, :, None, :]
            ci = c_im[d].transpose(0, 2, 1)[:, :, None, :]
            if sign > 0:
                val = cr * ppr[..., None] - ci * ppi[..., None]
            else:
                val = -(cr * ppi[..., None] + ci * ppr[..., None])
            outs.append(val.reshape(g, SSM_STATE, q * SSM_GROUP))
        return jnp.stack(outs)

    cre, cim = carry_out(+1), carry_out(-1)

    qh = q * SSM_GROUP
    wc = jnp.stack([wre[0], wim[0], wre[1], wim[1]]).astype(BF16)
    wc = (wc.reshape(N_COMP, nt, GPT, q, SSM_GROUP, SSM_STATE).transpose(1, 3, 2, 4, 0, 5)
          .reshape(nt, q * LANES, N_COMP * SSM_STATE))
    cc = jnp.stack([cre[0], cim[0], cre[1], cim[1]]).astype(BF16)
    cc = (cc.reshape(N_COMP, nt, GPT, SSM_STATE, qh).transpose(1, 0, 2, 3, 4)
          .reshape(nt, N_COMP * GPT * SSM_STATE, qh))
    decay = jnp.stack([pr[0, ..., q], pi[0, ..., q], pr[1, ..., q], pi[1, ..., q]])
    decay = decay.reshape(N_COMP, nt, 1, GPT * SSM_STATE).transpose(1, 0, 2, 3)
    return lag_tile, wc, cc, decay


def _chunk_rows(u_ref):
    nc, _, nb, lanes = u_ref.shape
    return jnp.concatenate([u_ref[:, qq].reshape(nc * nb, lanes) for qq in range(CHUNK)], axis=-1)


SPREAD_COLS = 512


def _spread_into(dst_ref, compact_ref, expand_ref, row_inner, col_inner):
    rows, cols = dst_ref.shape
    rsh, csh = row_inner.bit_length() - 1, col_inner.bit_length() - 1
    assert 1 << rsh == row_inner and 1 << csh == col_inner
    r = lax.broadcasted_iota(jnp.int32, (rows, SPREAD_COLS), 0)
    c = lax.broadcasted_iota(jnp.int32, (rows, SPREAD_COLS), 1)
    r_gi = lax.shift_right_logical(r, rsh) & (GPT - 1)
    for c0 in range(0, cols, SPREAD_COLS):
        y = _dot(compact_ref[...], expand_ref[:, c0:c0 + SPREAD_COLS])
        c_gi = lax.shift_right_logical(c + c0, csh) & (GPT - 1)
        dst_ref[:, c0:c0 + SPREAD_COLS] = jnp.where(r_gi == c_gi, y, 0.0).astype(dst_ref.dtype)


def _s5_inject_kernel(u_ref, wc_ref, exp_ref, s_ref, w_ref):
    @pl.when(pl.program_id(1) == 0)
    def _():
        _spread_into(w_ref, wc_ref, exp_ref, SSM_GROUP, SSM_STATE)

    s = _dot(_chunk_rows(u_ref), w_ref[...])
    width = s_ref.shape[-1]
    for comp in range(N_COMP):
        s_ref[comp] = s[:, comp * width:(comp + 1) * width]


def _s5_scan_kernel(s_ref, a_ref, o_ref, *, batch, n_lat_chunks, n_ctx_chunks):
    backward = pl.program_id(1) == 1
    ar, ai = a_ref[0], a_ref[1]

    def step(i, carry):
        sr, si = carry
        k = i - n_ctx_chunks
        fwd = jnp.where(i < n_ctx_chunks, n_lat_chunks + i, k)
        bwd = jnp.where(i < n_ctx_chunks, n_lat_chunks + n_ctx_chunks - 1 - i, n_lat_chunks - 1 - k)
        rows = pl.ds(pl.multiple_of(jnp.where(backward, bwd, fwd) * batch, batch), batch)
        o_ref[0, rows, :] = sr.astype(o_ref.dtype)
        o_ref[1, rows, :] = si.astype(o_ref.dtype)
        return (ar * sr - ai * si + s_ref[0, rows, :], ar * si + ai * sr + s_ref[1, rows, :])

    z = jnp.zeros((batch, s_ref.shape[-1]), F32)
    lax.fori_loop(0, n_lat_chunks + n_ctx_chunks, step, (z, z))


def _s5_out_kernel(u_ref, sp_ref, lag_ref, cc_ref, exp_ref, d_ref, o_ref, m_ref, c_ref):
    nc, _, nb, lanes = u_ref.shape
    width = sp_ref.shape[-1]

    @pl.when(pl.program_id(1) == 0)
    def _():
        for q_in in range(CHUNK):
            for q_out in range(CHUNK):
                m_ref[q_in * lanes:(q_in + 1) * lanes, q_out * lanes:(q_out + 1) * lanes] = (
                    lag_ref[q_out - q_in + CHUNK - 1])
        _spread_into(c_ref, cc_ref, exp_ref, SSM_STATE, SSM_GROUP)

    y = _dot(_chunk_rows(u_ref), m_ref[...])
    for comp in range(N_COMP):
        y = y + _dot(sp_ref[comp], c_ref[comp * width:(comp + 1) * width, :])
    for qq in range(CHUNK):
        yq = y[:, qq * lanes:(qq + 1) * lanes].reshape(nc, nb, lanes)
        o_ref[:, qq] = jax.nn.gelu(d_ref[...] * u_ref[:, qq].astype(F32) + yq).astype(o_ref.dtype)


def _s5_call(u_ctx, u_lat, d_skip, tables, layer):
    lag_tile, wc, cc, decay = tables
    exp_w = _expand_matrix(N_COMP, SSM_STATE)
    exp_c = _expand_matrix(CHUNK, SSM_GROUP)
    op_rows = CHUNK * LANES
    full2 = lambda a: pl.BlockSpec(a.shape, lambda j, r: (0, 0))
    b, lc, w = u_ctx.shape
    l = u_lat.shape[1]
    nt = w // LANES
    n_lat, n_ctx = l // CHUNK, lc // CHUNK
    n_chunks = n_lat + n_ctx
    rows = n_chunks * b
    sw = GPT * SSM_STATE
    u_t = jnp.concatenate([u_lat, u_ctx], axis=1).transpose(1, 0, 2).reshape(n_chunks, CHUNK, b, w)
    cb = _pick_tile(n_chunks, 18)
    rb = cb * b
    u_spec = pl.BlockSpec((cb, CHUNK, b, LANES), lambda j, r: (r, 0, 0, j))
    op_spec = lambda shape: pl.BlockSpec((None, None) + shape, lambda j, r: (layer, j, 0, 0))
    s_loc = pl.pallas_call(
        _s5_inject_kernel,
        grid=(nt, n_chunks // cb),
        in_specs=[u_spec, op_spec(wc.shape[2:]), full2(exp_w)],
        out_specs=pl.BlockSpec((None, N_COMP, rb, sw), lambda j, r: (j, 0, r, 0)),
        out_shape=jax.ShapeDtypeStruct((nt, N_COMP, rows, sw), F32),
        scratch_shapes=[pltpu.VMEM((op_rows, N_COMP * sw), BF16)],
        compiler_params=_cparams(("parallel", "arbitrary")),
        name="s5_inject",
    )(u_t, wc, exp_w)
    s_prev = pl.pallas_call(
        functools.partial(_s5_scan_kernel, batch=b, n_lat_chunks=n_lat, n_ctx_chunks=n_ctx),
        grid=(nt, 2),
        in_specs=[pl.BlockSpec((None, 2, rows, sw), lambda j, d: (j, d, 0, 0)),
                  pl.BlockSpec((None, None, 2, 1, sw), lambda j, d: (layer, j, d, 0, 0))],
        out_specs=pl.BlockSpec((None, 2, rows, sw), lambda j, d: (j, d, 0, 0)),
        out_shape=jax.ShapeDtypeStruct((nt, N_COMP, rows, sw), BF16),
        compiler_params=_cparams(("parallel", "parallel")),
        name="s5_scan",
    )(s_loc, decay)
    y = pl.pallas_call(
        _s5_out_kernel,
        grid=(nt, n_chunks // cb),
        in_specs=[u_spec,
                  pl.BlockSpec((None, N_COMP, rb, sw), lambda j, r: (j, 0, r, 0)),
                  pl.BlockSpec((None, None) + lag_tile.shape[2:], lambda j, r: (layer, j, 0, 0, 0)),
                  op_spec(cc.shape[2:]), full2(exp_c),
                  pl.BlockSpec((None, 1, LANES), lambda j, r: (layer, 0, j))],
        out_specs=u_spec,
        out_shape=jax.ShapeDtypeStruct((n_chunks, CHUNK, b, w), BF16),
        scratch_shapes=[pltpu.VMEM((op_rows, op_rows), BF16), pltpu.VMEM((N_COMP * sw, op_rows), BF16)],
        compiler_params=_cparams(("parallel", "arbitrary")),
        name="s5_out",
    )(u_t, s_prev, lag_tile, cc, exp_c, d_skip)
    return y.reshape(l + lc, b, w).transpose(1, 0, 2)


def _merge_kernel(*refs, with_router):
    (h_ref, yf_ref, ys_ref, ga_ref, gb_ref, g1_ref, sc2_ref, sh2_ref, n2g_ref,
     wf_ref, wa_ref, wb_ref, wo_ref) = refs[:13]
    rest = refs[13:]
    if with_router:
        wr_ref, br_ref, h1_ref, f_ref, lg_ref = rest
    else:
        h1_ref, f_ref = rest
    ys = ys_ref[...]
    ya = _dot(yf_ref[...], wf_ref[...])
    yb = _dot(ys, wa_ref[...]) * jax.nn.sigmoid(_dot(ys, wb_ref[...]))
    m = (ga_ref[...].astype(F32) * ya + gb_ref[...].astype(F32) * yb).astype(BF16)
    h1 = h_ref[...] + g1_ref[...] * _dot(m, wo_ref[...])
    h1_ref[...] = h1
    f = _rms_mod(h1, n2g_ref[...], sc2_ref[...], sh2_ref[...])
    f_ref[...] = f.astype(f_ref.dtype)
    if with_router:
        lg_ref[...] = _dot3(f, wr_ref[...]) + br_ref[...]


def _merge_call(h, yf, ys_all, ys_row0, ga, gb, mods, g1_row, sc2_row, sh2_row, n2g, wf, wa, wb, wo, layer,
                router, tm):
    b, l, d = h.shape
    w = yf.shape[-1]
    tok = lambda width: pl.BlockSpec((None, tm, width), lambda i, m: (i, m, 0))
    vec = lambda row: pl.BlockSpec((None, 1, d), lambda i, m: (row(i), 0, 0))
    full = lambda a: pl.BlockSpec(a.shape, lambda i, m: (0,) * a.ndim)
    per_layer = lambda a: pl.BlockSpec((None,) + a.shape[1:], lambda i, m: (layer,) + (0,) * (a.ndim - 1))
    blk0 = ys_row0 // tm
    assert blk0 * tm == ys_row0
    args = [h, yf, ys_all, ga, gb, mods, mods, mods, n2g, wf, wa, wb, wo]
    in_specs = [tok(d), tok(w), pl.BlockSpec((None, tm, w), lambda i, m: (i, m + blk0, 0)), tok(d), tok(d),
                vec(g1_row), vec(sc2_row), vec(sh2_row), per_layer(n2g), per_layer(wf), per_layer(wa),
                per_layer(wb), per_layer(wo)]
    out_specs = [tok(d), tok(d)]
    f_dtype = F32 if router is not None else BF16
    out_shape = [jax.ShapeDtypeStruct((b, l, d), F32), jax.ShapeDtypeStruct((b, l, d), f_dtype)]
    if router is not None:
        wr, br = router
        args += [wr, br]
        in_specs += [full(wr), full(br)]
        out_specs.append(tok(ROUTER_PAD))
        out_shape.append(jax.ShapeDtypeStruct((b, l, ROUTER_PAD), F32))
    return pl.pallas_call(
        functools.partial(_merge_kernel, with_router=router is not None),
        grid=(b, l // tm),
        in_specs=in_specs, out_specs=out_specs, out_shape=out_shape,
        compiler_params=_cparams(("parallel", "parallel")),
        name="merge",
    )(*args)


R_E1, R_E2, R_RANK1, R_RANK2, R_W1, R_W2 = range(6)


def _route_kernel(lg_ref, rec_ref, cnt_ref, carry_ref):
    @pl.when(pl.program_id(0) == 0)
    def _():
        carry_ref[...] = jnp.zeros_like(carry_ref)

    lg = lg_ref[...]
    tm = lg.shape[0]
    lane = lax.broadcasted_iota(jnp.int32, lg.shape, 1)
    m1 = jnp.max(lg, axis=-1, keepdims=True)
    i1 = jnp.min(jnp.where(lg == m1, lane, ROUTER_PAD), axis=-1, keepdims=True)
    lg2 = jnp.where(lane == i1, -jnp.inf, lg)
    m2 = jnp.max(lg2, axis=-1, keepdims=True)
    i2 = jnp.min(jnp.where(lg2 == m2, lane, ROUTER_PAD), axis=-1, keepdims=True)
    e = jnp.exp(m2 - m1)
    w1 = 1.0 / (1.0 + e)
    w2 = e * w1

    oh1 = lane == i1
    oh2 = lane == i2
    row = lax.broadcasted_iota(jnp.int32, (tm, tm), 0)
    col = lax.broadcasted_iota(jnp.int32, (tm, tm), 1)
    below = (row > col).astype(BF16)
    p1 = _dot(below, oh1.astype(BF16))
    p2 = _dot(below, oh2.astype(BF16))
    c1 = jnp.sum(oh1.astype(F32), axis=0, keepdims=True)
    c2 = jnp.sum(oh2.astype(F32), axis=0, keepdims=True)
    base = carry_ref[...]
    r1 = jnp.sum(jnp.where(oh1, p1 + base, 0.0), axis=-1, keepdims=True)
    r2 = jnp.sum(jnp.where(oh2, p2 + (base + c1), 0.0), axis=-1, keepdims=True)
    total = base + c1 + c2
    carry_ref[...] = total
    cnt_ref[...] = total

    rec = jnp.zeros_like(lg)
    for slot, val in ((R_E1, i1.astype(F32)), (R_E2, i2.astype(F32)), (R_RANK1, r1), (R_RANK2, r2),
                      (R_W1, w1), (R_W2, w2)):
        rec = jnp.where(lane == slot, val, rec)
    rec_ref[...] = rec


def _route_call(logits, tm):
    t, n = logits.shape
    return pl.pallas_call(
        _route_kernel,
        grid=(t // tm,),
        in_specs=[pl.BlockSpec((tm, n), lambda i: (i, 0))],
        out_specs=[pl.BlockSpec((tm, n), lambda i: (i, 0)), pl.BlockSpec((1, n), lambda i: (0, 0))],
        out_shape=[jax.ShapeDtypeStruct((t, n), F32), jax.ShapeDtypeStruct((1, n), F32)],
        scratch_shapes=[pltpu.VMEM((1, n), F32)],
        compiler_params=_cparams(("arbitrary",)),
        name="route",
    )(logits)


def _dispatch_kernel(pos_ref, f_ref, xs_in_ref, xs_ref, sem):
    del xs_in_ref
    tm = f_ref.shape[0]
    base = pl.program_id(0) * (2 * tm)

    def row_copy(r, k):
        return pltpu.make_async_copy(f_ref.at[pl.ds(r, 1)], xs_ref.at[pl.ds(pos_ref[base + 2 * r + k], 1)], sem)

    def issue(r, carry):
        row_copy(r, 0).start()
        row_copy(r, 1).start()
        return carry

    lax.fori_loop(0, tm, issue, 0, unroll=8)
    for _ in range(2):
        pltpu.make_async_copy(f_ref, xs_ref.at[pl.ds(0, tm)], sem).wait()


def _dispatch_call(pos, f, n_rows, tm):
    t, d = f.shape
    return pl.pallas_call(
        _dispatch_kernel,
        grid_spec=pltpu.PrefetchScalarGridSpec(
            num_scalar_prefetch=1,
            grid=(t // tm,),
            in_specs=[pl.BlockSpec((tm, d), lambda i, pos: (i, 0)),
                      pl.BlockSpec(memory_space=pl.ANY)],
            out_specs=pl.BlockSpec(memory_space=pl.ANY),
            scratch_shapes=[pltpu.SemaphoreType.DMA]),
        out_shape=jax.ShapeDtypeStruct((n_rows, d), f.dtype),
        input_output_aliases={2: 0},
        compiler_params=pltpu.CompilerParams(dimension_semantics=("arbitrary",), vmem_limit_bytes=VMEM_LIMIT,
                                             disable_bounds_checks=True),
        name="dispatch",
    )(pos, f, jnp.zeros((n_rows, d), f.dtype))


N_WBUF = 3


def _weight_ring(wg_hbm, wu_hbm, wd_hbm, wg_buf, wu_buf, wd_buf, sem, n_steps, weight_set):
    i, j = pl.program_id(0), pl.program_id(1)
    nj = pl.num_programs(1)
    tf = wg_buf.shape[-1]
    g = i * nj + j

    def copies(ci, cj, slot):
        cols = pl.ds(pl.multiple_of(cj * tf, tf), tf)
        s = weight_set(ci)
        return (pltpu.make_async_copy(wg_hbm.at[s, :, cols], wg_buf.at[slot], sem.at[0, slot]),
                pltpu.make_async_copy(wu_hbm.at[s, :, cols], wu_buf.at[slot], sem.at[1, slot]),
                pltpu.make_async_copy(wd_hbm.at[s, cols, :], wd_buf.at[slot], sem.at[2, slot]))

    def start_ahead(ahead):
        ci = lax.div(g + ahead, nj)
        cj = g + ahead - ci * nj
        ci = jnp.minimum(ci, pl.num_programs(0) - 1)

        @pl.when(g + ahead < n_steps)
        def _():
            for c in copies(ci, cj, lax.rem(g + ahead, N_WBUF)):
                c.start()

    @pl.when(g == 0)
    def _():
        for ahead in range(N_WBUF - 1):
            start_ahead(ahead)

    start_ahead(N_WBUF - 1)
    slot = lax.rem(g, N_WBUF)

    @pl.when(g < n_steps)
    def _():
        for c in copies(i, j, slot):
            c.wait()

    return slot


def _gffn_kernel(te_ref, nu_ref, x_ref, wg_hbm, wu_hbm, wd_hbm, o_ref, xb_ref, wg_buf, wu_buf, wd_buf, sem,
                 *, set0):
    j = pl.program_id(1)
    slot = _weight_ring(wg_hbm, wu_hbm, wd_hbm, wg_buf, wu_buf, wd_buf, sem,
                        nu_ref[0] * pl.num_programs(1), lambda ci: set0 + te_ref[ci])

    @pl.when(pl.program_id(0) < nu_ref[0])
    def _():
        @pl.when(j == 0)
        def _():
            xb_ref[...] = x_ref[...].astype(BF16)
            o_ref[...] = jnp.zeros_like(o_ref)

        x = xb_ref[...]
        hid = jax.nn.silu(_dot(x, wg_buf[slot].astype(BF16))) * _dot(x, wu_buf[slot].astype(BF16))
        o_ref[...] += _dot(hid.astype(BF16), wd_buf[slot].astype(BF16))


def _gffn_call(tile_expert, n_used, xs, wg, wu, wd, idx, tm, tf):
    n_rows, d = xs.shape
    n_e, f = wg.shape[1], wg.shape[-1]
    nj = f // tf
    wg, wu, wd = (w.reshape((-1,) + w.shape[2:]) for w in (wg, wu, wd))

    def row_map(i, j, te, nu):
        return jnp.minimum(i, nu[0] - 1), 0

    hbm = pl.BlockSpec(memory_space=pl.ANY)
    return pl.pallas_call(
        functools.partial(_gffn_kernel, set0=idx * n_e),
        grid_spec=pltpu.PrefetchScalarGridSpec(
            num_scalar_prefetch=2,
            grid=(n_rows // tm, nj),
            in_specs=[pl.BlockSpec((tm, d), row_map), hbm, hbm, hbm],
            out_specs=pl.BlockSpec((tm, d), row_map),
            scratch_shapes=[pltpu.VMEM((tm, d), BF16),
                            pltpu.VMEM((N_WBUF, d, tf), wg.dtype), pltpu.VMEM((N_WBUF, d, tf), wu.dtype),
                            pltpu.VMEM((N_WBUF, tf, d), wd.dtype), pltpu.SemaphoreType.DMA((3, N_WBUF))]),
        out_shape=jax.ShapeDtypeStruct((n_rows, d), F32),
        input_output_aliases={2: 0},
        compiler_params=_cparams(("arbitrary", "arbitrary")),
        name="gffn",
    )(tile_expert, n_used, xs, wg, wu, wd)


def _combine_kernel(pos_ref, ys_ref, h_ref, rec_ref, g2_ref, fg_ref, o_ref, ybuf_ref, sem, *, final_norm):
    tm = h_ref.shape[0]
    base = pl.program_id(0) * (2 * tm)

    def row_copy(r, k):
        return pltpu.make_async_copy(ys_ref.at[pl.ds(pos_ref[base + 2 * r + k], 1)],
                                     ybuf_ref.at[k, pl.ds(r, 1)], sem)

    def issue(r, carry):
        row_copy(r, 0).start()
        row_copy(r, 1).start()
        return carry

    lax.fori_loop(0, tm, issue, 0, unroll=8)
    for k in range(2):
        pltpu.make_async_copy(ys_ref.at[pl.ds(0, tm)], ybuf_ref.at[k], sem).wait()
    rec = rec_ref[...]
    y = rec[:, R_W1:R_W1 + 1] * ybuf_ref[0] + rec[:, R_W2:R_W2 + 1] * ybuf_ref[1]
    out = h_ref[...] + g2_ref[...] * y
    if final_norm:
        out = out * lax.rsqrt(jnp.mean(out * out, axis=-1, keepdims=True) + EPS) * fg_ref[...]
    o_ref[...] = out


def _combine_call(pos, ys, h, rec, mods, g2_row, tiles_per_batch, final_g, tm):
    t, d = h.shape
    fg = (final_g if final_g is not None else jnp.ones((d,), F32)).reshape(1, d)
    return pl.pallas_call(
        functools.partial(_combine_kernel, final_norm=final_g is not None),
        grid_spec=pltpu.PrefetchScalarGridSpec(
            num_scalar_prefetch=1,
            grid=(t // tm,),
            in_specs=[pl.BlockSpec(memory_space=pl.ANY),
                      pl.BlockSpec((tm, d), lambda i, pos: (i, 0)),
                      pl.BlockSpec((tm, ROUTER_PAD), lambda i, pos: (i, 0)),
                      pl.BlockSpec((None, 1, d), lambda i, pos: (g2_row(i // tiles_per_batch), 0, 0)),
                      pl.BlockSpec((1, d), lambda i, pos: (0, 0))],
            out_specs=pl.BlockSpec((tm, d), lambda i, pos: (i, 0)),
            scratch_shapes=[pltpu.VMEM((2, tm, d), F32), pltpu.SemaphoreType.DMA]),
        out_shape=jax.ShapeDtypeStruct((t, d), F32),
        compiler_params=pltpu.CompilerParams(dimension_semantics=("arbitrary",), vmem_limit_bytes=VMEM_LIMIT,
                                             disable_bounds_checks=True),
        name="combine",
    )(pos, ys, h, rec, mods, fg)


def _moe_plan(rec, counts, tm, n_tiles):
    cnt = counts[0, :N_EXPERTS].astype(jnp.int32)
    nt = (cnt + (tm - 1)) // tm
    cum = jnp.cumsum(nt)
    start = (cum - nt) * tm
    e = rec[:, R_E1:R_E2 + 1].astype(jnp.int32)
    rank = rec[:, R_RANK1:R_RANK2 + 1].astype(jnp.int32)
    ex = lax.broadcasted_iota(jnp.int32, e.shape + (N_EXPERTS,), 2)
    pos = rank + jnp.sum(jnp.where(e[..., None] == ex, start, 0), axis=-1)
    tile = jnp.arange(n_tiles, dtype=jnp.int32)
    te = jnp.sum((tile[:, None] >= cum[None, :]).astype(jnp.int32), axis=1)
    n_used = cum[-1:]
    last_e = jnp.sum((n_used - 1 >= cum).astype(jnp.int32))
    return pos.reshape(-1), jnp.minimum(te, last_e), n_used


def _ffn_kernel(x_ref, wg_hbm, wu_hbm, wd_hbm, h_ref, g2_ref, fg_ref, o_ref, acc_ref, wg_buf, wu_buf, wd_buf, sem,
                *, final_norm, idx):
    j = pl.program_id(1)
    slot = _weight_ring(wg_hbm, wu_hbm, wd_hbm, wg_buf, wu_buf, wd_buf, sem,
                        pl.num_programs(0) * pl.num_programs(1), lambda ci: idx)

    @pl.when(j == 0)
    def _():
        acc_ref[...] = jnp.zeros_like(acc_ref)

    x = x_ref[...]
    hid = jax.nn.silu(_dot(x, wg_buf[slot].astype(BF16))) * _dot(x, wu_buf[slot].astype(BF16))
    acc_ref[...] += _dot(hid.astype(BF16), wd_buf[slot].astype(BF16))

    @pl.when(j == pl.num_programs(1) - 1)
    def _():
        out = h_ref[...] + g2_ref[...] * acc_ref[...]
        if final_norm:
            out = out * lax.rsqrt(jnp.mean(out * out, axis=-1, keepdims=True) + EPS) * fg_ref[...]
        o_ref[...] = out


def _ffn_call(x, wg, wu, wd, idx, h, mods, g2_row, tiles_per_batch, final_g, tm, tf):
    t, d = x.shape
    f = wg.shape[-1]
    tok = lambda width: pl.BlockSpec((tm, width), lambda m, j: (m, 0))
    fg = (final_g if final_g is not None else jnp.ones((d,), F32)).reshape(1, d)
    hbm = pl.BlockSpec(memory_space=pl.ANY)
    return pl.pallas_call(
        functools.partial(_ffn_kernel, final_norm=final_g is not None, idx=idx),
        grid=(t // tm, f // tf),
        in_specs=[tok(d), hbm, hbm, hbm,
                  tok(d),
                  pl.BlockSpec((None, 1, d), lambda m, j: (g2_row(m // tiles_per_batch), 0, 0)),
                  pl.BlockSpec((1, d), lambda m, j: (0, 0))],
        out_specs=tok(d),
        out_shape=jax.ShapeDtypeStruct((t, d), F32),
        scratch_shapes=[pltpu.VMEM((tm, d), F32),
                        pltpu.VMEM((N_WBUF, d, tf), wg.dtype), pltpu.VMEM((N_WBUF, d, tf), wu.dtype),
                        pltpu.VMEM((N_WBUF, tf, d), wd.dtype), pltpu.SemaphoreType.DMA((3, N_WBUF))],
        compiler_params=_cparams(("arbitrary", "arbitrary")),
        name="ffn",
    )(x, wg, wu, wd, h, mods, fg)


def _pos_table(n, dim):
    t = np.arange(n)
    r = (t // GRID_W).astype(np.float32)
    col = (t % GRID_W).astype(np.float32)
    quarter = dim // 4
    omega = (1.0 / (POS_BASE ** (np.arange(quarter, dtype=np.float32) / quarter))).astype(np.float32)
    ar = r[:, None] * omega
    ac = col[:, None] * omega
    return np.concatenate([np.sin(ar), np.cos(ar), np.sin(ac), np.cos(ac)], axis=-1).astype(np.float32)


def _pick_tile(n, pref):
    return pref if n % pref == 0 else n


def kernel(x, c, ctx, c_ctx, w_mod, b_mod, norm1_g, norm2_g, w_in, w_four, ssm_a_re, ssm_a_im, ssm_log_dt, ssm_b_re, ssm_b_im, ssm_c_re, ssm_c_im, ssm_d, w_glu_a, w_glu_b, w_out, ffn_w_gate, ffn_w_up, ffn_w_down, moe_w_router, moe_b_router, moe_w_gate, moe_w_up, moe_w_down, final_g):
    b, l, d = x.shape
    lc = ctx.shape[1]
    depth = w_mod.shape[0]
    wf_cols = w_four.shape[1]
    ws_cols = ssm_d.shape[1]
    off_ga = wf_cols + ws_cols
    tm_lat = _pick_tile(l, 512)
    tm_ctx = _pick_tile(lc, 256)

    n_cond = b + 8
    cond = jnp.concatenate([c, jnp.broadcast_to(c_ctx[None], (n_cond - b, d))], axis=0)
    pos = jnp.asarray(_pos_table(l, d))

    mods = _mods_call(cond, w_mod, b_mod)
    w_in_b, wf, wa, wb, wo = (w.astype(BF16) for w in (w_in, w_four, w_glu_a, w_glu_b, w_out))
    n1g, n2g = norm1_g.reshape(depth, 1, d), norm2_g.reshape(depth, 1, d)
    d_skip = ssm_d.reshape(depth, 1, ws_cols)
    tables = jax.vmap(_s5_tables)(ssm_a_re, ssm_a_im, ssm_log_dt, ssm_b_re, ssm_b_im, ssm_c_re, ssm_c_im)

    def mod_rows(layer, which):
        return (lambda i: (layer * n_cond + i) * N_MODS + which,
                lambda i: (layer * n_cond + b) * N_MODS + which)

    h = x
    z = ctx
    for layer in range(depth):
        need_ctx = layer < depth - 1
        sh1, sc1, g1, sh2, sc2, g2 = (mod_rows(layer, k) for k in range(N_MODS))
        widths = (wf_cols, ws_cols, d, d)
        acts = (False, False, True, True)

        outs = _inproj_call(h, pos if layer == 0 else None, mods, sc1[0], sh1[0], n1g, w_in_b, layer, 0,
                            widths, acts, tm_lat)
        if layer == 0:
            h, outs = outs[0], outs[1:]
        pf, ps, ga, gb = outs
        if need_ctx:
            cpf, cps, cga, cgb = _inproj_call(z, None, mods, sc1[1], sh1[1], n1g, w_in_b, layer, 0,
                                              widths, acts, tm_ctx)
        else:
            (cps,) = _inproj_call(z, None, mods, sc1[1], sh1[1], n1g, w_in_b, layer, wf_cols,
                                  (ws_cols,), (False,), tm_ctx)

        ys_all = _s5_call(cps, ps, d_skip, tables, layer)
        yf = _fourier_call(pf, tm_lat)

        moe = layer % 2 == 1
        idx = layer // 2
        router = None
        if moe:
            wr = jnp.pad(moe_w_router[idx], ((0, 0), (0, ROUTER_PAD - N_EXPERTS)))
            br = jnp.pad(moe_b_router[idx][None], ((0, 0), (0, ROUTER_PAD - N_EXPERTS)), constant_values=NEG_BIG)
            router = (wr, br)
        res = _merge_call(h, yf, ys_all, 0, ga, gb, mods, g1[0], sc2[0], sh2[0], n2g, wf, wa, wb, wo, layer,
                          router, tm_lat)
        last = layer == depth - 1
        fin = final_g if last else None
        t = b * l
        if moe:
            h1, f, logits = res
            tm_moe = min(1024, max(128, t // 8))
            n_tiles = 2 * t // tm_moe + N_EXPERTS
            rec, counts = _route_call(logits.reshape(t, ROUTER_PAD), _pick_tile(t, 512))
            pos, tile_expert, n_used = _moe_plan(rec, counts, tm_moe, n_tiles)
            xs = _dispatch_call(pos, f.reshape(t, d), n_tiles * tm_moe, _pick_tile(t, 512))
            ys = _gffn_call(tile_expert, n_used, xs, moe_w_gate, moe_w_up, moe_w_down, idx,
                            tm_moe, _pick_tile(moe_w_gate.shape[-1], FFN_TILE))
            tm_c = _pick_tile(l, 512)
            h = _combine_call(pos, ys, h1.reshape(t, d), rec, mods, g2[0], l // tm_c, fin, tm_c).reshape(b, l, d)
        else:
            h1, f = res
            ffn_w = (ffn_w_gate, ffn_w_up, ffn_w_down, idx)
            tf = _pick_tile(ffn_w_gate.shape[-1], 256)
            tm_f = _pick_tile(l, 1024)
            h = _ffn_call(f.reshape(t, d), *ffn_w, h1.reshape(t, d), mods, g2[0], l // tm_f, fin,
                          tm_f, tf).reshape(b, l, d)

        if need_ctx:
            if moe:
                raise NotImplementedError("context tokens through an expert layer")
            cyf = _fourier_call(cpf, tm_ctx)
            z1, cf = _merge_call(z, cyf, ys_all, l, cga, cgb, mods, g1[1], sc2[1], sh2[1], n2g, wf, wa, wb, wo,
                                 layer, None, tm_ctx)
            z = _ffn_call(cf.reshape(b * lc, d), *ffn_w, z1.reshape(b * lc, d), mods, g2[1], 1, None,
                          _pick_tile(b * lc, 1024), tf).reshape(b, lc, d)
    return h
```

```python
import functools
import math

import numpy as np
import jax
import jax.numpy as jnp
from jax import lax
from jax.experimental import pallas as pl
from jax.experimental.pallas import tpu as pltpu

F32 = jnp.float32
BF16 = jnp.bfloat16

GRID_W = 64
N_FOURIER_GROUPS = 4
FOURIER_GROUP = 128
SSM_GROUP = 16
SSM_STATE = 64
N_EXPERTS = 8
EPS = 1e-6
POS_BASE = 10000.0

CHUNK = 16
LANES = 128
GPT = LANES // SSM_GROUP
N_COMP = 4
FFN_TILE = 512
ROUTER_PAD = LANES
NEG_BIG = -1e30
VMEM_LIMIT = 56 * 1024 * 1024


def _cparams(sem):
    return pltpu.CompilerParams(dimension_semantics=sem, vmem_limit_bytes=VMEM_LIMIT)


def _dot(a, b):
    return jnp.dot(a, b, preferred_element_type=F32)


def _split_bf16(a):
    hi = a.astype(BF16)
    lo = (a - hi.astype(F32)).astype(BF16)
    return hi, lo


def _dot3(a, b):
    ah, al = _split_bf16(a)
    bh, bl = _split_bf16(b)
    return _dot(ah, bh) + (_dot(ah, bl) + _dot(al, bh))


def _rms_mod(h, g, sc, sh):
    y = h * lax.rsqrt(jnp.mean(h * h, axis=-1, keepdims=True) + EPS)
    return (y * g) * (1.0 + sc) + sh


def _mods_kernel(c_ref, w_ref, b_ref, o_ref):
    s = jax.nn.silu(c_ref[...])
    o_ref[...] = _dot3(s, w_ref[...]) + b_ref[...]


N_MODS = 6
MOD_SHIFT1, MOD_SCALE1, MOD_GATE1, MOD_SHIFT2, MOD_SCALE2, MOD_GATE2 = range(N_MODS)


def _mods_call(cond, w_mod, b_mod):
    r, d = cond.shape
    depth, _, n = w_mod.shape
    tn = 1024
    out = pl.pallas_call(
        _mods_kernel,
        grid=(depth, n // tn),
        in_specs=[pl.BlockSpec((r, d), lambda i, j: (0, 0)),
                  pl.BlockSpec((None, d, tn), lambda i, j: (i, 0, j)),
                  pl.BlockSpec((None, 1, tn), lambda i, j: (i, 0, j))],
        out_specs=pl.BlockSpec((None, r, tn), lambda i, j: (i, 0, j)),
        out_shape=jax.ShapeDtypeStruct((depth, r, n), F32),
        compiler_params=_cparams(("arbitrary", "arbitrary")),
        name="mods",
    )(cond, w_mod, b_mod.reshape(depth, 1, n))
    return out.reshape(depth * r * N_MODS, 1, d)


def _inproj_kernel(*refs, acts, has_pos):
    it = iter(refs)
    x_ref = next(it)
    pos_ref = next(it) if has_pos else None
    sc_ref, sh_ref, g_ref, w_ref = next(it), next(it), next(it), next(it)
    h_out = next(it) if has_pos else None
    outs = list(it)
    h = x_ref[...]
    if has_pos:
        h = h + pos_ref[...]
        h_out[...] = h
    a = _rms_mod(h, g_ref[...], sc_ref[...], sh_ref[...]).astype(BF16)
    col = 0
    for o, act in zip(outs, acts):
        width = o.shape[-1]
        p = _dot(a, w_ref[:, col:col + width])
        if act:
            p = jax.nn.sigmoid(p)
        o[...] = p.astype(o.dtype)
        col += width


def _inproj_call(x, pos, mods, sc_row, sh_row, g, w, layer, col0, widths, acts, tm):
    b, l, d = x.shape
    has_pos = pos is not None
    n_cols = sum(widths)
    col_blk = col0 // n_cols
    assert col_blk * n_cols == col0
    tok = pl.BlockSpec((None, tm, d), lambda i, m: (i, m, 0))
    vec = lambda row: pl.BlockSpec((None, 1, d), lambda i, m: (row(i), 0, 0))
    in_specs = [tok]
    args = [x]
    if has_pos:
        in_specs.append(pl.BlockSpec((tm, d), lambda i, m: (m, 0)))
        args.append(pos)
    in_specs += [vec(sc_row), vec(sh_row), pl.BlockSpec((None, 1, d), lambda i, m: (layer, 0, 0)),
                 pl.BlockSpec((None, d, n_cols), lambda i, m: (layer, 0, col_blk))]
    args += [mods, mods, g, w]
    out_specs, out_shape = [], []
    if has_pos:
        out_specs.append(tok)
        out_shape.append(jax.ShapeDtypeStruct((b, l, d), F32))
    for width in widths:
        out_specs.append(pl.BlockSpec((None, tm, width), lambda i, m: (i, m, 0)))
        out_shape.append(jax.ShapeDtypeStruct((b, l, width), BF16))
    return pl.pallas_call(
        functools.partial(_inproj_kernel, acts=tuple(acts), has_pos=has_pos),
        grid=(b, l // tm),
        in_specs=in_specs, out_specs=out_specs, out_shape=out_shape,
        compiler_params=_cparams(("parallel", "parallel")),
        name="inproj",
    )(*args)


DFT_PAD_ROWS = 16


def _dft_tables(l):
    k = np.arange(l // 2 + DFT_PAD_ROWS, dtype=np.int64)
    n = np.arange(l, dtype=np.int64)
    ang = 2.0 * np.pi * ((k[:, None] * n[None, :]) % l).astype(np.float64) / l
    dl = np.concatenate([np.cos(ang), np.sin(ang)], axis=1).astype(np.float32)
    c = np.arange(FOURIER_GROUP, dtype=np.int64)
    angc = 2.0 * np.pi * ((c[:, None] * c[None, :]) % FOURIER_GROUP).astype(np.float64) / FOURIER_GROUP
    return dl, np.cos(angc).astype(np.float32), np.sin(angc).astype(np.float32)


def _fourier_kernel(u_ref, cc_ref, cs_ref, dl_ref, nxt_ref, o_ref, v_ref, *, l, scale):
    m = pl.program_id(1)
    tk = dl_ref.shape[0]

    @pl.when(m == 0)
    def _():
        for j in range(N_FOURIER_GROUPS):
            cols = slice(j * FOURIER_GROUP, (j + 1) * FOURIER_GROUP)
            uj = u_ref[:, cols]
            v_ref[0:l, cols] = _dot(uj, cc_ref[...]).astype(BF16)
            v_ref[l:2 * l, cols] = _dot(uj, cs_ref[...]).astype(BF16)

    p = _dot(dl_ref[:, :l], v_ref[0:l, :])
    q = _dot(dl_ref[:, l:], v_ref[l:2 * l, :])
    o_ref[pl.ds(pl.multiple_of(m * tk, tk), tk), :] = ((p - q) * scale).astype(o_ref.dtype)
    row = lax.broadcasted_iota(jnp.int32, (tk, tk), 0)
    col = lax.broadcasted_iota(jnp.int32, (tk, tk), 1)
    flip = (row + col == tk).astype(BF16)
    mirrored = _dot(flip, ((p + q) * scale).astype(BF16))
    nxt = _dot(nxt_ref[:, :l], v_ref[0:l, :]) + _dot(nxt_ref[:, l:], v_ref[l:2 * l, :])
    first = lax.broadcasted_iota(jnp.int32, mirrored.shape, 0) == 0
    upper = jnp.where(first, nxt[0:1] * scale, mirrored)
    o_ref[pl.ds(pl.multiple_of(l - (m + 1) * tk, tk), tk), :] = upper.astype(o_ref.dtype)


def _fourier_call(u, tm):
    b, l, w = u.shape
    dl, cc, cs = _dft_tables(l)
    dl = jnp.asarray(dl).astype(BF16)
    cc = jnp.asarray(cc).astype(BF16)
    cs = jnp.asarray(cs).astype(BF16)
    scale = 1.0 / math.sqrt(l * FOURIER_GROUP)
    tk = min(tm, l // 2)
    nxt_blocks = tk // DFT_PAD_ROWS
    return pl.pallas_call(
        functools.partial(_fourier_kernel, l=l, scale=scale),
        grid=(b, l // 2 // tk),
        in_specs=[pl.BlockSpec((None, l, w), lambda i, m: (i, 0, 0)),
                  pl.BlockSpec((FOURIER_GROUP, FOURIER_GROUP), lambda i, m: (0, 0)),
                  pl.BlockSpec((FOURIER_GROUP, FOURIER_GROUP), lambda i, m: (0, 0)),
                  pl.BlockSpec((tk, 2 * l), lambda i, m: (m, 0)),
                  pl.BlockSpec((DFT_PAD_ROWS, 2 * l), lambda i, m: ((m + 1) * nxt_blocks, 0))],
        out_specs=pl.BlockSpec((None, l, w), lambda i, m: (i, 0, 0)),
        out_shape=jax.ShapeDtypeStruct((b, l, w), BF16),
        scratch_shapes=[pltpu.VMEM((2 * l, w), BF16)],
        compiler_params=_cparams(("parallel", "arbitrary")),
        name="fourier",
    )(u, cc, cs, dl, dl)


def _expand_matrix(n_outer, inner):
    r = np.arange(n_outer * inner)[:, None]
    c = np.arange(n_outer * GPT * inner)[None, :]
    same = (r // inner == c // (GPT * inner)) & (r % inner == c % inner)
    return jnp.asarray(same.astype(np.float32)).astype(BF16)


def _spread_groups(x, expand, row_inner, col_inner):
    y = jnp.einsum('jrc,cn->jrn', x, expand, preferred_element_type=BF16)
    r = lax.broadcasted_iota(jnp.int32, y.shape, 1)
    c = lax.broadcasted_iota(jnp.int32, y.shape, 2)
    return jnp.where((r // row_inner) % GPT == (c // col_inner) % GPT, y, jnp.zeros_like(y))


def _s5_tables(a_re, a_im, log_dt, b_re, b_im, c_re, c_im):
    hp = lax.Precision.HIGHEST
    q = CHUNK
    g = a_re.shape[1]
    dt = jnp.exp(log_dt)[..., None]
    lr, li = a_re * dt, a_im * dt
    em1_r = jnp.expm1(lr) * jnp.cos(li) - 2.0 * jnp.sin(0.5 * li) ** 2
    em1_i = jnp.exp(lr) * jnp.sin(li)
    den = a_re * a_re + a_im * a_im
    fr = (em1_r * a_re + em1_i * a_im) / den
    fi = (em1_i * a_re - em1_r * a_im) / den
    bbr = fr[..., None] * b_re - fi[..., None] * b_im
    bbi = fr[..., None] * b_im + fi[..., None] * b_re
    k = jnp.arange(q + 1, dtype=F32)
    mag = jnp.exp(lr[..., None] * k)
    pr = mag * jnp.cos(li[..., None] * k)
    pi = mag * jnp.sin(li[..., None] * k)

    prk = jnp.moveaxis(pr, -1, 2)[..., None]
    pik = jnp.moveaxis(pi, -1, 2)[..., None]
    wr = prk * bbr[:, :, None] - pik * bbi[:, :, None]
    wi = prk * bbi[:, :, None] + pik * bbr[:, :, None]
    kern = (jnp.einsum('dgop,dgtph->dgtoh', c_re, wr[:, :, :q], precision=hp)
            - jnp.einsum('dgop,dgtph->dgtoh', c_im, wi[:, :, :q], precision=hp))
    qi = jnp.arange(q)
    nt = g // GPT
    kf = kern[0].transpose(0, 1, 3, 2)
    kb = kern[1].transpose(0, 1, 3, 2)
    lag = jnp.concatenate([kb[:, :0:-1], kf[:, :1] + kb[:, :1], kf[:, 1:]], axis=1).astype(BF16)
    lag = (lag.reshape(nt, GPT, 2 * q - 1, SSM_GROUP, SSM_GROUP).transpose(0, 2, 1, 3, 4)
           .reshape(nt * (2 * q - 1), LANES, SSM_GROUP))
    lag_tile = _spread_groups(lag, _expand_matrix(1, SSM_GROUP), SSM_GROUP, SSM_GROUP)
    lag_tile = lag_tile.reshape(nt, 2 * q - 1, LANES, LANES)

    def inject(w):
        wf = w[0][:, q - 1 - qi]
        wb = w[1][:, qi]
        both = jnp.stack([wf, wb])
        return both.transpose(0, 1, 2, 4, 3).reshape(2, g, q * SSM_GROUP, SSM_STATE)

    wre, wim = inject(wr), inject(wi)

    def carry_out(sign):
        outs = []
        for d, idx in ((0, qi + 1), (1, q - qi)):
            ppr = pr[d][:, :, idx]
            ppi = pi[d][:, :, idx]
            cr = c_re[d].transpose(0, 2, 1)[:, :, None, :]
            ci = c_im[d].transpose(0, 2, 1)[:, :, None, :]
            if sign > 0:
                val = cr * ppr[..., None] - ci * ppi[..., None]
            else:
                val = -(cr * ppi[..., None] + ci * ppr[..., None])
            outs.append(val.reshape(g, SSM_STATE, q * SSM_GROUP))
        return jnp.stack(outs)

    cre, cim = carry_out(+1), carry_out(-1)

    qh = q * SSM_GROUP
    wc = jnp.stack([wre[0], wim[0], wre[1], wim[1]]).astype(BF16)
    wc = (wc.reshape(N_COMP, nt, GPT, q, SSM_GROUP, SSM_STATE).transpose(1, 3, 2, 4, 0, 5)
          .reshape(nt, q * LANES, N_COMP * SSM_STATE))
    cc = jnp.stack([cre[0], cim[0], cre[1], cim[1]]).astype(BF16)
    cc = (cc.reshape(N_COMP, nt, GPT, SSM_STATE, qh).transpose(1, 0, 2, 3, 4)
          .reshape(nt, N_COMP * GPT * SSM_STATE, qh))
    decay = jnp.stack([pr[0, ..., q], pi[0, ..., q], pr[1, ..., q], pi[1, ..., q]])
    decay = decay.reshape(N_COMP, nt, 1, GPT * SSM_STATE).transpose(1, 0, 2, 3)
    return lag_tile, wc, cc, decay


def _chunk_rows(u_ref):
    nc, _, nb, lanes = u_ref.shape
    return jnp.concatenate([u_ref[:, qq].reshape(nc * nb, lanes) for qq in range(CHUNK)], axis=-1)


SPREAD_COLS = 512


def _spread_into(dst_ref, compact_ref, expand_ref, row_inner, col_inner):
    rows, cols = dst_ref.shape
    rsh, csh = row_inner.bit_length() - 1, col_inner.bit_length() - 1
    assert 1 << rsh == row_inner and 1 << csh == col_inner
    r = lax.broadcasted_iota(jnp.int32, (rows, SPREAD_COLS), 0)
    c = lax.broadcasted_iota(jnp.int32, (rows, SPREAD_COLS), 1)
    r_gi = lax.shift_right_logical(r, rsh) & (GPT - 1)
    for c0 in range(0, cols, SPREAD_COLS):
        y = _dot(compact_ref[...], expand_ref[:, c0:c0 + SPREAD_COLS])
        c_gi = lax.shift_right_logical(c + c0, csh) & (GPT - 1)
        dst_ref[:, c0:c0 + SPREAD_COLS] = jnp.where(r_gi == c_gi, y, 0.0).astype(dst_ref.dtype)


def _s5_inject_kernel(u_ref, wc_ref, exp_ref, s_ref, w_ref):
    @pl.when(pl.program_id(1) == 0)
    def _():
        _spread_into(w_ref, wc_ref, exp_ref, SSM_GROUP, SSM_STATE)

    s = _dot(_chunk_rows(u_ref), w_ref[...])
    width = s_ref.shape[-1]
    for comp in range(N_COMP):
        s_ref[comp] = s[:, comp * width:(comp + 1) * width]


def _s5_scan_kernel(s_ref, a_ref, o_ref, *, batch, n_lat_chunks, n_ctx_chunks):
    backward = pl.program_id(1) == 1
    ar, ai = a_ref[0], a_ref[1]

    def step(i, carry):
        sr, si = carry
        k = i - n_ctx_chunks
        fwd = jnp.where(i < n_ctx_chunks, n_lat_chunks + i, k)
        bwd = jnp.where(i < n_ctx_chunks, n_lat_chunks + n_ctx_chunks - 1 - i, n_lat_chunks - 1 - k)
        rows = pl.ds(pl.multiple_of(jnp.where(backward, bwd, fwd) * batch, batch), batch)
        o_ref[0, rows, :] = sr.astype(o_ref.dtype)
        o_ref[1, rows, :] = si.astype(o_ref.dtype)
        return (ar * sr - ai * si + s_ref[0, rows, :], ar * si + ai * sr + s_ref[1, rows, :])

    z = jnp.zeros((batch, s_ref.shape[-1]), F32)
    lax.fori_loop(0, n_lat_chunks + n_ctx_chunks, step, (z, z))


def _s5_out_kernel(u_ref, sp_ref, lag_ref, cc_ref, exp_ref, d_ref, o_ref, m_ref, c_ref):
    nc, _, nb, lanes = u_ref.shape
    width = sp_ref.shape[-1]

    @pl.when(pl.program_id(1) == 0)
    def _():
        for q_in in range(CHUNK):
            for q_out in range(CHUNK):
                m_ref[q_in * lanes:(q_in + 1) * lanes, q_out * lanes:(q_out + 1) * lanes] = (
                    lag_ref[q_out - q_in + CHUNK - 1])
        _spread_into(c_ref, cc_ref, exp_ref, SSM_STATE, SSM_GROUP)

    y = _dot(_chunk_rows(u_ref), m_ref[...])
    for comp in range(N_COMP):
        y = y + _dot(sp_ref[comp], c_ref[comp * width:(comp + 1) * width, :])
    for qq in range(CHUNK):
        yq = y[:, qq * lanes:(qq + 1) * lanes].reshape(nc, nb, lanes)
        o_ref[:, qq] = jax.nn.gelu(d_ref[...] * u_ref[:, qq].astype(F32) + yq).astype(o_ref.dtype)


def _s5_call(u_ctx, u_lat, d_skip, tables, layer):
    lag_tile, wc, cc, decay = tables
    exp_w = _expand_matrix(N_COMP, SSM_STATE)
    exp_c = _expand_matrix(CHUNK, SSM_GROUP)
    op_rows = CHUNK * LANES
    full2 = lambda a: pl.BlockSpec(a.shape, lambda j, r: (0, 0))
    b, lc, w = u_ctx.shape
    l = u_lat.shape[1]
    nt = w // LANES
    n_lat, n_ctx = l // CHUNK, lc // CHUNK
    n_chunks = n_lat + n_ctx
    rows = n_chunks * b
    sw = GPT * SSM_STATE
    u_t = jnp.concatenate([u_lat, u_ctx], axis=1).transpose(1, 0, 2).reshape(n_chunks, CHUNK, b, w)
    cb = _pick_tile(n_chunks, 36)
    rb = cb * b
    u_spec = pl.BlockSpec((cb, CHUNK, b, LANES), lambda j, r: (r, 0, 0, j))
    op_spec = lambda shape: pl.BlockSpec((None, None) + shape, lambda j, r: (layer, j, 0, 0))
    s_loc = pl.pallas_call(
        _s5_inject_kernel,
        grid=(nt, n_chunks // cb),
        in_specs=[u_spec, op_spec(wc.shape[2:]), full2(exp_w)],
        out_specs=pl.BlockSpec((None, N_COMP, rb, sw), lambda j, r: (j, 0, r, 0)),
        out_shape=jax.ShapeDtypeStruct((nt, N_COMP, rows, sw), F32),
        scratch_shapes=[pltpu.VMEM((op_rows, N_COMP * sw), BF16)],
        compiler_params=_cparams(("parallel", "arbitrary")),
        name="s5_inject",
    )(u_t, wc, exp_w)
    s_prev = pl.pallas_call(
        functools.partial(_s5_scan_kernel, batch=b, n_lat_chunks=n_lat, n_ctx_chunks=n_ctx),
        grid=(nt, 2),
        in_specs=[pl.BlockSpec((None, 2, rows, sw), lambda j, d: (j, d, 0, 0)),
                  pl.BlockSpec((None, None, 2, 1, sw), lambda j, d: (layer, j, d, 0, 0))],
        out_specs=pl.BlockSpec((None, 2, rows, sw), lambda j, d: (j, d, 0, 0)),
        out_shape=jax.ShapeDtypeStruct((nt, N_COMP, rows, sw), BF16),
        compiler_params=_cparams(("parallel", "parallel")),
        name="s5_scan",
    )(s_loc, decay)
    y = pl.pallas_call(
        _s5_out_kernel,
        grid=(nt, n_chunks // cb),
        in_specs=[u_spec,
                  pl.BlockSpec((None, N_COMP, rb, sw), lambda j, r: (j, 0, r, 0)),
                  pl.BlockSpec((None, None) + lag_tile.shape[2:], lambda j, r: (layer, j, 0, 0, 0)),
                  op_spec(cc.shape[2:]), full2(exp_c),
                  pl.BlockSpec((None, 1, LANES), lambda j, r: (layer, 0, j))],
        out_specs=u_spec,
        out_shape=jax.ShapeDtypeStruct((n_chunks, CHUNK, b, w), BF16),
        scratch_shapes=[pltpu.VMEM((op_rows, op_rows), BF16), pltpu.VMEM((N_COMP * sw, op_rows), BF16)],
        compiler_params=_cparams(("parallel", "arbitrary")),
        name="s5_out",
    )(u_t, s_prev, lag_tile, cc, exp_c, d_skip)
    return y.reshape(l + lc, b, w).transpose(1, 0, 2)


def _merge_kernel(*refs, with_router):
    (h_ref, yf_ref, ys_ref, ga_ref, gb_ref, g1_ref, sc2_ref, sh2_ref, n2g_ref,
     wf_ref, wa_ref, wb_ref, wo_ref) = refs[:13]
    rest = refs[13:]
    if with_router:
        wr_ref, br_ref, h1_ref, f_ref, lg_ref = rest
    else:
        h1_ref, f_ref = rest
    ys = ys_ref[...]
    ya = _dot(yf_ref[...], wf_ref[...])
    yb = _dot(ys, wa_ref[...]) * jax.nn.sigmoid(_dot(ys, wb_ref[...]))
    m = (ga_ref[...].astype(F32) * ya + gb_ref[...].astype(F32) * yb).astype(BF16)
    h1 = h_ref[...] + g1_ref[...] * _dot(m, wo_ref[...])
    h1_ref[...] = h1
    f = _rms_mod(h1, n2g_ref[...], sc2_ref[...], sh2_ref[...])
    f_ref[...] = f.astype(f_ref.dtype)
    if with_router:
        lg_ref[...] = _dot3(f, wr_ref[...]) + br_ref[...]


def _merge_call(h, yf, ys_all, ys_row0, ga, gb, mods, g1_row, sc2_row, sh2_row, n2g, wf, wa, wb, wo, layer,
                router, tm):
    b, l, d = h.shape
    w = yf.shape[-1]
    tok = lambda width: pl.BlockSpec((None, tm, width), lambda i, m: (i, m, 0))
    vec = lambda row: pl.BlockSpec((None, 1, d), lambda i, m: (row(i), 0, 0))
    full = lambda a: pl.BlockSpec(a.shape, lambda i, m: (0,) * a.ndim)
    per_layer = lambda a: pl.BlockSpec((None,) + a.shape[1:], lambda i, m: (layer,) + (0,) * (a.ndim - 1))
    blk0 = ys_row0 // tm
    assert blk0 * tm == ys_row0
    args = [h, yf, ys_all, ga, gb, mods, mods, mods, n2g, wf, wa, wb, wo]
    in_specs = [tok(d), tok(w), pl.BlockSpec((None, tm, w), lambda i, m: (i, m + blk0, 0)), tok(d), tok(d),
                vec(g1_row), vec(sc2_row), vec(sh2_row), per_layer(n2g), per_layer(wf), per_layer(wa),
                per_layer(wb), per_layer(wo)]
    out_specs = [tok(d), tok(d)]
    f_dtype = F32 if router is not None else BF16
    out_shape = [jax.ShapeDtypeStruct((b, l, d), F32), jax.ShapeDtypeStruct((b, l, d), f_dtype)]
    if router is not None:
        wr, br = router
        args += [wr, br]
        in_specs += [full(wr), full(br)]
        out_specs.append(tok(ROUTER_PAD))
        out_shape.append(jax.ShapeDtypeStruct((b, l, ROUTER_PAD), F32))
    return pl.pallas_call(
        functools.partial(_merge_kernel, with_router=router is not None),
        grid=(b, l // tm),
        in_specs=in_specs, out_specs=out_specs, out_shape=out_shape,
        compiler_params=_cparams(("parallel", "parallel")),
        name="merge",
    )(*args)


R_E1, R_E2, R_RANK1, R_RANK2, R_W1, R_W2 = range(6)


def _route_kernel(lg_ref, rec_ref, cnt_ref, carry_ref):
    @pl.when(pl.program_id(0) == 0)
    def _():
        carry_ref[...] = jnp.zeros_like(carry_ref)

    lg = lg_ref[...]
    tm = lg.shape[0]
    lane = lax.broadcasted_iota(jnp.int32, lg.shape, 1)
    m1 = jnp.max(lg, axis=-1, keepdims=True)
    i1 = jnp.min(jnp.where(lg == m1, lane, ROUTER_PAD), axis=-1, keepdims=True)
    lg2 = jnp.where(lane == i1, -jnp.inf, lg)
    m2 = jnp.max(lg2, axis=-1, keepdims=True)
    i2 = jnp.min(jnp.where(lg2 == m2, lane, ROUTER_PAD), axis=-1, keepdims=True)
    e = jnp.exp(m2 - m1)
    w1 = 1.0 / (1.0 + e)
    w2 = e * w1

    oh1 = lane == i1
    oh2 = lane == i2
    row = lax.broadcasted_iota(jnp.int32, (tm, tm), 0)
    col = lax.broadcasted_iota(jnp.int32, (tm, tm), 1)
    below = (row > col).astype(BF16)
    p1 = _dot(below, oh1.astype(BF16))
    p2 = _dot(below, oh2.astype(BF16))
    c1 = jnp.sum(oh1.astype(F32), axis=0, keepdims=True)
    c2 = jnp.sum(oh2.astype(F32), axis=0, keepdims=True)
    base = carry_ref[...]
    r1 = jnp.sum(jnp.where(oh1, p1 + base, 0.0), axis=-1, keepdims=True)
    r2 = jnp.sum(jnp.where(oh2, p2 + (base + c1), 0.0), axis=-1, keepdims=True)
    total = base + c1 + c2
    carry_ref[...] = total
    cnt_ref[...] = total

    rec = jnp.zeros_like(lg)
    for slot, val in ((R_E1, i1.astype(F32)), (R_E2, i2.astype(F32)), (R_RANK1, r1), (R_RANK2, r2),
                      (R_W1, w1), (R_W2, w2)):
        rec = jnp.where(lane == slot, val, rec)
    rec_ref[...] = rec


def _route_call(logits, tm):
    t, n = logits.shape
    return pl.pallas_call(
        _route_kernel,
        grid=(t // tm,),
        in_specs=[pl.BlockSpec((tm, n), lambda i: (i, 0))],
        out_specs=[pl.BlockSpec((tm, n), lambda i: (i, 0)), pl.BlockSpec((1, n), lambda i: (0, 0))],
        out_shape=[jax.ShapeDtypeStruct((t, n), F32), jax.ShapeDtypeStruct((1, n), F32)],
        scratch_shapes=[pltpu.VMEM((1, n), F32)],
        compiler_params=_cparams(("arbitrary",)),
        name="route",
    )(logits)


def _dispatch_kernel(pos_ref, fill_ref, f_ref, xs_ref, zero_ref, sem, zsem, *, tile_rows):
    tm = f_ref.shape[0]
    base = pl.program_id(0) * (2 * tm)

    def row_copy(r, k):
        return pltpu.make_async_copy(f_ref.at[pl.ds(r, 1)], xs_ref.at[pl.ds(pos_ref[base + 2 * r + k], 1)], sem)

    def issue(r, carry):
        row_copy(r, 0).start()
        row_copy(r, 1).start()
        return carry

    lax.fori_loop(0, tm, issue, 0, unroll=8)
    for _ in range(2):
        pltpu.make_async_copy(f_ref, xs_ref.at[pl.ds(0, tm)], sem).wait()

    @pl.when(pl.program_id(0) == pl.num_programs(0) - 1)
    def _():
        zero_ref[...] = jnp.zeros_like(zero_ref)

        def zero_row(row):
            return pltpu.make_async_copy(zero_ref.at[pl.ds(0, 1)], xs_ref.at[pl.ds(row, 1)], zsem)

        def zero_tile(tile):
            rows = pl.ds(pl.multiple_of(tile * tile_rows, tile_rows), tile_rows)
            return pltpu.make_async_copy(zero_ref, xs_ref.at[rows], zsem)

        def start(copy_of):
            def body(k, carry):
                copy_of(k).start()
                return carry
            return body

        def wait(copy_of):
            def body(k, carry):
                copy_of(k).wait()
                return carry
            return body

        for e in range(N_EXPERTS):
            first, count = fill_ref[e], fill_ref[N_EXPERTS + e]
            lax.fori_loop(first, first + count, start(zero_row), 0)
            lax.fori_loop(first, first + count, wait(zero_row), 0)
        n_used, n_tiles = fill_ref[2 * N_EXPERTS], xs_ref.shape[0] // tile_rows
        lax.fori_loop(n_used, n_tiles, start(zero_tile), 0)
        lax.fori_loop(n_used, n_tiles, wait(zero_tile), 0)


def _dispatch_call(pos, fill, f, n_rows, tile_rows, tm):
    t, d = f.shape
    return pl.pallas_call(
        functools.partial(_dispatch_kernel, tile_rows=tile_rows),
        grid_spec=pltpu.PrefetchScalarGridSpec(
            num_scalar_prefetch=2,
            grid=(t // tm,),
            in_specs=[pl.BlockSpec((tm, d), lambda i, pos, fill: (i, 0))],
            out_specs=pl.BlockSpec(memory_space=pl.ANY),
            scratch_shapes=[pltpu.VMEM((tile_rows, d), f.dtype), pltpu.SemaphoreType.DMA,
                            pltpu.SemaphoreType.DMA]),
        out_shape=jax.ShapeDtypeStruct((n_rows, d), f.dtype),
        compiler_params=pltpu.CompilerParams(dimension_semantics=("arbitrary",), vmem_limit_bytes=VMEM_LIMIT,
                                             disable_bounds_checks=True),
        name="dispatch",
    )(pos, fill, f)


N_WBUF = 3


def _weight_ring(wg_hbm, wu_hbm, wd_hbm, wg_buf, wu_buf, wd_buf, sem, n_steps, weight_set):
    i, j = pl.program_id(0), pl.program_id(1)
    nj = pl.num_programs(1)
    tf = wg_buf.shape[-1]
    g = i * nj + j

    def copies(ci, cj, slot):
        cols = pl.ds(pl.multiple_of(cj * tf, tf), tf)
        s = weight_set(ci)
        return (pltpu.make_async_copy(wg_hbm.at[s, :, cols], wg_buf.at[slot], sem.at[0, slot]),
                pltpu.make_async_copy(wu_hbm.at[s, :, cols], wu_buf.at[slot], sem.at[1, slot]),
                pltpu.make_async_copy(wd_hbm.at[s, cols, :], wd_buf.at[slot], sem.at[2, slot]))

    def start_ahead(ahead):
        ci = lax.div(g + ahead, nj)
        cj = g + ahead - ci * nj
        ci = jnp.minimum(ci, pl.num_programs(0) - 1)

        @pl.when(g + ahead < n_steps)
        def _():
            for c in copies(ci, cj, lax.rem(g + ahead, N_WBUF)):
                c.start()

    @pl.when(g == 0)
    def _():
        for ahead in range(N_WBUF - 1):
            start_ahead(ahead)

    start_ahead(N_WBUF - 1)
    slot = lax.rem(g, N_WBUF)

    @pl.when(g < n_steps)
    def _():
        for c in copies(i, j, slot):
            c.wait()

    return slot


def _gffn_kernel(te_ref, nu_ref, x_ref, wg_hbm, wu_hbm, wd_hbm, o_ref, xb_ref, wg_buf, wu_buf, wd_buf, sem,
                 *, set0):
    j = pl.program_id(1)
    slot = _weight_ring(wg_hbm, wu_hbm, wd_hbm, wg_buf, wu_buf, wd_buf, sem,
                        nu_ref[0] * pl.num_programs(1), lambda ci: set0 + te_ref[ci])

    @pl.when(pl.program_id(0) < nu_ref[0])
    def _():
        @pl.when(j == 0)
        def _():
            xb_ref[...] = x_ref[...].astype(BF16)
            o_ref[...] = jnp.zeros_like(o_ref)

        x = xb_ref[...]
        hid = jax.nn.silu(_dot(x, wg_buf[slot].astype(BF16))) * _dot(x, wu_buf[slot].astype(BF16))
        o_ref[...] += _dot(hid.astype(BF16), wd_buf[slot].astype(BF16))


def _gffn_call(tile_expert, n_used, xs, wg, wu, wd, idx, tm, tf):
    n_rows, d = xs.shape
    n_e, f = wg.shape[1], wg.shape[-1]
    nj = f // tf
    wg, wu, wd = (w.reshape((-1,) + w.shape[2:]) for w in (wg, wu, wd))

    def row_map(i, j, te, nu):
        return jnp.minimum(i, nu[0] - 1), 0

    hbm = pl.BlockSpec(memory_space=pl.ANY)
    return pl.pallas_call(
        functools.partial(_gffn_kernel, set0=idx * n_e),
        grid_spec=pltpu.PrefetchScalarGridSpec(
            num_scalar_prefetch=2,
            grid=(n_rows // tm, nj),
            in_specs=[pl.BlockSpec((tm, d), row_map), hbm, hbm, hbm],
            out_specs=pl.BlockSpec((tm, d), row_map),
            scratch_shapes=[pltpu.VMEM((tm, d), BF16),
                            pltpu.VMEM((N_WBUF, d, tf), wg.dtype), pltpu.VMEM((N_WBUF, d, tf), wu.dtype),
                            pltpu.VMEM((N_WBUF, tf, d), wd.dtype), pltpu.SemaphoreType.DMA((3, N_WBUF))]),
        out_shape=jax.ShapeDtypeStruct((n_rows, d), F32),
        input_output_aliases={2: 0},
        compiler_params=_cparams(("arbitrary", "arbitrary")),
        name="gffn",
    )(tile_expert, n_used, xs, wg, wu, wd)


def _combine_kernel(pos_ref, ys_ref, h_ref, rec_ref, g2_ref, fg_ref, o_ref, ybuf_ref, sem, *, final_norm):
    tm = h_ref.shape[0]
    base = pl.program_id(0) * (2 * tm)

    def row_copy(r, k):
        return pltpu.make_async_copy(ys_ref.at[pl.ds(pos_ref[base + 2 * r + k], 1)],
                                     ybuf_ref.at[k, pl.ds(r, 1)], sem)

    def issue(r, carry):
        row_copy(r, 0).start()
        row_copy(r, 1).start()
        return carry

    lax.fori_loop(0, tm, issue, 0, unroll=8)
    for k in range(2):
        pltpu.make_async_copy(ys_ref.at[pl.ds(0, tm)], ybuf_ref.at[k], sem).wait()
    rec = rec_ref[...]
    y = rec[:, R_W1:R_W1 + 1] * ybuf_ref[0] + rec[:, R_W2:R_W2 + 1] * ybuf_ref[1]
    out = h_ref[...] + g2_ref[...] * y
    if final_norm:
        out = out * lax.rsqrt(jnp.mean(out * out, axis=-1, keepdims=True) + EPS) * fg_ref[...]
    o_ref[...] = out


def _combine_call(pos, ys, h, rec, mods, g2_row, tiles_per_batch, final_g, tm):
    t, d = h.shape
    fg = (final_g if final_g is not None else jnp.ones((d,), F32)).reshape(1, d)
    return pl.pallas_call(
        functools.partial(_combine_kernel, final_norm=final_g is not None),
        grid_spec=pltpu.PrefetchScalarGridSpec(
            num_scalar_prefetch=1,
            grid=(t // tm,),
            in_specs=[pl.BlockSpec(memory_space=pl.ANY),
                      pl.BlockSpec((tm, d), lambda i, pos: (i, 0)),
                      pl.BlockSpec((tm, ROUTER_PAD), lambda i, pos: (i, 0)),
                      pl.BlockSpec((None, 1, d), lambda i, pos: (g2_row(i // tiles_per_batch), 0, 0)),
                      pl.BlockSpec((1, d), lambda i, pos: (0, 0))],
            out_specs=pl.BlockSpec((tm, d), lambda i, pos: (i, 0)),
            scratch_shapes=[pltpu.VMEM((2, tm, d), F32), pltpu.SemaphoreType.DMA]),
        out_shape=jax.ShapeDtypeStruct((t, d), F32),
        compiler_params=pltpu.CompilerParams(dimension_semantics=("arbitrary",), vmem_limit_bytes=VMEM_LIMIT,
                                             disable_bounds_checks=True),
        name="combine",
    )(pos, ys, h, rec, mods, fg)


def _moe_plan(rec, counts, tm, n_tiles):
    cnt = counts[0, :N_EXPERTS].astype(jnp.int32)
    nt = (cnt + (tm - 1)) // tm
    cum = jnp.cumsum(nt)
    start = (cum - nt) * tm
    e = rec[:, R_E1:R_E2 + 1].astype(jnp.int32)
    rank = rec[:, R_RANK1:R_RANK2 + 1].astype(jnp.int32)
    ex = lax.broadcasted_iota(jnp.int32, e.shape + (N_EXPERTS,), 2)
    pos = rank + jnp.sum(jnp.where(e[..., None] == ex, start, 0), axis=-1)
    tile = jnp.arange(n_tiles, dtype=jnp.int32)
    te = jnp.sum((tile[:, None] >= cum[None, :]).astype(jnp.int32), axis=1)
    n_used = cum[-1:]
    last_e = jnp.sum((n_used - 1 >= cum).astype(jnp.int32))
    fill = jnp.concatenate([start + cnt, nt * tm - cnt, n_used])
    return pos.reshape(-1), jnp.minimum(te, last_e), n_used, fill


def _ffn_kernel(x_ref, wg_hbm, wu_hbm, wd_hbm, h_ref, g2_ref, fg_ref, o_ref, acc_ref, wg_buf, wu_buf, wd_buf, sem,
                *, final_norm, idx):
    j = pl.program_id(1)
    slot = _weight_ring(wg_hbm, wu_hbm, wd_hbm, wg_buf, wu_buf, wd_buf, sem,
                        pl.num_programs(0) * pl.num_programs(1), lambda ci: idx)

    @pl.when(j == 0)
    def _():
        acc_ref[...] = jnp.zeros_like(acc_ref)

    x = x_ref[...]
    hid = jax.nn.silu(_dot(x, wg_buf[slot].astype(BF16))) * _dot(x, wu_buf[slot].astype(BF16))
    acc_ref[...] += _dot(hid.astype(BF16), wd_buf[slot].astype(BF16))

    @pl.when(j == pl.num_programs(1) - 1)
    def _():
        out = h_ref[...] + g2_ref[...] * acc_ref[...]
        if final_norm:
            out = out * lax.rsqrt(jnp.mean(out * out, axis=-1, keepdims=True) + EPS) * fg_ref[...]
        o_ref[...] = out


def _ffn_call(x, wg, wu, wd, idx, h, mods, g2_row, tiles_per_batch, final_g, tm, tf):
    t, d = x.shape
    f = wg.shape[-1]
    tok = lambda width: pl.BlockSpec((tm, width), lambda m, j: (m, 0))
    fg = (final_g if final_g is not None else jnp.ones((d,), F32)).reshape(1, d)
    hbm = pl.BlockSpec(memory_space=pl.ANY)
    return pl.pallas_call(
        functools.partial(_ffn_kernel, final_norm=final_g is not None, idx=idx),
        grid=(t // tm, f // tf),
        in_specs=[tok(d), hbm, hbm, hbm,
                  tok(d),
                  pl.BlockSpec((None, 1, d), lambda m, j: (g2_row(m // tiles_per_batch), 0, 0)),
                  pl.BlockSpec((1, d), lambda m, j: (0, 0))],
        out_specs=tok(d),
        out_shape=jax.ShapeDtypeStruct((t, d), F32),
        scratch_shapes=[pltpu.VMEM((tm, d), F32),
                        pltpu.VMEM((N_WBUF, d, tf), wg.dtype), pltpu.VMEM((N_WBUF, d, tf), wu.dtype),
                        pltpu.VMEM((N_WBUF, tf, d), wd.dtype), pltpu.SemaphoreType.DMA((3, N_WBUF))],
        compiler_params=_cparams(("arbitrary", "arbitrary")),
        name="ffn",
    )(x, wg, wu, wd, h, mods, fg)


def _pos_table(n, dim):
    t = np.arange(n)
    r = (t // GRID_W).astype(np.float32)
    col = (t % GRID_W).astype(np.float32)
    quarter = dim // 4
    omega = (1.0 / (POS_BASE ** (np.arange(quarter, dtype=np.float32) / quarter))).astype(np.float32)
    ar = r[:, None] * omega
    ac = col[:, None] * omega
    return np.concatenate([np.sin(ar), np.cos(ar), np.sin(ac), np.cos(ac)], axis=-1).astype(np.float32)


def _pick_tile(n, pref):
    return pref if n % pref == 0 else n


def kernel(x, c, ctx, c_ctx, w_mod, b_mod, norm1_g, norm2_g, w_in, w_four, ssm_a_re, ssm_a_im, ssm_log_dt, ssm_b_re, ssm_b_im, ssm_c_re, ssm_c_im, ssm_d, w_glu_a, w_glu_b, w_out, ffn_w_gate, ffn_w_up, ffn_w_down, moe_w_router, moe_b_router, moe_w_gate, moe_w_up, moe_w_down, final_g):
    b, l, d = x.shape
    lc = ctx.shape[1]
    depth = w_mod.shape[0]
    wf_cols = w_four.shape[1]
    ws_cols = ssm_d.shape[1]
    off_ga = wf_cols + ws_cols
    tm_lat = _pick_tile(l, 512)
    tm_ctx = _pick_tile(lc, 256)

    n_cond = b + 8
    cond = jnp.concatenate([c, jnp.broadcast_to(c_ctx[None], (n_cond - b, d))], axis=0)
    pos = jnp.asarray(_pos_table(l, d))

    mods = _mods_call(cond, w_mod, b_mod)
    w_in_b, wf, wa, wb, wo = (w.astype(BF16) for w in (w_in, w_four, w_glu_a, w_glu_b, w_out))
    n1g, n2g = norm1_g.reshape(depth, 1, d), norm2_g.reshape(depth, 1, d)
    d_skip = ssm_d.reshape(depth, 1, ws_cols)
    tables = jax.vmap(_s5_tables)(ssm_a_re, ssm_a_im, ssm_log_dt, ssm_b_re, ssm_b_im, ssm_c_re, ssm_c_im)

    def mod_rows(layer, which):
        return (lambda i: (layer * n_cond + i) * N_MODS + which,
                lambda i: (layer * n_cond + b) * N_MODS + which)

    h = x
    z = ctx
    for layer in range(depth):
        need_ctx = layer < depth - 1
        sh1, sc1, g1, sh2, sc2, g2 = (mod_rows(layer, k) for k in range(N_MODS))
        widths = (wf_cols, ws_cols, d, d)
        acts = (False, False, True, True)

        outs = _inproj_call(h, pos if layer == 0 else None, mods, sc1[0], sh1[0], n1g, w_in_b, layer, 0,
                            widths, acts, tm_lat)
        if layer == 0:
            h, outs = outs[0], outs[1:]
        pf, ps, ga, gb = outs
        if need_ctx:
            cpf, cps, cga, cgb = _inproj_call(z, None, mods, sc1[1], sh1[1], n1g, w_in_b, layer, 0,
                                              widths, acts, tm_ctx)
        else:
            (cps,) = _inproj_call(z, None, mods, sc1[1], sh1[1], n1g, w_in_b, layer, wf_cols,
                                  (ws_cols,), (False,), tm_ctx)

        ys_all = _s5_call(cps, ps, d_skip, tables, layer)
        yf = _fourier_call(pf, tm_lat)

        moe = layer % 2 == 1
        idx = layer // 2
        router = None
        if moe:
            wr = jnp.pad(moe_w_router[idx], ((0, 0), (0, ROUTER_PAD - N_EXPERTS)))
            br = jnp.pad(moe_b_router[idx][None], ((0, 0), (0, ROUTER_PAD - N_EXPERTS)), constant_values=NEG_BIG)
            router = (wr, br)
        res = _merge_call(h, yf, ys_all, 0, ga, gb, mods, g1[0], sc2[0], sh2[0], n2g, wf, wa, wb, wo, layer,
                          router, tm_lat)
        last = layer == depth - 1
        fin = final_g if last else None
        t = b * l
        if moe:
            h1, f, logits = res
            tm_moe = min(1024, max(128, t // 8))
            n_tiles = 2 * t // tm_moe + N_EXPERTS
            rec, counts = _route_call(logits.reshape(t, ROUTER_PAD), _pick_tile(t, 512))
            pos, tile_expert, n_used, fill = _moe_plan(rec, counts, tm_moe, n_tiles)
            xs = _dispatch_call(pos, fill, f.reshape(t, d), n_tiles * tm_moe, tm_moe, _pick_tile(t, 512))
            ys = _gffn_call(tile_expert, n_used, xs, moe_w_gate, moe_w_up, moe_w_down, idx,
                            tm_moe, _pick_tile(moe_w_gate.shape[-1], FFN_TILE))
            tm_c = _pick_tile(l, 512)
            h = _combine_call(pos, ys, h1.reshape(t, d), rec, mods, g2[0], l // tm_c, fin, tm_c).reshape(b, l, d)
        else:
            h1, f = res
            ffn_w = (ffn_w_gate, ffn_w_up, ffn_w_down, idx)
            tf = _pick_tile(ffn_w_gate.shape[-1], 256)
            tm_f = _pick_tile(l, 1024)
            h = _ffn_call(f.reshape(t, d), *ffn_w, h1.reshape(t, d), mods, g2[0], l // tm_f, fin,
                          tm_f, tf).reshape(b, l, d)

        if need_ctx:
            if moe:
                raise NotImplementedError("context tokens through an expert layer")
            cyf = _fourier_call(cpf, tm_ctx)
            z1, cf = _merge_call(z, cyf, ys_all, l, cga, cgb, mods, g1[1], sc2[1], sh2[1], n2g, wf, wa, wb, wo,
                                 layer, None, tm_ctx)
            z = _ffn_call(cf.reshape(b * lc, d), *ffn_w, z1.reshape(b * lc, d), mods, g2[1], 1, None,
                          _pick_tile(b * lc, 1024), tf).reshape(b, lc, d)
    return h
```

```python
import functools
import math

import numpy as np
import jax
import jax.numpy as jnp
from jax import lax
from jax.experimental import pallas as pl
from jax.experimental.pallas import tpu as pltpu

F32 = jnp.float32
BF16 = jnp.bfloat16

GRID_W = 64
N_FOURIER_GROUPS = 4
FOURIER_GROUP = 128
SSM_GROUP = 16
SSM_STATE = 64
N_EXPERTS = 8
EPS = 1e-6
POS_BASE = 10000.0

CHUNK = 16
LANES = 128
GPT = LANES // SSM_GROUP
N_COMP = 4
FFN_TILE = 512
ROUTER_PAD = LANES
NEG_BIG = -1e30
VMEM_LIMIT = 56 * 1024 * 1024


def _cparams(sem):
    return pltpu.CompilerParams(dimension_semantics=sem, vmem_limit_bytes=VMEM_LIMIT)


def _dot(a, b):
    return jnp.dot(a, b, preferred_element_type=F32)


def _split_bf16(a):
    hi = a.astype(BF16)
    lo = (a - hi.astype(F32)).astype(BF16)
    return hi, lo


def _dot3(a, b):
    ah, al = _split_bf16(a)
    bh, bl = _split_bf16(b)
    return _dot(ah, bh) + (_dot(ah, bl) + _dot(al, bh))


def _rms_mod(h, g, sc, sh):
    y = h * lax.rsqrt(jnp.mean(h * h, axis=-1, keepdims=True) + EPS)
    return (y * g) * (1.0 + sc) + sh


def _mods_kernel(c_ref, w_ref, b_ref, o_ref):
    s = jax.nn.silu(c_ref[...])
    o_ref[...] = _dot3(s, w_ref[...]) + b_ref[...]


N_MODS = 6
MOD_SHIFT1, MOD_SCALE1, MOD_GATE1, MOD_SHIFT2, MOD_SCALE2, MOD_GATE2 = range(N_MODS)


def _mods_call(cond, w_mod, b_mod):
    r, d = cond.shape
    depth, _, n = w_mod.shape
    tn = 1024
    out = pl.pallas_call(
        _mods_kernel,
        grid=(depth, n // tn),
        in_specs=[pl.BlockSpec((r, d), lambda i, j: (0, 0)),
                  pl.BlockSpec((None, d, tn), lambda i, j: (i, 0, j)),
                  pl.BlockSpec((None, 1, tn), lambda i, j: (i, 0, j))],
        out_specs=pl.BlockSpec((None, r, tn), lambda i, j: (i, 0, j)),
        out_shape=jax.ShapeDtypeStruct((depth, r, n), F32),
        compiler_params=_cparams(("arbitrary", "arbitrary")),
        name="mods",
    )(cond, w_mod, b_mod.reshape(depth, 1, n))
    return out.reshape(depth * r * N_MODS, 1, d)


def _inproj_kernel(*refs, acts, has_pos):
    it = iter(refs)
    x_ref = next(it)
    pos_ref = next(it) if has_pos else None
    sc_ref, sh_ref, g_ref, w_ref = next(it), next(it), next(it), next(it)
    h_out = next(it) if has_pos else None
    outs = list(it)
    h = x_ref[...]
    if has_pos:
        h = h + pos_ref[...]
        h_out[...] = h
    a = _rms_mod(h, g_ref[...], sc_ref[...], sh_ref[...]).astype(BF16)
    col = 0
    for o, act in zip(outs, acts):
        width = o.shape[-1]
        p = _dot(a, w_ref[:, col:col + width])
        if act:
            p = jax.nn.sigmoid(p)
        o[...] = p.astype(o.dtype)
        col += width


def _inproj_call(x, pos, mods, sc_row, sh_row, g, w, layer, col0, widths, acts, tm):
    b, l, d = x.shape
    has_pos = pos is not None
    n_cols = sum(widths)
    col_blk = col0 // n_cols
    assert col_blk * n_cols == col0
    tok = pl.BlockSpec((None, tm, d), lambda i, m: (i, m, 0))
    vec = lambda row: pl.BlockSpec((None, 1, d), lambda i, m: (row(i), 0, 0))
    in_specs = [tok]
    args = [x]
    if has_pos:
        in_specs.append(pl.BlockSpec((tm, d), lambda i, m: (m, 0)))
        args.append(pos)
    in_specs += [vec(sc_row), vec(sh_row), pl.BlockSpec((None, 1, d), lambda i, m: (layer, 0, 0)),
                 pl.BlockSpec((None, d, n_cols), lambda i, m: (layer, 0, col_blk))]
    args += [mods, mods, g, w]
    out_specs, out_shape = [], []
    if has_pos:
        out_specs.append(tok)
        out_shape.append(jax.ShapeDtypeStruct((b, l, d), F32))
    for width in widths:
        out_specs.append(pl.BlockSpec((None, tm, width), lambda i, m: (i, m, 0)))
        out_shape.append(jax.ShapeDtypeStruct((b, l, width), BF16))
    return pl.pallas_call(
        functools.partial(_inproj_kernel, acts=tuple(acts), has_pos=has_pos),
        grid=(b, l // tm),
        in_specs=in_specs, out_specs=out_specs, out_shape=out_shape,
        compiler_params=_cparams(("parallel", "parallel")),
        name="inproj",
    )(*args)


DFT_PAD_ROWS = 16


def _dft_tables(l):
    k = np.arange(l // 2 + DFT_PAD_ROWS, dtype=np.int64)
    n = np.arange(l, dtype=np.int64)
    ang = 2.0 * np.pi * ((k[:, None] * n[None, :]) % l).astype(np.float64) / l
    dl = np.concatenate([np.cos(ang), np.sin(ang)], axis=1).astype(np.float32)
    c = np.arange(FOURIER_GROUP, dtype=np.int64)
    angc = 2.0 * np.pi * ((c[:, None] * c[None, :]) % FOURIER_GROUP).astype(np.float64) / FOURIER_GROUP
    return dl, np.cos(angc).astype(np.float32), np.sin(angc).astype(np.float32)


def _fourier_kernel(u_ref, cc_ref, cs_ref, dl_ref, nxt_ref, o_ref, v_ref, *, l, scale):
    m = pl.program_id(1)
    tk = dl_ref.shape[0]

    @pl.when(m == 0)
    def _():
        for j in range(N_FOURIER_GROUPS):
            cols = slice(j * FOURIER_GROUP, (j + 1) * FOURIER_GROUP)
            uj = u_ref[:, cols]
            v_ref[0:l, cols] = _dot(uj, cc_ref[...]).astype(BF16)
            v_ref[l:2 * l, cols] = _dot(uj, cs_ref[...]).astype(BF16)

    p = _dot(dl_ref[:, :l], v_ref[0:l, :])
    q = _dot(dl_ref[:, l:], v_ref[l:2 * l, :])
    o_ref[pl.ds(pl.multiple_of(m * tk, tk), tk), :] = ((p - q) * scale).astype(o_ref.dtype)
    row = lax.broadcasted_iota(jnp.int32, (tk, tk), 0)
    col = lax.broadcasted_iota(jnp.int32, (tk, tk), 1)
    flip = (row + col == tk).astype(BF16)
    mirrored = _dot(flip, ((p + q) * scale).astype(BF16))
    nxt = _dot(nxt_ref[:, :l], v_ref[0:l, :]) + _dot(nxt_ref[:, l:], v_ref[l:2 * l, :])
    first = lax.broadcasted_iota(jnp.int32, mirrored.shape, 0) == 0
    upper = jnp.where(first, nxt[0:1] * scale, mirrored)
    o_ref[pl.ds(pl.multiple_of(l - (m + 1) * tk, tk), tk), :] = upper.astype(o_ref.dtype)


def _fourier_call(u, tm):
    b, l, w = u.shape
    dl, cc, cs = _dft_tables(l)
    dl = jnp.asarray(dl).astype(BF16)
    cc = jnp.asarray(cc).astype(BF16)
    cs = jnp.asarray(cs).astype(BF16)
    scale = 1.0 / math.sqrt(l * FOURIER_GROUP)
    tk = min(tm, l // 2)
    nxt_blocks = tk // DFT_PAD_ROWS
    return pl.pallas_call(
        functools.partial(_fourier_kernel, l=l, scale=scale),
        grid=(b, l // 2 // tk),
        in_specs=[pl.BlockSpec((None, l, w), lambda i, m: (i, 0, 0)),
                  pl.BlockSpec((FOURIER_GROUP, FOURIER_GROUP), lambda i, m: (0, 0)),
                  pl.BlockSpec((FOURIER_GROUP, FOURIER_GROUP), lambda i, m: (0, 0)),
                  pl.BlockSpec((tk, 2 * l), lambda i, m: (m, 0)),
                  pl.BlockSpec((DFT_PAD_ROWS, 2 * l), lambda i, m: ((m + 1) * nxt_blocks, 0))],
        out_specs=pl.BlockSpec((None, l, w), lambda i, m: (i, 0, 0)),
        out_shape=jax.ShapeDtypeStruct((b, l, w), BF16),
        scratch_shapes=[pltpu.VMEM((2 * l, w), BF16)],
        compiler_params=_cparams(("parallel", "arbitrary")),
        name="fourier",
    )(u, cc, cs, dl, dl)


def _expand_matrix(n_outer, inner):
    r = np.arange(n_outer * inner)[:, None]
    c = np.arange(n_outer * GPT * inner)[None, :]
    same = (r // inner == c // (GPT * inner)) & (r % inner == c % inner)
    return jnp.asarray(same.astype(np.float32)).astype(BF16)


def _spread_groups(x, expand, row_inner, col_inner):
    y = jnp.einsum('jrc,cn->jrn', x, expand, preferred_element_type=BF16)
    r = lax.broadcasted_iota(jnp.int32, y.shape, 1)
    c = lax.broadcasted_iota(jnp.int32, y.shape, 2)
    return jnp.where((r // row_inner) % GPT == (c // col_inner) % GPT, y, jnp.zeros_like(y))


def _s5_tables(a_re, a_im, log_dt, b_re, b_im, c_re, c_im):
    hp = lax.Precision.HIGHEST
    q = CHUNK
    g = a_re.shape[1]
    dt = jnp.exp(log_dt)[..., None]
    lr, li = a_re * dt, a_im * dt
    em1_r = jnp.expm1(lr) * jnp.cos(li) - 2.0 * jnp.sin(0.5 * li) ** 2
    em1_i = jnp.exp(lr) * jnp.sin(li)
    den = a_re * a_re + a_im * a_im
    fr = (em1_r * a_re + em1_i * a_im) / den
    fi = (em1_i * a_re - em1_r * a_im) / den
    bbr = fr[..., None] * b_re - fi[..., None] * b_im
    bbi = fr[..., None] * b_im + fi[..., None] * b_re
    k = jnp.arange(q + 1, dtype=F32)
    mag = jnp.exp(lr[..., None] * k)
    pr = mag * jnp.cos(li[..., None] * k)
    pi = mag * jnp.sin(li[..., None] * k)

    prk = jnp.moveaxis(pr, -1, 2)[..., None]
    pik = jnp.moveaxis(pi, -1, 2)[..., None]
    wr = prk * bbr[:, :, None] - pik * bbi[:, :, None]
    wi = prk * bbi[:, :, None] + pik * bbr[:, :, None]
    kern = (jnp.einsum('dgop,dgtph->dgtoh', c_re, wr[:, :, :q], precision=hp)
            - jnp.einsum('dgop,dgtph->dgtoh', c_im, wi[:, :, :q], precision=hp))
    qi = jnp.arange(q)
    nt = g // GPT
    kf = kern[0].transpose(0, 1, 3, 2)
    kb = kern[1].transpose(0, 1, 3, 2)
    lag = jnp.concatenate([kb[:, :0:-1], kf[:, :1] + kb[:, :1], kf[:, 1:]], axis=1).astype(BF16)
    lag = (lag.reshape(nt, GPT, 2 * q - 1, SSM_GROUP, SSM_GROUP).transpose(0, 2, 1, 3, 4)
           .reshape(nt * (2 * q - 1), LANES, SSM_GROUP))
    lag_tile = _spread_groups(lag, _expand_matrix(1, SSM_GROUP), SSM_GROUP, SSM_GROUP)
    lag_tile = lag_tile.reshape(nt, 2 * q - 1, LANES, LANES)

    def inject(w):
        wf = w[0][:, q - 1 - qi]
        wb = w[1][:, qi]
        both = jnp.stack([wf, wb])
        return both.transpose(0, 1, 2, 4, 3).reshape(2, g, q * SSM_GROUP, SSM_STATE)

    wre, wim = inject(wr), inject(wi)

    def carry_out(sign):
        outs = []
        for d, idx in ((0, qi + 1), (1, q - qi)):
            ppr = pr[d][:, :, idx]
            ppi = pi[d][:, :, idx]
            cr = c_re[d].transpose(0, 2, 1)[:, :, None, :]
            ci = c_im[d].transpose(0, 2, 1)[:, :, None, :]
            if sign > 0:
                val = cr * ppr[..., None] - ci * ppi[..., None]
            else:
                val = -(cr * ppi[..., None] + ci * ppr[..., None])
            outs.append(val.reshape(g, SSM_STATE, q * SSM_GROUP))
        return jnp.stack(outs)

    cre, cim = carry_out(+1), carry_out(-1)

    qh = q * SSM_GROUP
    wc = jnp.stack([wre[0], wim[0], wre[1], wim[1]]).astype(BF16)
    wc = (wc.reshape(N_COMP, nt, GPT, q, SSM_GROUP, SSM_STATE).transpose(1, 3, 2, 4, 0, 5)
          .reshape(nt, q * LANES, N_COMP * SSM_STATE))
    cc = jnp.stack([cre[0], cim[0], cre[1], cim[1]]).astype(BF16)
    cc = (cc.reshape(N_COMP, nt, GPT, SSM_STATE, qh).transpose(1, 0, 2, 3, 4)
          .reshape(nt, N_COMP * GPT * SSM_STATE, qh))
    decay = jnp.stack([pr[0, ..., q], pi[0, ..., q], pr[1, ..., q], pi[1, ..., q]])
    decay = decay.reshape(N_COMP, nt, 1, GPT * SSM_STATE).transpose(1, 0, 2, 3)
    return lag_tile, wc, cc, decay


def _chunk_rows(u_ref):
    nc, _, nb, lanes = u_ref.shape
    return jnp.concatenate([u_ref[:, qq].reshape(nc * nb, lanes) for qq in range(CHUNK)], axis=-1)


SPREAD_COLS = 512


def _spread_into(dst_ref, compact_ref, expand_ref, row_inner, col_inner):
    rows, cols = dst_ref.shape
    rsh, csh = row_inner.bit_length() - 1, col_inner.bit_length() - 1
    assert 1 << rsh == row_inner and 1 << csh == col_inner
    r = lax.broadcasted_iota(jnp.int32, (rows, SPREAD_COLS), 0)
    c = lax.broadcasted_iota(jnp.int32, (rows, SPREAD_COLS), 1)
    r_gi = lax.shift_right_logical(r, rsh) & (GPT - 1)
    for c0 in range(0, cols, SPREAD_COLS):
        y = _dot(compact_ref[...], expand_ref[:, c0:c0 + SPREAD_COLS])
        c_gi = lax.shift_right_logical(c + c0, csh) & (GPT - 1)
        dst_ref[:, c0:c0 + SPREAD_COLS] = jnp.where(r_gi == c_gi, y, 0.0).astype(dst_ref.dtype)


def _s5_inject_kernel(u_ref, wc_ref, exp_ref, s_ref, w_ref):
    @pl.when(pl.program_id(1) == 0)
    def _():
        _spread_into(w_ref, wc_ref, exp_ref, SSM_GROUP, SSM_STATE)

    s = _dot(_chunk_rows(u_ref), w_ref[...])
    width = s_ref.shape[-1]
    for comp in range(N_COMP):
        s_ref[comp] = s[:, comp * width:(comp + 1) * width]


def _s5_scan_kernel(s_ref, a_ref, o_ref, *, batch, n_lat_chunks, n_ctx_chunks):
    backward = pl.program_id(1) == 1
    ar, ai = a_ref[0], a_ref[1]

    def step(i, carry):
        sr, si = carry
        k = i - n_ctx_chunks
        fwd = jnp.where(i < n_ctx_chunks, n_lat_chunks + i, k)
        bwd = jnp.where(i < n_ctx_chunks, n_lat_chunks + n_ctx_chunks - 1 - i, n_lat_chunks - 1 - k)
        rows = pl.ds(pl.multiple_of(jnp.where(backward, bwd, fwd) * batch, batch), batch)
        o_ref[0, rows, :] = sr.astype(o_ref.dtype)
        o_ref[1, rows, :] = si.astype(o_ref.dtype)
        return (ar * sr - ai * si + s_ref[0, rows, :], ar * si + ai * sr + s_ref[1, rows, :])

    z = jnp.zeros((batch, s_ref.shape[-1]), F32)
    lax.fori_loop(0, n_lat_chunks + n_ctx_chunks, step, (z, z))


def _s5_out_kernel(u_ref, sp_ref, lag_ref, cc_ref, exp_ref, d_ref, o_ref, m_ref, c_ref):
    nc, _, nb, lanes = u_ref.shape
    width = sp_ref.shape[-1]

    @pl.when(pl.program_id(1) == 0)
    def _():
        for q_in in range(CHUNK):
            for q_out in range(CHUNK):
                m_ref[q_in * lanes:(q_in + 1) * lanes, q_out * lanes:(q_out + 1) * lanes] = (
                    lag_ref[q_out - q_in + CHUNK - 1])
        _spread_into(c_ref, cc_ref, exp_ref, SSM_STATE, SSM_GROUP)

    y = _dot(_chunk_rows(u_ref), m_ref[...])
    for comp in range(N_COMP):
        y = y + _dot(sp_ref[comp], c_ref[comp * width:(comp + 1) * width, :])
    for qq in range(CHUNK):
        yq = y[:, qq * lanes:(qq + 1) * lanes].reshape(nc, nb, lanes)
        o_ref[:, qq] = jax.nn.gelu(d_ref[...] * u_ref[:, qq].astype(F32) + yq).astype(o_ref.dtype)


def _s5_call(u_ctx, u_lat, d_skip, tables, layer):
    lag_tile, wc, cc, decay = tables
    exp_w = _expand_matrix(N_COMP, SSM_STATE)
    exp_c = _expand_matrix(CHUNK, SSM_GROUP)
    op_rows = CHUNK * LANES
    full2 = lambda a: pl.BlockSpec(a.shape, lambda j, r: (0, 0))
    b, lc, w = u_ctx.shape
    l = u_lat.shape[1]
    nt = w // LANES
    n_lat, n_ctx = l // CHUNK, lc // CHUNK
    n_chunks = n_lat + n_ctx
    rows = n_chunks * b
    sw = GPT * SSM_STATE
    u_t = jnp.concatenate([u_lat, u_ctx], axis=1).transpose(1, 0, 2).reshape(n_chunks, CHUNK, b, w)
    cb = _pick_tile(n_chunks, 36)
    rb = cb * b
    u_spec = pl.BlockSpec((cb, CHUNK, b, LANES), lambda j, r: (r, 0, 0, j))
    op_spec = lambda shape: pl.BlockSpec((None, None) + shape, lambda j, r: (layer, j, 0, 0))
    s_loc = pl.pallas_call(
        _s5_inject_kernel,
        grid=(nt, n_chunks // cb),
        in_specs=[u_spec, op_spec(wc.shape[2:]), full2(exp_w)],
        out_specs=pl.BlockSpec((None, N_COMP, rb, sw), lambda j, r: (j, 0, r, 0)),
        out_shape=jax.ShapeDtypeStruct((nt, N_COMP, rows, sw), F32),
        scratch_shapes=[pltpu.VMEM((op_rows, N_COMP * sw), BF16)],
        compiler_params=_cparams(("parallel", "arbitrary")),
        name="s5_inject",
    )(u_t, wc, exp_w)
    s_prev = pl.pallas_call(
        functools.partial(_s5_scan_kernel, batch=b, n_lat_chunks=n_lat, n_ctx_chunks=n_ctx),
        grid=(nt, 2),
        in_specs=[pl.BlockSpec((None, 2, rows, sw), lambda j, d: (j, d, 0, 0)),
                  pl.BlockSpec((None, None, 2, 1, sw), lambda j, d: (layer, j, d, 0, 0))],
        out_specs=pl.BlockSpec((None, 2, rows, sw), lambda j, d: (j, d, 0, 0)),
        out_shape=jax.ShapeDtypeStruct((nt, N_COMP, rows, sw), BF16),
        compiler_params=_cparams(("parallel", "parallel")),
        name="s5_scan",
    )(s_loc, decay)
    y = pl.pallas_call(
        _s5_out_kernel,
        grid=(nt, n_chunks // cb),
        in_specs=[u_spec,
                  pl.BlockSpec((None, N_COMP, rb, sw), lambda j, r: (j, 0, r, 0)),
                  pl.BlockSpec((None, None) + lag_tile.shape[2:], lambda j, r: (layer, j, 0, 0, 0)),
                  op_spec(cc.shape[2:]), full2(exp_c),
                  pl.BlockSpec((None, 1, LANES), lambda j, r: (layer, 0, j))],
        out_specs=u_spec,
        out_shape=jax.ShapeDtypeStruct((n_chunks, CHUNK, b, w), BF16),
        scratch_shapes=[pltpu.VMEM((op_rows, op_rows), BF16), pltpu.VMEM((N_COMP * sw, op_rows), BF16)],
        compiler_params=_cparams(("parallel", "arbitrary")),
        name="s5_out",
    )(u_t, s_prev, lag_tile, cc, exp_c, d_skip)
    return y.reshape(l + lc, b, w).transpose(1, 0, 2)


def _merge_kernel(*refs, with_router):
    (h_ref, yf_ref, ys_ref, ga_ref, gb_ref, g1_ref, sc2_ref, sh2_ref, n2g_ref,
     wf_ref, wa_ref, wb_ref, wo_ref) = refs[:13]
    rest = refs[13:]
    if with_router:
        wr_ref, br_ref, h1_ref, f_ref, lg_ref = rest
    else:
        h1_ref, f_ref = rest
    ys = ys_ref[...]
    ya = _dot(yf_ref[...], wf_ref[...])
    yb = _dot(ys, wa_ref[...]) * jax.nn.sigmoid(_dot(ys, wb_ref[...]))
    m = (ga_ref[...].astype(F32) * ya + gb_ref[...].astype(F32) * yb).astype(BF16)
    h1 = h_ref[...] + g1_ref[...] * _dot(m, wo_ref[...])
    h1_ref[...] = h1
    f = _rms_mod(h1, n2g_ref[...], sc2_ref[...], sh2_ref[...])
    f_ref[...] = f.astype(f_ref.dtype)
    if with_router:
        lg_ref[...] = _dot3(f, wr_ref[...]) + br_ref[...]


def _merge_call(h, yf, ys_all, ys_row0, ga, gb, mods, g1_row, sc2_row, sh2_row, n2g, wf, wa, wb, wo, layer,
                router, tm):
    b, l, d = h.shape
    w = yf.shape[-1]
    tok = lambda width: pl.BlockSpec((None, tm, width), lambda i, m: (i, m, 0))
    vec = lambda row: pl.BlockSpec((None, 1, d), lambda i, m: (row(i), 0, 0))
    full = lambda a: pl.BlockSpec(a.shape, lambda i, m: (0,) * a.ndim)
    per_layer = lambda a: pl.BlockSpec((None,) + a.shape[1:], lambda i, m: (layer,) + (0,) * (a.ndim - 1))
    blk0 = ys_row0 // tm
    assert blk0 * tm == ys_row0
    args = [h, yf, ys_all, ga, gb, mods, mods, mods, n2g, wf, wa, wb, wo]
    in_specs = [tok(d), tok(w), pl.BlockSpec((None, tm, w), lambda i, m: (i, m + blk0, 0)), tok(d), tok(d),
                vec(g1_row), vec(sc2_row), vec(sh2_row), per_layer(n2g), per_layer(wf), per_layer(wa),
                per_layer(wb), per_layer(wo)]
    out_specs = [tok(d), tok(d)]
    f_dtype = F32 if router is not None else BF16
    out_shape = [jax.ShapeDtypeStruct((b, l, d), F32), jax.ShapeDtypeStruct((b, l, d), f_dtype)]
    if router is not None:
        wr, br = router
        args += [wr, br]
        in_specs += [full(wr), full(br)]
        out_specs.append(tok(ROUTER_PAD))
        out_shape.append(jax.ShapeDtypeStruct((b, l, ROUTER_PAD), F32))
    return pl.pallas_call(
        functools.partial(_merge_kernel, with_router=router is not None),
        grid=(b, l // tm),
        in_specs=in_specs, out_specs=out_specs, out_shape=out_shape,
        compiler_params=_cparams(("parallel", "parallel")),
        name="merge",
    )(*args)


R_E1, R_E2, R_RANK1, R_RANK2, R_W1, R_W2 = range(6)


def _route_kernel(lg_ref, rec_ref, cnt_ref, carry_ref):
    @pl.when(pl.program_id(0) == 0)
    def _():
        carry_ref[...] = jnp.zeros_like(carry_ref)

    lg = lg_ref[...]
    tm = lg.shape[0]
    lane = lax.broadcasted_iota(jnp.int32, lg.shape, 1)
    m1 = jnp.max(lg, axis=-1, keepdims=True)
    i1 = jnp.min(jnp.where(lg == m1, lane, ROUTER_PAD), axis=-1, keepdims=True)
    lg2 = jnp.where(lane == i1, -jnp.inf, lg)
    m2 = jnp.max(lg2, axis=-1, keepdims=True)
    i2 = jnp.min(jnp.where(lg2 == m2, lane, ROUTER_PAD), axis=-1, keepdims=True)
    e = jnp.exp(m2 - m1)
    w1 = 1.0 / (1.0 + e)
    w2 = e * w1

    oh1 = lane == i1
    oh2 = lane == i2
    row = lax.broadcasted_iota(jnp.int32, (tm, tm), 0)
    col = lax.broadcasted_iota(jnp.int32, (tm, tm), 1)
    below = (row > col).astype(BF16)
    p1 = _dot(below, oh1.astype(BF16))
    p2 = _dot(below, oh2.astype(BF16))
    c1 = jnp.sum(oh1.astype(F32), axis=0, keepdims=True)
    c2 = jnp.sum(oh2.astype(F32), axis=0, keepdims=True)
    base = carry_ref[...]
    r1 = jnp.sum(jnp.where(oh1, p1 + base, 0.0), axis=-1, keepdims=True)
    r2 = jnp.sum(jnp.where(oh2, p2 + (base + c1), 0.0), axis=-1, keepdims=True)
    total = base + c1 + c2
    carry_ref[...] = total
    cnt_ref[...] = total

    rec = jnp.zeros_like(lg)
    for slot, val in ((R_E1, i1.astype(F32)), (R_E2, i2.astype(F32)), (R_RANK1, r1), (R_RANK2, r2),
                      (R_W1, w1), (R_W2, w2)):
        rec = jnp.where(lane == slot, val, rec)
    rec_ref[...] = rec


def _route_call(logits, tm):
    t, n = logits.shape
    return pl.pallas_call(
        _route_kernel,
        grid=(t // tm,),
        in_specs=[pl.BlockSpec((tm, n), lambda i: (i, 0))],
        out_specs=[pl.BlockSpec((tm, n), lambda i: (i, 0)), pl.BlockSpec((1, n), lambda i: (0, 0))],
        out_shape=[jax.ShapeDtypeStruct((t, n), F32), jax.ShapeDtypeStruct((1, n), F32)],
        scratch_shapes=[pltpu.VMEM((1, n), F32)],
        compiler_params=_cparams(("arbitrary",)),
        name="route",
    )(logits)


def _dispatch_kernel(pos_ref, fill_ref, f_ref, xs_ref, zero_ref, sem, zsem, *, tile_rows):
    tm = f_ref.shape[0]
    base = pl.program_id(0) * (2 * tm)

    def row_copy(r, k):
        return pltpu.make_async_copy(f_ref.at[pl.ds(r, 1)], xs_ref.at[pl.ds(pos_ref[base + 2 * r + k], 1)], sem)

    def issue(r, carry):
        row_copy(r, 0).start()
        row_copy(r, 1).start()
        return carry

    lax.fori_loop(0, tm, issue, 0, unroll=8)
    for _ in range(2):
        pltpu.make_async_copy(f_ref, xs_ref.at[pl.ds(0, tm)], sem).wait()

    @pl.when(pl.program_id(0) == pl.num_programs(0) - 1)
    def _():
        zero_ref[...] = jnp.zeros_like(zero_ref)

        def zero_row(row):
            return pltpu.make_async_copy(zero_ref.at[pl.ds(0, 1)], xs_ref.at[pl.ds(row, 1)], zsem)

        def zero_tile(tile):
            rows = pl.ds(pl.multiple_of(tile * tile_rows, tile_rows), tile_rows)
            return pltpu.make_async_copy(zero_ref, xs_ref.at[rows], zsem)

        def start(copy_of):
            def body(k, carry):
                copy_of(k).start()
                return carry
            return body

        def wait(copy_of):
            def body(k, carry):
                copy_of(k).wait()
                return carry
            return body

        for e in range(N_EXPERTS):
            first, count = fill_ref[e], fill_ref[N_EXPERTS + e]
            lax.fori_loop(first, first + count, start(zero_row), 0)
            lax.fori_loop(first, first + count, wait(zero_row), 0)
        n_used, n_tiles = fill_ref[2 * N_EXPERTS], xs_ref.shape[0] // tile_rows
        lax.fori_loop(n_used, n_tiles, start(zero_tile), 0)
        lax.fori_loop(n_used, n_tiles, wait(zero_tile), 0)


def _dispatch_call(pos, fill, f, n_rows, tile_rows, tm):
    t, d = f.shape
    return pl.pallas_call(
        functools.partial(_dispatch_kernel, tile_rows=tile_rows),
        grid_spec=pltpu.PrefetchScalarGridSpec(
            num_scalar_prefetch=2,
            grid=(t // tm,),
            in_specs=[pl.BlockSpec((tm, d), lambda i, pos, fill: (i, 0))],
            out_specs=pl.BlockSpec(memory_space=pl.ANY),
            scratch_shapes=[pltpu.VMEM((tile_rows, d), f.dtype), pltpu.SemaphoreType.DMA,
                            pltpu.SemaphoreType.DMA]),
        out_shape=jax.ShapeDtypeStruct((n_rows, d), f.dtype),
        compiler_params=pltpu.CompilerParams(dimension_semantics=("arbitrary",), vmem_limit_bytes=VMEM_LIMIT,
                                             disable_bounds_checks=True),
        name="dispatch",
    )(pos, fill, f)


N_WBUF = 3


def _weight_ring(wg_hbm, wu_hbm, wd_hbm, wg_buf, wu_buf, wd_buf, sem, n_steps, weight_set):
    i, j = pl.program_id(0), pl.program_id(1)
    nj = pl.num_programs(1)
    tf = wg_buf.shape[-1]
    g = i * nj + j

    def copies(ci, cj, slot):
        cols = pl.ds(pl.multiple_of(cj * tf, tf), tf)
        s = weight_set(ci)
        return (pltpu.make_async_copy(wg_hbm.at[s, :, cols], wg_buf.at[slot], sem.at[0, slot]),
                pltpu.make_async_copy(wu_hbm.at[s, :, cols], wu_buf.at[slot], sem.at[1, slot]),
                pltpu.make_async_copy(wd_hbm.at[s, cols, :], wd_buf.at[slot], sem.at[2, slot]))

    def start_ahead(ahead):
        ci = lax.div(g + ahead, nj)
        cj = g + ahead - ci * nj
        ci = jnp.minimum(ci, pl.num_programs(0) - 1)

        @pl.when(g + ahead < n_steps)
        def _():
            for c in copies(ci, cj, lax.rem(g + ahead, N_WBUF)):
                c.start()

    @pl.when(g == 0)
    def _():
        for ahead in range(N_WBUF - 1):
            start_ahead(ahead)

    start_ahead(N_WBUF - 1)
    slot = lax.rem(g, N_WBUF)

    @pl.when(g < n_steps)
    def _():
        for c in copies(i, j, slot):
            c.wait()

    return slot


ROW_TILE = 8


def _gffn_kernel(te_ref, nu_ref, x_ref, wg_hbm, wu_hbm, wd_hbm, o_ref, xb_ref, acc_ref, wg_buf, wu_buf, wd_buf, sem,
                 *, set0):
    j = pl.program_id(1)
    last = j == pl.num_programs(1) - 1
    used = pl.program_id(0) < nu_ref[0]
    slot = _weight_ring(wg_hbm, wu_hbm, wd_hbm, wg_buf, wu_buf, wd_buf, sem,
                        nu_ref[0] * pl.num_programs(1), lambda ci: set0 + te_ref[ci])

    @pl.when(used)
    def _():
        @pl.when(j == 0)
        def _():
            xb_ref[...] = x_ref[...].astype(BF16)
            acc_ref[...] = jnp.zeros_like(acc_ref)

        x = xb_ref[...]
        hid = jax.nn.silu(_dot(x, wg_buf[slot].astype(BF16))) * _dot(x, wu_buf[slot].astype(BF16))
        acc_ref[...] += _dot(hid.astype(BF16), wd_buf[slot].astype(BF16))

        @pl.when(last)
        def _():
            tm = acc_ref.shape[0]
            for c in range(ROW_TILE):
                o_ref[pl.ds(c, tm, stride=ROW_TILE), :] = acc_ref[:, c * LANES:(c + 1) * LANES]

    @pl.when(jnp.logical_not(used) & last)
    def _():
        o_ref[...] = jnp.zeros_like(o_ref)


def _gffn_call(tile_expert, n_used, xs, wg, wu, wd, idx, tm, tf):
    n_rows, d = xs.shape
    n_e, f = wg.shape[1], wg.shape[-1]
    nj = f // tf
    wg, wu, wd = (w.reshape((-1,) + w.shape[2:]) for w in (wg, wu, wd))

    def row_map(i, j, te, nu):
        return jnp.minimum(i, nu[0] - 1), 0

    hbm = pl.BlockSpec(memory_space=pl.ANY)
    return pl.pallas_call(
        functools.partial(_gffn_kernel, set0=idx * n_e),
        grid_spec=pltpu.PrefetchScalarGridSpec(
            num_scalar_prefetch=2,
            grid=(n_rows // tm, nj),
            in_specs=[pl.BlockSpec((tm, d), row_map), hbm, hbm, hbm],
            out_specs=pl.BlockSpec((tm * ROW_TILE, LANES), lambda i, j, te, nu: (i, 0)),
            scratch_shapes=[pltpu.VMEM((tm, d), BF16), pltpu.VMEM((tm, d), F32),
                            pltpu.VMEM((N_WBUF, d, tf), wg.dtype), pltpu.VMEM((N_WBUF, d, tf), wu.dtype),
                            pltpu.VMEM((N_WBUF, tf, d), wd.dtype), pltpu.SemaphoreType.DMA((3, N_WBUF))]),
        out_shape=jax.ShapeDtypeStruct((n_rows * ROW_TILE, LANES), F32),
        compiler_params=_cparams(("arbitrary", "arbitrary")),
        name="gffn",
    )(tile_expert, n_used, xs, wg, wu, wd)


def _combine_kernel(pos_ref, ys_ref, h_ref, rec_ref, g2_ref, fg_ref, o_ref, ybuf_ref, sem, *, final_norm):
    tm = h_ref.shape[0]
    base = pl.program_id(0) * (2 * tm)

    def row_copy(r, k):
        src = pl.multiple_of(pos_ref[base + 2 * r + k] * ROW_TILE, ROW_TILE)
        dst = pl.multiple_of(r * ROW_TILE, ROW_TILE)
        return pltpu.make_async_copy(ys_ref.at[pl.ds(src, ROW_TILE)], ybuf_ref.at[k, pl.ds(dst, ROW_TILE)], sem)

    def issue(r, carry):
        row_copy(r, 0).start()
        row_copy(r, 1).start()
        return carry

    lax.fori_loop(0, tm, issue, 0, unroll=8)
    for k in range(2):
        pltpu.make_async_copy(ys_ref.at[pl.ds(0, tm * ROW_TILE)], ybuf_ref.at[k], sem).wait()
    rec = rec_ref[...]
    w1, w2 = rec[:, R_W1:R_W1 + 1], rec[:, R_W2:R_W2 + 1]
    y = jnp.concatenate([w1 * ybuf_ref[0, pl.ds(c, tm, stride=ROW_TILE), :]
                         + w2 * ybuf_ref[1, pl.ds(c, tm, stride=ROW_TILE), :] for c in range(ROW_TILE)], axis=-1)
    out = h_ref[...] + g2_ref[...] * y
    if final_norm:
        out = out * lax.rsqrt(jnp.mean(out * out, axis=-1, keepdims=True) + EPS) * fg_ref[...]
    o_ref[...] = out


def _combine_call(pos, ys, h, rec, mods, g2_row, tiles_per_batch, final_g, tm):
    t, d = h.shape
    fg = (final_g if final_g is not None else jnp.ones((d,), F32)).reshape(1, d)
    return pl.pallas_call(
        functools.partial(_combine_kernel, final_norm=final_g is not None),
        grid_spec=pltpu.PrefetchScalarGridSpec(
            num_scalar_prefetch=1,
            grid=(t // tm,),
            in_specs=[pl.BlockSpec(memory_space=pl.ANY),
                      pl.BlockSpec((tm, d), lambda i, pos: (i, 0)),
                      pl.BlockSpec((tm, ROUTER_PAD), lambda i, pos: (i, 0)),
                      pl.BlockSpec((None, 1, d), lambda i, pos: (g2_row(i // tiles_per_batch), 0, 0)),
                      pl.BlockSpec((1, d), lambda i, pos: (0, 0))],
            out_specs=pl.BlockSpec((tm, d), lambda i, pos: (i, 0)),
            scratch_shapes=[pltpu.VMEM((2, tm * ROW_TILE, LANES), F32), pltpu.SemaphoreType.DMA]),
        out_shape=jax.ShapeDtypeStruct((t, d), F32),
        compiler_params=pltpu.CompilerParams(dimension_semantics=("arbitrary",), vmem_limit_bytes=VMEM_LIMIT,
                                             disable_bounds_checks=True),
        name="combine",
    )(pos, ys, h, rec, mods, fg)


def _moe_plan(rec, counts, tm, n_tiles):
    cnt = counts[0, :N_EXPERTS].astype(jnp.int32)
    nt = (cnt + (tm - 1)) // tm
    cum = jnp.cumsum(nt)
    start = (cum - nt) * tm
    e = rec[:, R_E1:R_E2 + 1].astype(jnp.int32)
    rank = rec[:, R_RANK1:R_RANK2 + 1].astype(jnp.int32)
    ex = lax.broadcasted_iota(jnp.int32, e.shape + (N_EXPERTS,), 2)
    pos = rank + jnp.sum(jnp.where(e[..., None] == ex, start, 0), axis=-1)
    tile = jnp.arange(n_tiles, dtype=jnp.int32)
    te = jnp.sum((tile[:, None] >= cum[None, :]).astype(jnp.int32), axis=1)
    n_used = cum[-1:]
    last_e = jnp.sum((n_used - 1 >= cum).astype(jnp.int32))
    fill = jnp.concatenate([start + cnt, nt * tm - cnt, n_used])
    return pos.reshape(-1), jnp.minimum(te, last_e), n_used, fill


def _ffn_kernel(x_ref, wg_hbm, wu_hbm, wd_hbm, h_ref, g2_ref, fg_ref, o_ref, acc_ref, wg_buf, wu_buf, wd_buf, sem,
                *, final_norm, idx):
    j = pl.program_id(1)
    slot = _weight_ring(wg_hbm, wu_hbm, wd_hbm, wg_buf, wu_buf, wd_buf, sem,
                        pl.num_programs(0) * pl.num_programs(1), lambda ci: idx)

    @pl.when(j == 0)
    def _():
        acc_ref[...] = jnp.zeros_like(acc_ref)

    x = x_ref[...]
    hid = jax.nn.silu(_dot(x, wg_buf[slot].astype(BF16))) * _dot(x, wu_buf[slot].astype(BF16))
    acc_ref[...] += _dot(hid.astype(BF16), wd_buf[slot].astype(BF16))

    @pl.when(j == pl.num_programs(1) - 1)
    def _():
        out = h_ref[...] + g2_ref[...] * acc_ref[...]
        if final_norm:
            out = out * lax.rsqrt(jnp.mean(out * out, axis=-1, keepdims=True) + EPS) * fg_ref[...]
        o_ref[...] = out


def _ffn_call(x, wg, wu, wd, idx, h, mods, g2_row, tiles_per_batch, final_g, tm, tf):
    t, d = x.shape
    f = wg.shape[-1]
    tok = lambda width: pl.BlockSpec((tm, width), lambda m, j: (m, 0))
    fg = (final_g if final_g is not None else jnp.ones((d,), F32)).reshape(1, d)
    hbm = pl.BlockSpec(memory_space=pl.ANY)
    return pl.pallas_call(
        functools.partial(_ffn_kernel, final_norm=final_g is not None, idx=idx),
        grid=(t // tm, f // tf),
        in_specs=[tok(d), hbm, hbm, hbm,
                  tok(d),
                  pl.BlockSpec((None, 1, d), lambda m, j: (g2_row(m // tiles_per_batch), 0, 0)),
                  pl.BlockSpec((1, d), lambda m, j: (0, 0))],
        out_specs=tok(d),
        out_shape=jax.ShapeDtypeStruct((t, d), F32),
        scratch_shapes=[pltpu.VMEM((tm, d), F32),
                        pltpu.VMEM((N_WBUF, d, tf), wg.dtype), pltpu.VMEM((N_WBUF, d, tf), wu.dtype),
                        pltpu.VMEM((N_WBUF, tf, d), wd.dtype), pltpu.SemaphoreType.DMA((3, N_WBUF))],
        compiler_params=_cparams(("arbitrary", "arbitrary")),
        name="ffn",
    )(x, wg, wu, wd, h, mods, fg)


def _pos_table(n, dim):
    t = np.arange(n)
    r = (t // GRID_W).astype(np.float32)
    col = (t % GRID_W).astype(np.float32)
    quarter = dim // 4
    omega = (1.0 / (POS_BASE ** (np.arange(quarter, dtype=np.float32) / quarter))).astype(np.float32)
    ar = r[:, None] * omega
    ac = col[:, None] * omega
    return np.concatenate([np.sin(ar), np.cos(ar), np.sin(ac), np.cos(ac)], axis=-1).astype(np.float32)


def _pick_tile(n, pref):
    return pref if n % pref == 0 else n


def kernel(x, c, ctx, c_ctx, w_mod, b_mod, norm1_g, norm2_g, w_in, w_four, ssm_a_re, ssm_a_im, ssm_log_dt, ssm_b_re, ssm_b_im, ssm_c_re, ssm_c_im, ssm_d, w_glu_a, w_glu_b, w_out, ffn_w_gate, ffn_w_up, ffn_w_down, moe_w_router, moe_b_router, moe_w_gate, moe_w_up, moe_w_down, final_g):
    b, l, d = x.shape
    lc = ctx.shape[1]
    depth = w_mod.shape[0]
    wf_cols = w_four.shape[1]
    ws_cols = ssm_d.shape[1]
    off_ga = wf_cols + ws_cols
    tm_lat = _pick_tile(l, 512)
    tm_ctx = _pick_tile(lc, 256)

    n_cond = b + 8
    cond = jnp.concatenate([c, jnp.broadcast_to(c_ctx[None], (n_cond - b, d))], axis=0)
    pos = jnp.asarray(_pos_table(l, d))

    mods = _mods_call(cond, w_mod, b_mod)
    w_in_b, wf, wa, wb, wo = (w.astype(BF16) for w in (w_in, w_four, w_glu_a, w_glu_b, w_out))
    n1g, n2g = norm1_g.reshape(depth, 1, d), norm2_g.reshape(depth, 1, d)
    d_skip = ssm_d.reshape(depth, 1, ws_cols)
    tables = jax.vmap(_s5_tables)(ssm_a_re, ssm_a_im, ssm_log_dt, ssm_b_re, ssm_b_im, ssm_c_re, ssm_c_im)

    def mod_rows(layer, which):
        return (lambda i: (layer * n_cond + i) * N_MODS + which,
                lambda i: (layer * n_cond + b) * N_MODS + which)

    h = x
    z = ctx
    for layer in range(depth):
        need_ctx = layer < depth - 1
        sh1, sc1, g1, sh2, sc2, g2 = (mod_rows(layer, k) for k in range(N_MODS))
        widths = (wf_cols, ws_cols, d, d)
        acts = (False, False, True, True)

        outs = _inproj_call(h, pos if layer == 0 else None, mods, sc1[0], sh1[0], n1g, w_in_b, layer, 0,
                            widths, acts, tm_lat)
        if layer == 0:
            h, outs = outs[0], outs[1:]
        pf, ps, ga, gb = outs
        if need_ctx:
            cpf, cps, cga, cgb = _inproj_call(z, None, mods, sc1[1], sh1[1], n1g, w_in_b, layer, 0,
                                              widths, acts, tm_ctx)
        else:
            (cps,) = _inproj_call(z, None, mods, sc1[1], sh1[1], n1g, w_in_b, layer, wf_cols,
                                  (ws_cols,), (False,), tm_ctx)

        ys_all = _s5_call(cps, ps, d_skip, tables, layer)
        yf = _fourier_call(pf, tm_lat)

        moe = layer % 2 == 1
        idx = layer // 2
        router = None
        if moe:
            wr = jnp.pad(moe_w_router[idx], ((0, 0), (0, ROUTER_PAD - N_EXPERTS)))
            br = jnp.pad(moe_b_router[idx][None], ((0, 0), (0, ROUTER_PAD - N_EXPERTS)), constant_values=NEG_BIG)
            router = (wr, br)
        res = _merge_call(h, yf, ys_all, 0, ga, gb, mods, g1[0], sc2[0], sh2[0], n2g, wf, wa, wb, wo, layer,
                          router, tm_lat)
        last = layer == depth - 1
        fin = final_g if last else None
        t = b * l
        if moe:
            h1, f, logits = res
            tm_moe = min(1024, max(128, t // 8))
            n_tiles = 2 * t // tm_moe + N_EXPERTS
            rec, counts = _route_call(logits.reshape(t, ROUTER_PAD), _pick_tile(t, 512))
            pos, tile_expert, n_used, fill = _moe_plan(rec, counts, tm_moe, n_tiles)
            xs = _dispatch_call(pos, fill, f.reshape(t, d), n_tiles * tm_moe, tm_moe, _pick_tile(t, 512))
            ys = _gffn_call(tile_expert, n_used, xs, moe_w_gate, moe_w_up, moe_w_down, idx,
                            tm_moe, _pick_tile(moe_w_gate.shape[-1], FFN_TILE))
            tm_c = _pick_tile(l, 512)
            h = _combine_call(pos, ys, h1.reshape(t, d), rec, mods, g2[0], l // tm_c, fin, tm_c).reshape(b, l, d)
        else:
            h1, f = res
            ffn_w = (ffn_w_gate, ffn_w_up, ffn_w_down, idx)
            tf = _pick_tile(ffn_w_gate.shape[-1], 256)
            tm_f = _pick_tile(l, 1024)
            h = _ffn_call(f.reshape(t, d), *ffn_w, h1.reshape(t, d), mods, g2[0], l // tm_f, fin,
                          tm_f, tf).reshape(b, l, d)

        if need_ctx:
            if moe:
                raise NotImplementedError("context tokens through an expert layer")
            cyf = _fourier_call(cpf, tm_ctx)
            z1, cf = _merge_call(z, cyf, ys_all, l, cga, cgb, mods, g1[1], sc2[1], sh2[1], n2g, wf, wa, wb, wo,
                                 layer, None, tm_ctx)
            z = _ffn_call(cf.reshape(b * lc, d), *ffn_w, z1.reshape(b * lc, d), mods, g2[1], 1, None,
                          _pick_tile(b * lc, 1024), tf).reshape(b, lc, d)
    return h
```

```python
import functools
import math

import numpy as np
import jax
import jax.numpy as jnp
from jax import lax
from jax.experimental import pallas as pl
from jax.experimental.pallas import tpu as pltpu

F32 = jnp.float32
BF16 = jnp.bfloat16

GRID_W = 64
N_FOURIER_GROUPS = 4
FOURIER_GROUP = 128
SSM_GROUP = 16
SSM_STATE = 64
N_EXPERTS = 8
EPS = 1e-6
POS_BASE = 10000.0

CHUNK = 16
LANES = 128
GPT = LANES // SSM_GROUP
N_COMP = 4
FFN_TILE = 512
ROUTER_PAD = LANES
NEG_BIG = -1e30
VMEM_LIMIT = 56 * 1024 * 1024


def _cparams(sem):
    return pltpu.CompilerParams(dimension_semantics=sem, vmem_limit_bytes=VMEM_LIMIT)


def _dot(a, b):
    return jnp.dot(a, b, preferred_element_type=F32)


def _split_bf16(a):
    hi = a.astype(BF16)
    lo = (a - hi.astype(F32)).astype(BF16)
    return hi, lo


def _dot3(a, b):
    ah, al = _split_bf16(a)
    bh, bl = _split_bf16(b)
    return _dot(ah, bh) + (_dot(ah, bl) + _dot(al, bh))


def _rms_mod(h, g, sc, sh):
    y = h * lax.rsqrt(jnp.mean(h * h, axis=-1, keepdims=True) + EPS)
    return (y * g) * (1.0 + sc) + sh


def _mods_kernel(c_ref, w_ref, b_ref, o_ref):
    s = jax.nn.silu(c_ref[...])
    o_ref[...] = _dot3(s, w_ref[...]) + b_ref[...]


N_MODS = 6
MOD_SHIFT1, MOD_SCALE1, MOD_GATE1, MOD_SHIFT2, MOD_SCALE2, MOD_GATE2 = range(N_MODS)


def _mods_call(cond, w_mod, b_mod):
    r, d = cond.shape
    depth, _, n = w_mod.shape
    tn = 1024
    out = pl.pallas_call(
        _mods_kernel,
        grid=(depth, n // tn),
        in_specs=[pl.BlockSpec((r, d), lambda i, j: (0, 0)),
                  pl.BlockSpec((None, d, tn), lambda i, j: (i, 0, j)),
                  pl.BlockSpec((None, 1, tn), lambda i, j: (i, 0, j))],
        out_specs=pl.BlockSpec((None, r, tn), lambda i, j: (i, 0, j)),
        out_shape=jax.ShapeDtypeStruct((depth, r, n), F32),
        compiler_params=_cparams(("arbitrary", "arbitrary")),
        name="mods",
    )(cond, w_mod, b_mod.reshape(depth, 1, n))
    return out.reshape(depth * r * N_MODS, 1, d)


def _inproj_kernel(*refs, acts, has_pos):
    it = iter(refs)
    x_ref = next(it)
    pos_ref = next(it) if has_pos else None
    sc_ref, sh_ref, g_ref, w_ref = next(it), next(it), next(it), next(it)
    h_out = next(it) if has_pos else None
    outs = list(it)
    h = x_ref[...]
    if has_pos:
        h = h + pos_ref[...]
        h_out[...] = h
    a = _rms_mod(h, g_ref[...], sc_ref[...], sh_ref[...]).astype(BF16)
    col = 0
    for o, act in zip(outs, acts):
        width = o.shape[-1]
        p = _dot(a, w_ref[:, col:col + width])
        if act:
            p = jax.nn.sigmoid(p)
        o[...] = p.astype(o.dtype)
        col += width


def _inproj_call(x, pos, mods, sc_row, sh_row, g, w, layer, col0, widths, acts, tm):
    b, l, d = x.shape
    has_pos = pos is not None
    n_cols = sum(widths)
    col_blk = col0 // n_cols
    assert col_blk * n_cols == col0
    tok = pl.BlockSpec((None, tm, d), lambda i, m: (i, m, 0))
    vec = lambda row: pl.BlockSpec((None, 1, d), lambda i, m: (row(i), 0, 0))
    in_specs = [tok]
    args = [x]
    if has_pos:
        in_specs.append(pl.BlockSpec((tm, d), lambda i, m: (m, 0)))
        args.append(pos)
    in_specs += [vec(sc_row), vec(sh_row), pl.BlockSpec((None, 1, d), lambda i, m: (layer, 0, 0)),
                 pl.BlockSpec((None, d, n_cols), lambda i, m: (layer, 0, col_blk))]
    args += [mods, mods, g, w]
    out_specs, out_shape = [], []
    if has_pos:
        out_specs.append(tok)
        out_shape.append(jax.ShapeDtypeStruct((b, l, d), F32))
    for width in widths:
        out_specs.append(pl.BlockSpec((None, tm, width), lambda i, m: (i, m, 0)))
        out_shape.append(jax.ShapeDtypeStruct((b, l, width), BF16))
    return pl.pallas_call(
        functools.partial(_inproj_kernel, acts=tuple(acts), has_pos=has_pos),
        grid=(b, l // tm),
        in_specs=in_specs, out_specs=out_specs, out_shape=out_shape,
        compiler_params=_cparams(("parallel", "parallel")),
        name="inproj",
    )(*args)


DFT_PAD_ROWS = 16


def _dft_tables(l):
    k = np.arange(l // 2 + DFT_PAD_ROWS, dtype=np.int64)
    n = np.arange(l, dtype=np.int64)
    ang = 2.0 * np.pi * ((k[:, None] * n[None, :]) % l).astype(np.float64) / l
    dl = np.concatenate([np.cos(ang), np.sin(ang)], axis=1).astype(np.float32)
    c = np.arange(FOURIER_GROUP, dtype=np.int64)
    angc = 2.0 * np.pi * ((c[:, None] * c[None, :]) % FOURIER_GROUP).astype(np.float64) / FOURIER_GROUP
    return dl, np.cos(angc).astype(np.float32), np.sin(angc).astype(np.float32)


def _fourier_kernel(u_ref, cc_ref, cs_ref, dl_ref, nxt_ref, o_ref, v_ref, *, l, scale):
    m = pl.program_id(1)
    tk = dl_ref.shape[0]

    @pl.when(m == 0)
    def _():
        for j in range(N_FOURIER_GROUPS):
            cols = slice(j * FOURIER_GROUP, (j + 1) * FOURIER_GROUP)
            uj = u_ref[:, cols]
            v_ref[0:l, cols] = _dot(uj, cc_ref[...]).astype(BF16)
            v_ref[l:2 * l, cols] = _dot(uj, cs_ref[...]).astype(BF16)

    p = _dot(dl_ref[:, :l], v_ref[0:l, :])
    q = _dot(dl_ref[:, l:], v_ref[l:2 * l, :])
    o_ref[pl.ds(pl.multiple_of(m * tk, tk), tk), :] = ((p - q) * scale).astype(o_ref.dtype)
    row = lax.broadcasted_iota(jnp.int32, (tk, tk), 0)
    col = lax.broadcasted_iota(jnp.int32, (tk, tk), 1)
    flip = (row + col == tk).astype(BF16)
    mirrored = _dot(flip, ((p + q) * scale).astype(BF16))
    nxt = _dot(nxt_ref[:, :l], v_ref[0:l, :]) + _dot(nxt_ref[:, l:], v_ref[l:2 * l, :])
    first = lax.broadcasted_iota(jnp.int32, mirrored.shape, 0) == 0
    upper = jnp.where(first, nxt[0:1] * scale, mirrored)
    o_ref[pl.ds(pl.multiple_of(l - (m + 1) * tk, tk), tk), :] = upper.astype(o_ref.dtype)


def _fourier_call(u, tm):
    b, l, w = u.shape
    dl, cc, cs = _dft_tables(l)
    dl = jnp.asarray(dl).astype(BF16)
    cc = jnp.asarray(cc).astype(BF16)
    cs = jnp.asarray(cs).astype(BF16)
    scale = 1.0 / math.sqrt(l * FOURIER_GROUP)
    tk = min(tm, l // 2)
    nxt_blocks = tk // DFT_PAD_ROWS
    return pl.pallas_call(
        functools.partial(_fourier_kernel, l=l, scale=scale),
        grid=(b, l // 2 // tk),
        in_specs=[pl.BlockSpec((None, l, w), lambda i, m: (i, 0, 0)),
                  pl.BlockSpec((FOURIER_GROUP, FOURIER_GROUP), lambda i, m: (0, 0)),
                  pl.BlockSpec((FOURIER_GROUP, FOURIER_GROUP), lambda i, m: (0, 0)),
                  pl.BlockSpec((tk, 2 * l), lambda i, m: (m, 0)),
                  pl.BlockSpec((DFT_PAD_ROWS, 2 * l), lambda i, m: ((m + 1) * nxt_blocks, 0))],
        out_specs=pl.BlockSpec((None, l, w), lambda i, m: (i, 0, 0)),
        out_shape=jax.ShapeDtypeStruct((b, l, w), BF16),
        scratch_shapes=[pltpu.VMEM((2 * l, w), BF16)],
        compiler_params=_cparams(("parallel", "arbitrary")),
        name="fourier",
    )(u, cc, cs, dl, dl)


def _expand_matrix(n_outer, inner):
    r = np.arange(n_outer * inner)[:, None]
    c = np.arange(n_outer * GPT * inner)[None, :]
    same = (r // inner == c // (GPT * inner)) & (r % inner == c % inner)
    return jnp.asarray(same.astype(np.float32)).astype(BF16)


def _spread_groups(x, expand, row_inner, col_inner):
    y = jnp.einsum('jrc,cn->jrn', x, expand, preferred_element_type=BF16)
    r = lax.broadcasted_iota(jnp.int32, y.shape, 1)
    c = lax.broadcasted_iota(jnp.int32, y.shape, 2)
    return jnp.where((r // row_inner) % GPT == (c // col_inner) % GPT, y, jnp.zeros_like(y))


def _s5_tables(a_re, a_im, log_dt, b_re, b_im, c_re, c_im):
    hp = lax.Precision.HIGHEST
    q = CHUNK
    g = a_re.shape[1]
    dt = jnp.exp(log_dt)[..., None]
    lr, li = a_re * dt, a_im * dt
    em1_r = jnp.expm1(lr) * jnp.cos(li) - 2.0 * jnp.sin(0.5 * li) ** 2
    em1_i = jnp.exp(lr) * jnp.sin(li)
    den = a_re * a_re + a_im * a_im
    fr = (em1_r * a_re + em1_i * a_im) / den
    fi = (em1_i * a_re - em1_r * a_im) / den
    bbr = fr[..., None] * b_re - fi[..., None] * b_im
    bbi = fr[..., None] * b_im + fi[..., None] * b_re
    k = jnp.arange(q + 1, dtype=F32)
    mag = jnp.exp(lr[..., None] * k)
    pr = mag * jnp.cos(li[..., None] * k)
    pi = mag * jnp.sin(li[..., None] * k)

    prk = jnp.moveaxis(pr, -1, 2)[..., None]
    pik = jnp.moveaxis(pi, -1, 2)[..., None]
    wr = prk * bbr[:, :, None] - pik * bbi[:, :, None]
    wi = prk * bbi[:, :, None] + pik * bbr[:, :, None]
    kern = (jnp.einsum('dgop,dgtph->dgtoh', c_re, wr[:, :, :q], precision=hp)
            - jnp.einsum('dgop,dgtph->dgtoh', c_im, wi[:, :, :q], precision=hp))
    qi = jnp.arange(q)
    nt = g // GPT
    kf = kern[0].transpose(0, 1, 3, 2)
    kb = kern[1].transpose(0, 1, 3, 2)
    lag = jnp.concatenate([kb[:, :0:-1], kf[:, :1] + kb[:, :1], kf[:, 1:]], axis=1).astype(BF16)
    lag = (lag.reshape(nt, GPT, 2 * q - 1, SSM_GROUP, SSM_GROUP).transpose(0, 2, 1, 3, 4)
           .reshape(nt * (2 * q - 1), LANES, SSM_GROUP))
    lag_tile = _spread_groups(lag, _expand_matrix(1, SSM_GROUP), SSM_GROUP, SSM_GROUP)
    lag_tile = lag_tile.reshape(nt, 2 * q - 1, LANES, LANES)

    def inject(w):
        wf = w[0][:, q - 1 - qi]
        wb = w[1][:, qi]
        both = jnp.stack([wf, wb])
        return both.transpose(0, 1, 2, 4, 3).reshape(2, g, q * SSM_GROUP, SSM_STATE)

    wre, wim = inject(wr), inject(wi)

    def carry_out(sign):
        outs = []
        for d, idx in ((0, qi + 1), (1, q - qi)):
            ppr = pr[d][:, :, idx]
            ppi = pi[d][:, :, idx]
            cr = c_re[d].transpose(0, 2, 1)[:, :, None, :]
            ci = c_im[d].transpose(0, 2, 1)[:, :, None, :]
            if sign > 0:
                val = cr * ppr[..., None] - ci * ppi[..., None]
            else:
                val = -(cr * ppi[..., None] + ci * ppr[..., None])
            outs.append(val.reshape(g, SSM_STATE, q * SSM_GROUP))
        return jnp.stack(outs)

    cre, cim = carry_out(+1), carry_out(-1)

    qh = q * SSM_GROUP
    wc = jnp.stack([wre[0], wim[0], wre[1], wim[1]]).astype(BF16)
    wc = (wc.reshape(N_COMP, nt, GPT, q, SSM_GROUP, SSM_STATE).transpose(1, 3, 2, 4, 0, 5)
          .reshape(nt, q * LANES, N_COMP * SSM_STATE))
    cc = jnp.stack([cre[0], cim[0], cre[1], cim[1]]).astype(BF16)
    cc = (cc.reshape(N_COMP, nt, GPT, SSM_STATE, qh).transpose(1, 0, 2, 3, 4)
          .reshape(nt, N_COMP * GPT * SSM_STATE, qh))
    decay = jnp.stack([pr[0, ..., q], pi[0, ..., q], pr[1, ..., q], pi[1, ..., q]])
    decay = decay.reshape(N_COMP, nt, 1, GPT * SSM_STATE).transpose(1, 0, 2, 3)
    return lag_tile, wc, cc, decay


def _chunk_rows(u_ref):
    nc, _, nb, lanes = u_ref.shape
    return jnp.concatenate([u_ref[:, qq].reshape(nc * nb, lanes) for qq in range(CHUNK)], axis=-1)


SPREAD_COLS = 512


def _spread_into(dst_ref, compact_ref, expand_ref, row_inner, col_inner):
    rows, cols = dst_ref.shape
    rsh, csh = row_inner.bit_length() - 1, col_inner.bit_length() - 1
    assert 1 << rsh == row_inner and 1 << csh == col_inner
    r = lax.broadcasted_iota(jnp.int32, (rows, SPREAD_COLS), 0)
    c = lax.broadcasted_iota(jnp.int32, (rows, SPREAD_COLS), 1)
    r_gi = lax.shift_right_logical(r, rsh) & (GPT - 1)
    for c0 in range(0, cols, SPREAD_COLS):
        y = _dot(compact_ref[...], expand_ref[:, c0:c0 + SPREAD_COLS])
        c_gi = lax.shift_right_logical(c + c0, csh) & (GPT - 1)
        dst_ref[:, c0:c0 + SPREAD_COLS] = jnp.where(r_gi == c_gi, y, 0.0).astype(dst_ref.dtype)


def _s5_inject_kernel(u_ref, wc_ref, exp_ref, s_ref, w_ref):
    @pl.when(pl.program_id(1) == 0)
    def _():
        _spread_into(w_ref, wc_ref, exp_ref, SSM_GROUP, SSM_STATE)

    s = _dot(_chunk_rows(u_ref), w_ref[...])
    width = s_ref.shape[-1]
    for comp in range(N_COMP):
        s_ref[comp] = s[:, comp * width:(comp + 1) * width]


def _s5_scan_kernel(s_ref, a_ref, o_ref, *, batch, n_lat_chunks, n_ctx_chunks):
    backward = pl.program_id(1) == 1
    ar, ai = a_ref[0], a_ref[1]

    def step(i, carry):
        sr, si = carry
        k = i - n_ctx_chunks
        fwd = jnp.where(i < n_ctx_chunks, n_lat_chunks + i, k)
        bwd = jnp.where(i < n_ctx_chunks, n_lat_chunks + n_ctx_chunks - 1 - i, n_lat_chunks - 1 - k)
        rows = pl.ds(pl.multiple_of(jnp.where(backward, bwd, fwd) * batch, batch), batch)
        o_ref[0, rows, :] = sr.astype(o_ref.dtype)
        o_ref[1, rows, :] = si.astype(o_ref.dtype)
        return (ar * sr - ai * si + s_ref[0, rows, :], ar * si + ai * sr + s_ref[1, rows, :])

    z = jnp.zeros((batch, s_ref.shape[-1]), F32)
    lax.fori_loop(0, n_lat_chunks + n_ctx_chunks, step, (z, z))


def _s5_out_kernel(u_ref, sp_ref, lag_ref, cc_ref, exp_ref, d_ref, o_ref, m_ref, c_ref):
    nc, _, nb, lanes = u_ref.shape
    width = sp_ref.shape[-1]

    @pl.when(pl.program_id(1) == 0)
    def _():
        for q_in in range(CHUNK):
            for q_out in range(CHUNK):
                m_ref[q_in * lanes:(q_in + 1) * lanes, q_out * lanes:(q_out + 1) * lanes] = (
                    lag_ref[q_out - q_in + CHUNK - 1])
        _spread_into(c_ref, cc_ref, exp_ref, SSM_STATE, SSM_GROUP)

    y = _dot(_chunk_rows(u_ref), m_ref[...])
    for comp in range(N_COMP):
        y = y + _dot(sp_ref[comp], c_ref[comp * width:(comp + 1) * width, :])
    for qq in range(CHUNK):
        yq = y[:, qq * lanes:(qq + 1) * lanes].reshape(nc, nb, lanes)
        o_ref[:, qq] = jax.nn.gelu(d_ref[...] * u_ref[:, qq].astype(F32) + yq).astype(o_ref.dtype)


def _s5_call(u_ctx, u_lat, d_skip, tables, layer):
    lag_tile, wc, cc, decay = tables
    exp_w = _expand_matrix(N_COMP, SSM_STATE)
    exp_c = _expand_matrix(CHUNK, SSM_GROUP)
    op_rows = CHUNK * LANES
    full2 = lambda a: pl.BlockSpec(a.shape, lambda j, r: (0, 0))
    b, lc, w = u_ctx.shape
    l = u_lat.shape[1]
    nt = w // LANES
    n_lat, n_ctx = l // CHUNK, lc // CHUNK
    n_chunks = n_lat + n_ctx
    rows = n_chunks * b
    sw = GPT * SSM_STATE
    u_t = jnp.concatenate([u_lat, u_ctx], axis=1).transpose(1, 0, 2).reshape(n_chunks, CHUNK, b, w)
    cb = _pick_tile(n_chunks, 36)
    rb = cb * b
    u_spec = pl.BlockSpec((cb, CHUNK, b, LANES), lambda j, r: (r, 0, 0, j))
    op_spec = lambda shape: pl.BlockSpec((None, None) + shape, lambda j, r: (layer, j, 0, 0))
    s_loc = pl.pallas_call(
        _s5_inject_kernel,
        grid=(nt, n_chunks // cb),
        in_specs=[u_spec, op_spec(wc.shape[2:]), full2(exp_w)],
        out_specs=pl.BlockSpec((None, N_COMP, rb, sw), lambda j, r: (j, 0, r, 0)),
        out_shape=jax.ShapeDtypeStruct((nt, N_COMP, rows, sw), F32),
        scratch_shapes=[pltpu.VMEM((op_rows, N_COMP * sw), BF16)],
        compiler_params=_cparams(("parallel", "arbitrary")),
        name="s5_inject",
    )(u_t, wc, exp_w)
    s_prev = pl.pallas_call(
        functools.partial(_s5_scan_kernel, batch=b, n_lat_chunks=n_lat, n_ctx_chunks=n_ctx),
        grid=(nt, 2),
        in_specs=[pl.BlockSpec((None, 2, rows, sw), lambda j, d: (j, d, 0, 0)),
                  pl.BlockSpec((None, None, 2, 1, sw), lambda j, d: (layer, j, d, 0, 0))],
        out_specs=pl.BlockSpec((None, 2, rows, sw), lambda j, d: (j, d, 0, 0)),
        out_shape=jax.ShapeDtypeStruct((nt, N_COMP, rows, sw), BF16),
        compiler_params=_cparams(("parallel", "parallel")),
        name="s5_scan",
    )(s_loc, decay)
    y = pl.pallas_call(
        _s5_out_kernel,
        grid=(nt, n_chunks // cb),
        in_specs=[u_spec,
                  pl.BlockSpec((None, N_COMP, rb, sw), lambda j, r: (j, 0, r, 0)),
                  pl.BlockSpec((None, None) + lag_tile.shape[2:], lambda j, r: (layer, j, 0, 0, 0)),
                  op_spec(cc.shape[2:]), full2(exp_c),
                  pl.BlockSpec((None, 1, LANES), lambda j, r: (layer, 0, j))],
        out_specs=u_spec,
        out_shape=jax.ShapeDtypeStruct((n_chunks, CHUNK, b, w), BF16),
        scratch_shapes=[pltpu.VMEM((op_rows, op_rows), BF16), pltpu.VMEM((N_COMP * sw, op_rows), BF16)],
        compiler_params=_cparams(("parallel", "arbitrary")),
        name="s5_out",
    )(u_t, s_prev, lag_tile, cc, exp_c, d_skip)
    return y.reshape(l + lc, b, w).transpose(1, 0, 2)


def _merge_kernel(*refs, with_router):
    (h_ref, yf_ref, ys_ref, ga_ref, gb_ref, g1_ref, sc2_ref, sh2_ref, n2g_ref,
     wf_ref, wa_ref, wb_ref, wo_ref) = refs[:13]
    rest = refs[13:]
    if with_router:
        wr_ref, br_ref, h1_ref, f_ref, lg_ref = rest
    else:
        h1_ref, f_ref = rest
    ys = ys_ref[...]
    ya = _dot(yf_ref[...], wf_ref[...])
    yb = _dot(ys, wa_ref[...]) * jax.nn.sigmoid(_dot(ys, wb_ref[...]))
    m = (ga_ref[...].astype(F32) * ya + gb_ref[...].astype(F32) * yb).astype(BF16)
    h1 = h_ref[...] + g1_ref[...] * _dot(m, wo_ref[...])
    h1_ref[...] = h1
    f = _rms_mod(h1, n2g_ref[...], sc2_ref[...], sh2_ref[...])
    f_ref[...] = f.astype(f_ref.dtype)
    if with_router:
        lg_ref[...] = _dot3(f, wr_ref[...]) + br_ref[...]


def _merge_call(h, yf, ys_all, ys_row0, ga, gb, mods, g1_row, sc2_row, sh2_row, n2g, wf, wa, wb, wo, layer,
                router, tm):
    b, l, d = h.shape
    w = yf.shape[-1]
    tok = lambda width: pl.BlockSpec((None, tm, width), lambda i, m: (i, m, 0))
    vec = lambda row: pl.BlockSpec((None, 1, d), lambda i, m: (row(i), 0, 0))
    full = lambda a: pl.BlockSpec(a.shape, lambda i, m: (0,) * a.ndim)
    per_layer = lambda a: pl.BlockSpec((None,) + a.shape[1:], lambda i, m: (layer,) + (0,) * (a.ndim - 1))
    blk0 = ys_row0 // tm
    assert blk0 * tm == ys_row0
    args = [h, yf, ys_all, ga, gb, mods, mods, mods, n2g, wf, wa, wb, wo]
    in_specs = [tok(d), tok(w), pl.BlockSpec((None, tm, w), lambda i, m: (i, m + blk0, 0)), tok(d), tok(d),
                vec(g1_row), vec(sc2_row), vec(sh2_row), per_layer(n2g), per_layer(wf), per_layer(wa),
                per_layer(wb), per_layer(wo)]
    out_specs = [tok(d), tok(d)]
    f_dtype = F32 if router is not None else BF16
    out_shape = [jax.ShapeDtypeStruct((b, l, d), F32), jax.ShapeDtypeStruct((b, l, d), f_dtype)]
    if router is not None:
        wr, br = router
        args += [wr, br]
        in_specs += [full(wr), full(br)]
        out_specs.append(tok(ROUTER_PAD))
        out_shape.append(jax.ShapeDtypeStruct((b, l, ROUTER_PAD), F32))
    return pl.pallas_call(
        functools.partial(_merge_kernel, with_router=router is not None),
        grid=(b, l // tm),
        in_specs=in_specs, out_specs=out_specs, out_shape=out_shape,
        compiler_params=_cparams(("parallel", "parallel")),
        name="merge",
    )(*args)


R_E1, R_E2, R_RANK1, R_RANK2, R_W1, R_W2 = range(6)


def _route_kernel(lg_ref, rec_ref, cnt_ref, carry_ref):
    @pl.when(pl.program_id(0) == 0)
    def _():
        carry_ref[...] = jnp.zeros_like(carry_ref)

    lg = lg_ref[...]
    tm = lg.shape[0]
    lane = lax.broadcasted_iota(jnp.int32, lg.shape, 1)
    m1 = jnp.max(lg, axis=-1, keepdims=True)
    i1 = jnp.min(jnp.where(lg == m1, lane, ROUTER_PAD), axis=-1, keepdims=True)
    lg2 = jnp.where(lane == i1, -jnp.inf, lg)
    m2 = jnp.max(lg2, axis=-1, keepdims=True)
    i2 = jnp.min(jnp.where(lg2 == m2, lane, ROUTER_PAD), axis=-1, keepdims=True)
    e = jnp.exp(m2 - m1)
    w1 = 1.0 / (1.0 + e)
    w2 = e * w1

    oh1 = lane == i1
    oh2 = lane == i2
    row = lax.broadcasted_iota(jnp.int32, (tm, tm), 0)
    col = lax.broadcasted_iota(jnp.int32, (tm, tm), 1)
    below = (row > col).astype(BF16)
    p1 = _dot(below, oh1.astype(BF16))
    p2 = _dot(below, oh2.astype(BF16))
    c1 = jnp.sum(oh1.astype(F32), axis=0, keepdims=True)
    c2 = jnp.sum(oh2.astype(F32), axis=0, keepdims=True)
    base = carry_ref[...]
    r1 = jnp.sum(jnp.where(oh1, p1 + base, 0.0), axis=-1, keepdims=True)
    r2 = jnp.sum(jnp.where(oh2, p2 + (base + c1), 0.0), axis=-1, keepdims=True)
    total = base + c1 + c2
    carry_ref[...] = total
    cnt_ref[...] = total

    rec = jnp.zeros_like(lg)
    for slot, val in ((R_E1, i1.astype(F32)), (R_E2, i2.astype(F32)), (R_RANK1, r1), (R_RANK2, r2),
                      (R_W1, w1), (R_W2, w2)):
        rec = jnp.where(lane == slot, val, rec)
    rec_ref[...] = rec


def _route_call(logits, tm):
    t, n = logits.shape
    return pl.pallas_call(
        _route_kernel,
        grid=(t // tm,),
        in_specs=[pl.BlockSpec((tm, n), lambda i: (i, 0))],
        out_specs=[pl.BlockSpec((tm, n), lambda i: (i, 0)), pl.BlockSpec((1, n), lambda i: (0, 0))],
        out_shape=[jax.ShapeDtypeStruct((t, n), F32), jax.ShapeDtypeStruct((1, n), F32)],
        scratch_shapes=[pltpu.VMEM((1, n), F32)],
        compiler_params=_cparams(("arbitrary",)),
        name="route",
    )(logits)


def _dispatch_kernel(pos_ref, fill_ref, f_ref, xs_ref, zero_ref, sem, zsem, *, tile_rows):
    tm = f_ref.shape[0]
    base = pl.program_id(0) * (2 * tm)

    def row_copy(r, k):
        return pltpu.make_async_copy(f_ref.at[pl.ds(r, 1)], xs_ref.at[pl.ds(pos_ref[base + 2 * r + k], 1)], sem)

    def issue(r, carry):
        row_copy(r, 0).start()
        row_copy(r, 1).start()
        return carry

    lax.fori_loop(0, tm, issue, 0, unroll=8)
    for _ in range(2):
        pltpu.make_async_copy(f_ref, xs_ref.at[pl.ds(0, tm)], sem).wait()

    @pl.when(pl.program_id(0) == pl.num_programs(0) - 1)
    def _():
        zero_ref[...] = jnp.zeros_like(zero_ref)

        def zero_row(row):
            return pltpu.make_async_copy(zero_ref.at[pl.ds(0, 1)], xs_ref.at[pl.ds(row, 1)], zsem)

        def zero_tile(tile):
            rows = pl.ds(pl.multiple_of(tile * tile_rows, tile_rows), tile_rows)
            return pltpu.make_async_copy(zero_ref, xs_ref.at[rows], zsem)

        def start(copy_of):
            def body(k, carry):
                copy_of(k).start()
                return carry
            return body

        def wait(copy_of):
            def body(k, carry):
                copy_of(k).wait()
                return carry
            return body

        n_used, n_tiles = fill_ref[2 * N_EXPERTS], xs_ref.shape[0] // tile_rows
        lax.fori_loop(n_used, n_tiles, start(zero_tile), 0)
        for e in range(N_EXPERTS):
            first, count = fill_ref[e], fill_ref[N_EXPERTS + e]
            lax.fori_loop(first, first + count, start(zero_row), 0)
        for e in range(N_EXPERTS):
            first, count = fill_ref[e], fill_ref[N_EXPERTS + e]
            lax.fori_loop(first, first + count, wait(zero_row), 0)
        lax.fori_loop(n_used, n_tiles, wait(zero_tile), 0)


def _dispatch_call(pos, fill, f, n_rows, tile_rows, tm):
    t, d = f.shape
    return pl.pallas_call(
        functools.partial(_dispatch_kernel, tile_rows=tile_rows),
        grid_spec=pltpu.PrefetchScalarGridSpec(
            num_scalar_prefetch=2,
            grid=(t // tm,),
            in_specs=[pl.BlockSpec((tm, d), lambda i, pos, fill: (i, 0))],
            out_specs=pl.BlockSpec(memory_space=pl.ANY),
            scratch_shapes=[pltpu.VMEM((tile_rows, d), f.dtype), pltpu.SemaphoreType.DMA,
                            pltpu.SemaphoreType.DMA]),
        out_shape=jax.ShapeDtypeStruct((n_rows, d), f.dtype),
        compiler_params=pltpu.CompilerParams(dimension_semantics=("arbitrary",), vmem_limit_bytes=VMEM_LIMIT,
                                             disable_bounds_checks=True),
        name="dispatch",
    )(pos, fill, f)


N_WBUF = 3


def _weight_ring(wg_hbm, wu_hbm, wd_hbm, wg_buf, wu_buf, wd_buf, sem, n_steps, weight_set):
    i, j = pl.program_id(0), pl.program_id(1)
    nj = pl.num_programs(1)
    tf = wg_buf.shape[-1]
    g = i * nj + j

    def copies(ci, cj, slot):
        cols = pl.ds(pl.multiple_of(cj * tf, tf), tf)
        s = weight_set(ci)
        return (pltpu.make_async_copy(wg_hbm.at[s, :, cols], wg_buf.at[slot], sem.at[0, slot]),
                pltpu.make_async_copy(wu_hbm.at[s, :, cols], wu_buf.at[slot], sem.at[1, slot]),
                pltpu.make_async_copy(wd_hbm.at[s, cols, :], wd_buf.at[slot], sem.at[2, slot]))

    def start_ahead(ahead):
        ci = lax.div(g + ahead, nj)
        cj = g + ahead - ci * nj
        ci = jnp.minimum(ci, pl.num_programs(0) - 1)

        @pl.when(g + ahead < n_steps)
        def _():
            for c in copies(ci, cj, lax.rem(g + ahead, N_WBUF)):
                c.start()

    @pl.when(g == 0)
    def _():
        for ahead in range(N_WBUF - 1):
            start_ahead(ahead)

    start_ahead(N_WBUF - 1)
    slot = lax.rem(g, N_WBUF)

    @pl.when(g < n_steps)
    def _():
        for c in copies(i, j, slot):
            c.wait()

    return slot


def _gffn_kernel(te_ref, nu_ref, x_ref, wg_hbm, wu_hbm, wd_hbm, o_ref, xb_ref, wg_buf, wu_buf, wd_buf, sem,
                 *, set0):
    j = pl.program_id(1)
    slot = _weight_ring(wg_hbm, wu_hbm, wd_hbm, wg_buf, wu_buf, wd_buf, sem,
                        nu_ref[0] * pl.num_programs(1), lambda ci: set0 + te_ref[ci])

    @pl.when(pl.program_id(0) < nu_ref[0])
    def _():
        @pl.when(j == 0)
        def _():
            xb_ref[...] = x_ref[...].astype(BF16)
            o_ref[...] = jnp.zeros_like(o_ref)

        x = xb_ref[...]
        hid = jax.nn.silu(_dot(x, wg_buf[slot].astype(BF16))) * _dot(x, wu_buf[slot].astype(BF16))
        o_ref[...] += _dot(hid.astype(BF16), wd_buf[slot].astype(BF16))


def _gffn_call(tile_expert, n_used, xs, wg, wu, wd, idx, tm, tf):
    n_rows, d = xs.shape
    n_e, f = wg.shape[1], wg.shape[-1]
    nj = f // tf
    wg, wu, wd = (w.reshape((-1,) + w.shape[2:]) for w in (wg, wu, wd))

    def row_map(i, j, te, nu):
        return jnp.minimum(i, nu[0] - 1), 0

    hbm = pl.BlockSpec(memory_space=pl.ANY)
    return pl.pallas_call(
        functools.partial(_gffn_kernel, set0=idx * n_e),
        grid_spec=pltpu.PrefetchScalarGridSpec(
            num_scalar_prefetch=2,
            grid=(n_rows // tm, nj),
            in_specs=[pl.BlockSpec((tm, d), row_map), hbm, hbm, hbm],
            out_specs=pl.BlockSpec((tm, d), row_map),
            scratch_shapes=[pltpu.VMEM((tm, d), BF16),
                            pltpu.VMEM((N_WBUF, d, tf), wg.dtype), pltpu.VMEM((N_WBUF, d, tf), wu.dtype),
                            pltpu.VMEM((N_WBUF, tf, d), wd.dtype), pltpu.SemaphoreType.DMA((3, N_WBUF))]),
        out_shape=jax.ShapeDtypeStruct((n_rows, d), F32),
        input_output_aliases={2: 0},
        compiler_params=_cparams(("arbitrary", "arbitrary")),
        name="gffn",
    )(tile_expert, n_used, xs, wg, wu, wd)


def _combine_kernel(pos_ref, ys_ref, h_ref, rec_ref, g2_ref, fg_ref, o_ref, ybuf_ref, sem, *, final_norm):
    tm = h_ref.shape[0]
    base = pl.program_id(0) * (2 * tm)

    def row_copy(r, k):
        return pltpu.make_async_copy(ys_ref.at[pl.ds(pos_ref[base + 2 * r + k], 1)],
                                     ybuf_ref.at[k, pl.ds(r, 1)], sem)

    def issue(r, carry):
        row_copy(r, 0).start()
        row_copy(r, 1).start()
        return carry

    lax.fori_loop(0, tm, issue, 0, unroll=8)
    for k in range(2):
        pltpu.make_async_copy(ys_ref.at[pl.ds(0, tm)], ybuf_ref.at[k], sem).wait()
    rec = rec_ref[...]
    y = rec[:, R_W1:R_W1 + 1] * ybuf_ref[0] + rec[:, R_W2:R_W2 + 1] * ybuf_ref[1]
    out = h_ref[...] + g2_ref[...] * y
    if final_norm:
        out = out * lax.rsqrt(jnp.mean(out * out, axis=-1, keepdims=True) + EPS) * fg_ref[...]
    o_ref[...] = out


def _combine_call(pos, ys, h, rec, mods, g2_row, tiles_per_batch, final_g, tm):
    t, d = h.shape
    fg = (final_g if final_g is not None else jnp.ones((d,), F32)).reshape(1, d)
    return pl.pallas_call(
        functools.partial(_combine_kernel, final_norm=final_g is not None),
        grid_spec=pltpu.PrefetchScalarGridSpec(
            num_scalar_prefetch=1,
            grid=(t // tm,),
            in_specs=[pl.BlockSpec(memory_space=pl.ANY),
                      pl.BlockSpec((tm, d), lambda i, pos: (i, 0)),
                      pl.BlockSpec((tm, ROUTER_PAD), lambda i, pos: (i, 0)),
                      pl.BlockSpec((None, 1, d), lambda i, pos: (g2_row(i // tiles_per_batch), 0, 0)),
                      pl.BlockSpec((1, d), lambda i, pos: (0, 0))],
            out_specs=pl.BlockSpec((tm, d), lambda i, pos: (i, 0)),
            scratch_shapes=[pltpu.VMEM((2, tm, d), F32), pltpu.SemaphoreType.DMA]),
        out_shape=jax.ShapeDtypeStruct((t, d), F32),
        compiler_params=pltpu.CompilerParams(dimension_semantics=("arbitrary",), vmem_limit_bytes=VMEM_LIMIT,
                                             disable_bounds_checks=True),
        name="combine",
    )(pos, ys, h, rec, mods, fg)


def _moe_plan(rec, counts, tm, n_tiles):
    cnt = counts[0, :N_EXPERTS].astype(jnp.int32)
    nt = (cnt + (tm - 1)) // tm
    cum = jnp.cumsum(nt)
    start = (cum - nt) * tm
    e = rec[:, R_E1:R_E2 + 1].astype(jnp.int32)
    rank = rec[:, R_RANK1:R_RANK2 + 1].astype(jnp.int32)
    ex = lax.broadcasted_iota(jnp.int32, e.shape + (N_EXPERTS,), 2)
    pos = rank + jnp.sum(jnp.where(e[..., None] == ex, start, 0), axis=-1)
    tile = jnp.arange(n_tiles, dtype=jnp.int32)
    te = jnp.sum((tile[:, None] >= cum[None, :]).astype(jnp.int32), axis=1)
    n_used = cum[-1:]
    last_e = jnp.sum((n_used - 1 >= cum).astype(jnp.int32))
    fill = jnp.concatenate([start + cnt, nt * tm - cnt, n_used])
    return pos.reshape(-1), jnp.minimum(te, last_e), n_used, fill


def _ffn_kernel(x_ref, wg_hbm, wu_hbm, wd_hbm, h_ref, g2_ref, fg_ref, o_ref, acc_ref, wg_buf, wu_buf, wd_buf, sem,
                *, final_norm, idx):
    j = pl.program_id(1)
    slot = _weight_ring(wg_hbm, wu_hbm, wd_hbm, wg_buf, wu_buf, wd_buf, sem,
                        pl.num_programs(0) * pl.num_programs(1), lambda ci: idx)

    @pl.when(j == 0)
    def _():
        acc_ref[...] = jnp.zeros_like(acc_ref)

    x = x_ref[...]
    hid = jax.nn.silu(_dot(x, wg_buf[slot].astype(BF16))) * _dot(x, wu_buf[slot].astype(BF16))
    acc_ref[...] += _dot(hid.astype(BF16), wd_buf[slot].astype(BF16))

    @pl.when(j == pl.num_programs(1) - 1)
    def _():
        out = h_ref[...] + g2_ref[...] * acc_ref[...]
        if final_norm:
            out = out * lax.rsqrt(jnp.mean(out * out, axis=-1, keepdims=True) + EPS) * fg_ref[...]
        o_ref[...] = out


def _ffn_call(x, wg, wu, wd, idx, h, mods, g2_row, tiles_per_batch, final_g, tm, tf):
    t, d = x.shape
    f = wg.shape[-1]
    tok = lambda width: pl.BlockSpec((tm, width), lambda m, j: (m, 0))
    fg = (final_g if final_g is not None else jnp.ones((d,), F32)).reshape(1, d)
    hbm = pl.BlockSpec(memory_space=pl.ANY)
    return pl.pallas_call(
        functools.partial(_ffn_kernel, final_norm=final_g is not None, idx=idx),
        grid=(t // tm, f // tf),
        in_specs=[tok(d), hbm, hbm, hbm,
                  tok(d),
                  pl.BlockSpec((None, 1, d), lambda m, j: (g2_row(m // tiles_per_batch), 0, 0)),
                  pl.BlockSpec((1, d), lambda m, j: (0, 0))],
        out_specs=tok(d),
        out_shape=jax.ShapeDtypeStruct((t, d), F32),
        scratch_shapes=[pltpu.VMEM((tm, d), F32),
                        pltpu.VMEM((N_WBUF, d, tf), wg.dtype), pltpu.VMEM((N_WBUF, d, tf), wu.dtype),
                        pltpu.VMEM((N_WBUF, tf, d), wd.dtype), pltpu.SemaphoreType.DMA((3, N_WBUF))],
        compiler_params=_cparams(("arbitrary", "arbitrary")),
        name="ffn",
    )(x, wg, wu, wd, h, mods, fg)


def _pos_table(n, dim):
    t = np.arange(n)
    r = (t // GRID_W).astype(np.float32)
    col = (t % GRID_W).astype(np.float32)
    quarter = dim // 4
    omega = (1.0 / (POS_BASE ** (np.arange(quarter, dtype=np.float32) / quarter))).astype(np.float32)
    ar = r[:, None] * omega
    ac = col[:, None] * omega
    return np.concatenate([np.sin(ar), np.cos(ar), np.sin(ac), np.cos(ac)], axis=-1).astype(np.float32)


def _pick_tile(n, pref):
    return pref if n % pref == 0 else n


def kernel(x, c, ctx, c_ctx, w_mod, b_mod, norm1_g, norm2_g, w_in, w_four, ssm_a_re, ssm_a_im, ssm_log_dt, ssm_b_re, ssm_b_im, ssm_c_re, ssm_c_im, ssm_d, w_glu_a, w_glu_b, w_out, ffn_w_gate, ffn_w_up, ffn_w_down, moe_w_router, moe_b_router, moe_w_gate, moe_w_up, moe_w_down, final_g):
    b, l, d = x.shape
    lc = ctx.shape[1]
    depth = w_mod.shape[0]
    wf_cols = w_four.shape[1]
    ws_cols = ssm_d.shape[1]
    off_ga = wf_cols + ws_cols
    tm_lat = _pick_tile(l, 512)
    tm_ctx = _pick_tile(lc, 256)

    n_cond = b + 8
    cond = jnp.concatenate([c, jnp.broadcast_to(c_ctx[None], (n_cond - b, d))], axis=0)
    pos = jnp.asarray(_pos_table(l, d))

    mods = _mods_call(cond, w_mod, b_mod)
    w_in_b, wf, wa, wb, wo = (w.astype(BF16) for w in (w_in, w_four, w_glu_a, w_glu_b, w_out))
    n1g, n2g = norm1_g.reshape(depth, 1, d), norm2_g.reshape(depth, 1, d)
    d_skip = ssm_d.reshape(depth, 1, ws_cols)
    tables = jax.vmap(_s5_tables)(ssm_a_re, ssm_a_im, ssm_log_dt, ssm_b_re, ssm_b_im, ssm_c_re, ssm_c_im)

    def mod_rows(layer, which):
        return (lambda i: (layer * n_cond + i) * N_MODS + which,
                lambda i: (layer * n_cond + b) * N_MODS + which)

    h = x
    z = ctx
    for layer in range(depth):
        need_ctx = layer < depth - 1
        sh1, sc1, g1, sh2, sc2, g2 = (mod_rows(layer, k) for k in range(N_MODS))
        widths = (wf_cols, ws_cols, d, d)
        acts = (False, False, True, True)

        outs = _inproj_call(h, pos if layer == 0 else None, mods, sc1[0], sh1[0], n1g, w_in_b, layer, 0,
                            widths, acts, tm_lat)
        if layer == 0:
            h, outs = outs[0], outs[1:]
        pf, ps, ga, gb = outs
        if need_ctx:
            cpf, cps, cga, cgb = _inproj_call(z, None, mods, sc1[1], sh1[1], n1g, w_in_b, layer, 0,
                                              widths, acts, tm_ctx)
        else:
            (cps,) = _inproj_call(z, None, mods, sc1[1], sh1[1], n1g, w_in_b, layer, wf_cols,
                                  (ws_cols,), (False,), tm_ctx)

        ys_all = _s5_call(cps, ps, d_skip, tables, layer)
        yf = _fourier_call(pf, tm_lat)

        moe = layer % 2 == 1
        idx = layer // 2
        router = None
        if moe:
            wr = jnp.pad(moe_w_router[idx], ((0, 0), (0, ROUTER_PAD - N_EXPERTS)))
            br = jnp.pad(moe_b_router[idx][None], ((0, 0), (0, ROUTER_PAD - N_EXPERTS)), constant_values=NEG_BIG)
            router = (wr, br)
        res = _merge_call(h, yf, ys_all, 0, ga, gb, mods, g1[0], sc2[0], sh2[0], n2g, wf, wa, wb, wo, layer,
                          router, tm_lat)
        last = layer == depth - 1
        fin = final_g if last else None
        t = b * l
        if moe:
            h1, f, logits = res
            tm_moe = min(1024, max(128, t // 8))
            n_tiles = 2 * t // tm_moe + N_EXPERTS
            rec, counts = _route_call(logits.reshape(t, ROUTER_PAD), _pick_tile(t, 256))
            pos, tile_expert, n_used, fill = _moe_plan(rec, counts, tm_moe, n_tiles)
            xs = _dispatch_call(pos, fill, f.reshape(t, d), n_tiles * tm_moe, tm_moe, _pick_tile(t, 512))
            ys = _gffn_call(tile_expert, n_used, xs, moe_w_gate, moe_w_up, moe_w_down, idx,
                            tm_moe, _pick_tile(moe_w_gate.shape[-1], FFN_TILE))
            tm_c = _pick_tile(l, 512)
            h = _combine_call(pos, ys, h1.reshape(t, d), rec, mods, g2[0], l // tm_c, fin, tm_c).reshape(b, l, d)
        else:
            h1, f = res
            ffn_w = (ffn_w_gate, ffn_w_up, ffn_w_down, idx)
            tf = _pick_tile(ffn_w_gate.shape[-1], 256)
            tm_f = _pick_tile(l, 1024)
            h = _ffn_call(f.reshape(t, d), *ffn_w, h1.reshape(t, d), mods, g2[0], l // tm_f, fin,
                          tm_f, tf).reshape(b, l, d)

        if need_ctx:
            if moe:
                raise NotImplementedError("context tokens through an expert layer")
            cyf = _fourier_call(cpf, tm_ctx)
            z1, cf = _merge_call(z, cyf, ys_all, l, cga, cgb, mods, g1[1], sc2[1], sh2[1], n2g, wf, wa, wb, wo,
                                 layer, None, tm_ctx)
            z = _ffn_call(cf.reshape(b * lc, d), *ffn_w, z1.reshape(b * lc, d), mods, g2[1], 1, None,
                          _pick_tile(b * lc, 1024), tf).reshape(b, lc, d)
    return h
```
